```python
import math
import jax, jax.numpy as jnp
from jax import lax
import numpy as np

D_MODEL = 1024
BATCH = 32
SEQ = 256
DEPTH = 2
DEC_BATCH = 2
DEC_SEQ = 2048
PAST_LEN = 256

GRID_W = 64
N_EVEN = (DEPTH + 1) // 2
N_ODD = DEPTH // 2
H_A = 4
DK_A = 128
DV_A = 128
W_A = H_A * DK_A
W_B = 512
HY_ORDER = 2
HY_EMB = 33
HY_BANDS = (HY_EMB - 1) // 2
HY_FF = 64
HY_TARGET = 1e-2
HY_FAST = 0.3
HY_SLOW = 1.5
H_C = 4
DH_C = 64
DV_C = 2 * DH_C
W_C = H_C * DV_C
ROPE_BASE = 10000.0
H_D = 4
DK_D = 64
DV_D = 128
W_D = H_D * DV_D
GLA_RANK = 16
GLA_TAU = 16.0
D_FF = 2816
CHUNK = 32
Q_BLOCK = 128
EPS = 1e-6

EVEN_COLS = (W_A, W_A, W_A, W_A, W_A, (1 + HY_ORDER) * W_B)
ODD_COLS = (H_C * 2 * DH_C, H_C * 2 * DH_C, W_C, H_D * DK_D, H_D * DK_D, W_D, W_D, GLA_RANK, GLA_RANK)
D_IN_EVEN = sum(EVEN_COLS)
D_IN_ODD = sum(ODD_COLS)
F32 = jnp.float32

kernel_name = 'hybrid_hgrn2_hyena_diffattn_gla_prefix_step'


def split_cols(x, sizes):
    idx = [int(i) for i in np.cumsum(sizes)[:-1]]
    return jnp.split(x, idx, axis=-1)


def rms_norm(x, g):
    x32 = x.astype(F32)
    y = x32 * lax.rsqrt(jnp.mean(x32 * x32, axis=-1, keepdims=True) + EPS)
    return (y * g.astype(F32)).astype(x.dtype)


def heads(x, n):
    b_, L, w = x.shape
    return x.reshape(b_, L, n, w // n).transpose(0, 2, 1, 3)


def head_rms(o, g):
    b_, h_, L, d = o.shape
    o32 = o.transpose(0, 2, 1, 3).astype(F32)
    y = o32 * lax.rsqrt(jnp.mean(o32 * o32, axis=-1, keepdims=True) + EPS) * g.reshape(h_, d).astype(F32)
    return y.reshape(b_, L, h_ * d).astype(o.dtype)


def dwconv3(x, w, b):
    L = x.shape[1]
    xp = jnp.pad(x, ((0, 0), (1, 1), (0, 0)))
    return xp[:, :L] * w[0] + xp[:, 1:L + 1] * w[1] + xp[:, 2:] * w[2] + b


def chunk_gated_scan(q, k, v, log_f, s0):
    b_, h_, L, dk = q.shape
    dv = v.shape[-1]
    n = L // CHUNK

    def to_chunks(t):
        return jnp.moveaxis(t.astype(F32).reshape(b_, h_, n, CHUNK, t.shape[-1]), 2, 0)

    mask = jnp.tril(jnp.ones((CHUNK, CHUNK), bool))[:, :, None]

    def step(S, inp):
        qc, kc, vc, gc = inp
        cum = jnp.cumsum(gc, axis=2)
        rel = jnp.where(mask, cum[:, :, :, None, :] - cum[:, :, None, :, :], -jnp.inf)
        att = jnp.einsum('bhtk,bhsk,bhtsk->bhts', qc, kc, jnp.exp(rel))
        o = jnp.einsum('bhts,bhsv->bhtv', att, vc) + jnp.einsum('bhtk,bhkv->bhtv', qc * jnp.exp(cum), S)
        last = cum[:, :, -1:, :]
        S = jnp.exp(last[:, :, 0, :, None]) * S + jnp.einsum('bhsk,bhsv->bhkv', kc * jnp.exp(last - cum), vc)
        return S, o

    s_fin, o = lax.scan(step, s0.astype(F32), (to_chunks(q), to_chunks(k), to_chunks(v), to_chunks(log_f)))
    o = jnp.moveaxis(o, 0, 2).reshape(b_, h_, L, dv)
    return o.astype(v.dtype), s_fin.astype(v.dtype)


def bidir_scan(q, k_f, k_b, v, logf_f, logf_b, s0):
    o_f, s_f = chunk_gated_scan(q, k_f, v, logf_f, s0[:, 0])
    flip = lambda t: jnp.flip(t, axis=2)
    o_b, s_b = chunk_gated_scan(flip(q), flip(k_b), flip(v), flip(logf_b), s0[:, 1])
    return o_f + flip(o_b), jnp.stack([s_f, s_b], axis=1)


def hyena_filters(L, w1, b1, w2, b2, w3, freq):
    t = jnp.linspace(0.0, 1.0, L, dtype=F32)[:, None]
    w = 2.0 * math.pi * jnp.arange(L, dtype=F32)[:, None] / L
    fb = jnp.linspace(1e-4, HY_BANDS - 1, HY_BANDS, dtype=F32)[None]
    z = jnp.concatenate([t, jnp.cos(fb * w), -jnp.sin(fb * w)], axis=-1)
    h = jnp.sin(freq.astype(F32) * (z @ w1.astype(F32) + b1.astype(F32)))
    h = jnp.sin(freq.astype(F32) * (h @ w2.astype(F32) + b2.astype(F32)))
    h = (h @ w3.astype(F32)).reshape(L, HY_ORDER, 2, W_B)
    max_decay = math.log(HY_TARGET) / HY_FAST
    min_decay = math.log(HY_TARGET) / HY_SLOW
    deltas = jnp.abs(jnp.linspace(min_decay, max_decay, W_B, dtype=F32))
    window = jnp.exp(-t * deltas[None])
    return (h * window[:, None, None, :]).transpose(1, 2, 0, 3)


def two_sided_fftconv(u, filt):
    L = u.shape[1]
    kern = jnp.concatenate([filt[0], jnp.zeros((1, filt.shape[-1]), F32), jnp.flip(filt[1, 1:], axis=0)], axis=0)
    U = jnp.fft.rfft(u.astype(F32), n=2 * L, axis=1)
    K = jnp.fft.rfft(kern, axis=0)
    return jnp.fft.irfft(U * K[None], n=2 * L, axis=1)[:, :L].astype(u.dtype)


def rope_2d(x):
    L = x.shape[2]
    rows = L // GRID_W
    row = jnp.repeat(jnp.arange(rows), GRID_W).astype(F32)
    col = jnp.tile(jnp.arange(GRID_W), rows).astype(F32)
    half = DH_C // 2
    inv = ROPE_BASE ** (-jnp.arange(0, half, 2, dtype=F32) / half)

    def rot(t, pos):
        ang = pos[:, None] * inv[None]
        cos = jnp.cos(ang)[:, None, :]
        sin = jnp.sin(ang)[:, None, :]
        t1, t2 = jnp.split(t, 2, axis=-1)
        return jnp.concatenate([t1 * cos - t2 * sin, t1 * sin + t2 * cos], axis=-1)

    xr, xc = jnp.split(x.astype(F32), 2, axis=-1)
    return jnp.concatenate([rot(xr, row), rot(xc, col)], axis=-1).astype(x.dtype)


def diff_attention(q, keys, vals, lam):
    b_, h_, Lq = q.shape[:3]
    nb = Lq // Q_BLOCK
    qb = q.reshape(b_, h_, nb, Q_BLOCK, 2, DH_C).transpose(2, 0, 1, 3, 4, 5)
    scale = DH_C ** -0.5

    def block(qi):
        s = jnp.einsum('bhqpd,bhkpd->bhpqk', qi, keys).astype(F32) * scale
        p = jax.nn.softmax(s, axis=-1)
        w = p[:, :, 0] - lam * p[:, :, 1]
        return jnp.einsum('bhqk,bhkv->bhqv', w.astype(vals.dtype), vals)

    o = lax.map(block, qb)
    return o.transpose(1, 2, 0, 3, 4).reshape(b_, h_, Lq, vals.shape[-1])


def even_mixer(h, s0, p):
    L = h.shape[1]
    q, ff, fb, i, g, hy = split_cols(h @ p['w_in'], EVEN_COLS)
    lb = p['lb']
    f_f = lb[0] + (1.0 - lb[0]) * jax.nn.sigmoid(ff.astype(F32))
    f_b = lb[1] + (1.0 - lb[1]) * jax.nn.sigmoid(fb.astype(F32))
    qh = heads(jax.nn.silu(q) * DK_A ** -0.5, H_A)
    o_a, s_new = bidir_scan(qh, heads(1.0 - f_f, H_A), heads(1.0 - f_b, H_A), heads(i, H_A),
                            heads(jnp.log(f_f), H_A), heads(jnp.log(f_b), H_A), s0)
    out_a = head_rms(o_a, p['hgrn_norm']) * jax.nn.silu(g)
    hy = dwconv3(hy, p['hy_conv_w'], p['hy_conv_b'])
    v, x1, x2 = jnp.split(hy, 1 + HY_ORDER, axis=-1)
    filt = hyena_filters(L, p['hy_w1'], p['hy_b1'], p['hy_w2'], p['hy_b2'], p['hy_w3'], p['hy_freq'])
    z = v
    for o_idx, gate in enumerate((x1, x2)):
        z = gate * (two_sided_fftconv(z, filt[o_idx]) + z * p['hy_d'][o_idx])
    return jnp.concatenate([out_a, z], axis=-1) @ p['w_out'], s_new


def odd_mixer(h, s0, ctx_k, ctx_v, l, p):
    b_, L, _ = h.shape
    cq, ck, cv, dq, dk, dv, dg, da_f, da_b = split_cols(h @ p['w_in'], ODD_COLS)
    q = cq.reshape(b_, L, H_C, 2, DH_C).transpose(0, 2, 1, 3, 4)
    k = ck.reshape(b_, L, H_C, 2, DH_C).transpose(0, 2, 1, 3, 4)
    v = heads(cv, H_C)
    if ctx_k is None:
        keys, vals = k, v
        cache = (k.reshape(b_, H_C, L, 2 * DH_C), v)
    else:
        q = rope_2d(q)
        keys = jnp.concatenate([ctx_k.reshape(b_, H_C, -1, 2, DH_C), rope_2d(k)], axis=2)
        vals = jnp.concatenate([ctx_v, v], axis=2)
        cache = None
    lam_init = 0.8 - 0.6 * math.exp(-0.3 * l)
    lp = p['diff_lambda'].astype(F32)
    lam = jnp.exp(jnp.sum(lp[0] * lp[1])) - jnp.exp(jnp.sum(lp[2] * lp[3])) + lam_init
    o_c = diff_attention(q, keys, vals, lam)
    out_c = head_rms(o_c, p['diff_norm']) * (1.0 - lam_init)
    qd = heads(dq, H_D) * DK_D ** -0.5
    kd = heads(dk, H_D)
    vd = heads(dv, H_D)
    la_f = jax.nn.log_sigmoid((da_f @ p['gla_aw'][0] + p['gla_ab'][0]).astype(F32)) / GLA_TAU
    la_b = jax.nn.log_sigmoid((da_b @ p['gla_aw'][1] + p['gla_ab'][1]).astype(F32)) / GLA_TAU
    o_d, s_new = bidir_scan(qd, kd, kd, vd, heads(la_f, H_D), heads(la_b, H_D), s0)
    out_d = head_rms(o_d, p['gla_norm']) * jax.nn.silu(dg)
    return jnp.concatenate([out_c, out_d], axis=-1) @ p['w_out'], cache, s_new


def conv_ffn(h, up, cw, cb, down):
    u = dwconv3(h @ up, cw, cb)
    a, g = jnp.split(u, 2, axis=-1)
    return (jax.nn.silu(g) * a) @ down


def modulate(x, mod, j):
    return x * (1.0 + mod[:, :, 3 * j + 1]) + mod[:, :, 3 * j]


def setup_inputs(seed: int = 0) -> dict:
    key = jax.random.key(seed)
    ks = iter(jax.random.split(key, 40))

    def nrm(shape, scale=1.0):
        return jax.random.normal(next(ks), shape, F32) * scale

    def gain(shape):
        return 1.0 + nrm(shape, 0.05)

    D = D_MODEL
    return {
        'x_prompt': nrm((BATCH, SEQ, D)),
        'x_sample': nrm((DEC_BATCH, DEC_SEQ, D)),
        'state_hgrn': nrm((DEC_BATCH, N_EVEN, 2, H_A, DK_A, DV_A), 0.5),
        'cache_diff_k': nrm((DEC_BATCH, N_ODD, H_C, PAST_LEN, 2 * DH_C)),
        'cache_diff_v': nrm((DEC_BATCH, N_ODD, H_C, PAST_LEN, DV_C)),
        'state_gla': nrm((DEC_BATCH, N_ODD, 2, H_D, DK_D, DV_D), 0.5),
        'c': nrm((DEC_BATCH, D)),
        'c_ctx': nrm((D,)),
        'ada_w': nrm((DEPTH, D, 6 * D), 0.5 * D ** -0.5),
        'ada_b': nrm((DEPTH, 6 * D), 0.01),
        'norm_g': gain((DEPTH, 4, D)),
        'ffn_up': nrm((DEPTH, D, 2 * D_FF), D ** -0.5),
        'ffn_conv_w': nrm((DEPTH, 3, 2 * D_FF), 3 ** -0.5),
        'ffn_conv_b': nrm((DEPTH, 2 * D_FF), 0.01),
        'ffn_down': nrm((DEPTH, D_FF, D), D_FF ** -0.5),
        'w_in_even': nrm((N_EVEN, D, D_IN_EVEN), D ** -0.5),
        'w_out_even': nrm((N_EVEN, W_A + W_B, D), (W_A + W_B) ** -0.5),
        'hgrn_lb': nrm((DEPTH + 1, 2, W_A), 0.5),
        'hgrn_norm': gain((N_EVEN, W_A)),
        'hy_conv_w': nrm((N_EVEN, 3, (1 + HY_ORDER) * W_B), 3 ** -0.5),
        'hy_conv_b': nrm((N_EVEN, (1 + HY_ORDER) * W_B), 0.01),
        'hy_w1': nrm((N_EVEN, HY_EMB, HY_FF), HY_EMB ** -0.5),
        'hy_b1': nrm((N_EVEN, HY_FF), 0.1),
        'hy_w2': nrm((N_EVEN, HY_FF, HY_FF), HY_FF ** -0.5),
        'hy_b2': nrm((N_EVEN, HY_FF), 0.1),
        'hy_w3': nrm((N_EVEN, HY_FF, HY_ORDER * 2 * W_B), 0.1 * HY_FF ** -0.5),
        'hy_freq': 1.0 + nrm((N_EVEN, HY_FF), 0.1),
        'hy_d': nrm((N_EVEN, HY_ORDER, W_B), 0.5),
        'w_in_odd': nrm((N_ODD, D, D_IN_ODD), D ** -0.5),
        'w_out_odd': nrm((N_ODD, W_C + W_D, D), (W_C + W_D) ** -0.5),
        'diff_lambda': nrm((N_ODD, 4, DH_C), 0.1),
        'diff_norm': gain((N_ODD, W_C)),
        'gla_aw': nrm((N_ODD, 2, GLA_RANK, H_D * DK_D), GLA_RANK ** -0.5),
        'gla_ab': nrm((N_ODD, 2, H_D * DK_D), 0.01),
        'gla_norm': gain((N_ODD, W_D)),
    }


def reference(x_prompt, x_sample, state_hgrn, cache_diff_k, cache_diff_v, state_gla, c, c_ctx,
              ada_w, ada_b, norm_g, ffn_up, ffn_conv_w, ffn_conv_b, ffn_down,
              w_in_even, w_out_even, hgrn_lb, hgrn_norm, hy_conv_w, hy_conv_b,
              hy_w1, hy_b1, hy_w2, hy_b2, hy_w3, hy_freq, hy_d,
              w_in_odd, w_out_odd, diff_lambda, diff_norm, gla_aw, gla_ab, gla_norm):
    lb_all = jnp.cumsum(jax.nn.softmax(hgrn_lb.astype(F32), axis=0), axis=0)
    yp, ys = x_prompt, x_sample
    hg_states, cache_ks, cache_vs, gla_states = [], [], [], []
    for l in range(DEPTH):
        mod_p = (jax.nn.silu(c_ctx) @ ada_w[l] + ada_b[l]).reshape(-1, 1, 6, D_MODEL)
        mod_s = (jax.nn.silu(c) @ ada_w[l] + ada_b[l]).reshape(-1, 1, 6, D_MODEL)
        hp = modulate(rms_norm(yp, norm_g[l, 0]), mod_p, 0)
        hs = modulate(rms_norm(ys, norm_g[l, 0]), mod_s, 0)
        if l % 2 == 0:
            e = l // 2
            p = {'w_in': w_in_even[e], 'w_out': w_out_even[e], 'lb': lb_all[l], 'hgrn_norm': hgrn_norm[e],
                 'hy_conv_w': hy_conv_w[e], 'hy_conv_b': hy_conv_b[e], 'hy_w1': hy_w1[e], 'hy_b1': hy_b1[e],
                 'hy_w2': hy_w2[e], 'hy_b2': hy_b2[e], 'hy_w3': hy_w3[e], 'hy_freq': hy_freq[e], 'hy_d': hy_d[e]}
            zero = jnp.zeros((yp.shape[0], 2, H_A, DK_A, DV_A), yp.dtype)
            mp, st = even_mixer(hp, zero, p)
            ms, _ = even_mixer(hs, state_hgrn[:, e], p)
            hg_states.append(st)
        else:
            o = l // 2
            p = {'w_in': w_in_odd[o], 'w_out': w_out_odd[o], 'diff_lambda': diff_lambda[o],
                 'diff_norm': diff_norm[o], 'gla_aw': gla_aw[o], 'gla_ab': gla_ab[o], 'gla_norm': gla_norm[o]}
            zero = jnp.zeros((yp.shape[0], 2, H_D, DK_D, DV_D), yp.dtype)
            mp, (kc, vc), st = odd_mixer(hp, zero, None, None, l, p)
            ms, _, _ = odd_mixer(hs, state_gla[:, o], cache_diff_k[:, o], cache_diff_v[:, o], l, p)
            cache_ks.append(kc)
            cache_vs.append(vc)
            gla_states.append(st)
        yp = yp + mod_p[:, :, 2] * rms_norm(mp, norm_g[l, 1])
        ys = ys + mod_s[:, :, 2] * rms_norm(ms, norm_g[l, 1])
        hp = modulate(rms_norm(yp, norm_g[l, 2]), mod_p, 1)
        hs = modulate(rms_norm(ys, norm_g[l, 2]), mod_s, 1)
        yp = yp + mod_p[:, :, 5] * rms_norm(conv_ffn(hp, ffn_up[l], ffn_conv_w[l], ffn_conv_b[l], ffn_down[l]), norm_g[l, 3])
        ys = ys + mod_s[:, :, 5] * rms_norm(conv_ffn(hs, ffn_up[l], ffn_conv_w[l], ffn_conv_b[l], ffn_down[l]), norm_g[l, 3])
    new_state_hgrn = jnp.stack(hg_states, axis=1)
    new_cache_diff_k = jnp.stack(cache_ks, axis=1)
    new_cache_diff_v = jnp.stack(cache_vs, axis=1)
    new_state_gla = jnp.stack(gla_states, axis=1)
    return (yp, ys, new_state_hgrn, new_cache_diff_k, new_cache_diff_v, new_state_gla)
```

```python
import functools
import math

import jax
import jax.numpy as jnp
import numpy as np
from jax import lax
from jax.experimental import pallas as pl
from jax.experimental.pallas import tpu as pltpu

F32 = jnp.float32
BF16 = jnp.bfloat16
HIGHEST = lax.Precision.HIGHEST

D_MODEL = 1024
DEPTH = 2
GRID_W = 64
N_HEADS = 4
HEAD_W = 128
MIX_W = N_HEADS * HEAD_W
W_B = 512
HY_ORDER = 2
HY_EMB = 33
HY_BANDS = (HY_EMB - 1) // 2
HY_FF = 64
HY_TARGET = 1e-2
HY_FAST = 0.3
HY_SLOW = 1.5
DH_C = 64
DK_D = 64
GLA_RANK = 16
GLA_TAU = 16.0
ROPE_BASE = 10000.0
D_FF = 2816
EPS = 1e-6

LANES = 128
SUBLANES = 8
VMEM_LIMIT = 56 * 1024 * 1024
SCAN_CHUNK = 64
SCAN_ROWS = 256
NT_DIMS = (((1,), (1,)), ((), ()))


def _params(*sem):
    return pltpu.CompilerParams(dimension_semantics=sem, vmem_limit_bytes=VMEM_LIMIT)


def _silu(x):
    return x * (1.0 / (1.0 + jnp.exp(-x)))


def _rms(x, g):
    return x * lax.rsqrt(jnp.mean(x * x, axis=-1, keepdims=True) + EPS) * g


def _ada_kernel(c_ref, w_ref, b_ref, o_ref):
    s = _silu(c_ref[...])
    o_ref[...] = jnp.dot(s, w_ref[...], preferred_element_type=F32, precision=HIGHEST) + b_ref[...]


def _ada_mod(cvec, ada_w, ada_b):
    n = ada_w.shape[-1]
    tn = 1536
    return pl.pallas_call(
        _ada_kernel,
        grid=(DEPTH, n // tn),
        in_specs=[pl.BlockSpec((SUBLANES, D_MODEL), lambda l, j: (0, 0)),
                  pl.BlockSpec((None, D_MODEL, tn), lambda l, j: (l, 0, j)),
                  pl.BlockSpec((None, 1, tn), lambda l, j: (l, 0, j))],
        out_specs=pl.BlockSpec((None, SUBLANES, tn), lambda l, j: (l, 0, j)),
        out_shape=jax.ShapeDtypeStruct((DEPTH, SUBLANES, n), F32),
        compiler_params=_params("arbitrary", "arbitrary"),
    )(cvec, ada_w, ada_b.reshape(DEPTH, 1, n))


def _normmod_matmul_kernel(x_ref, g_ref, sh_ref, sc_ref, w_ref, o_ref):
    h = _rms(x_ref[...], g_ref[...]) * (1.0 + sc_ref[...]) + sh_ref[...]
    o_ref[...] = jnp.dot(h.astype(BF16), w_ref[...], preferred_element_type=F32)


def _mod_index(n_mod):
    return (lambda b, i: (b, 0, 0)) if n_mod > 1 else (lambda b, i: (0, 0, 0))


def _normmod_matmul(x, g, shift, scale, w, tm):
    b, l, d = x.shape
    n = w.shape[1]
    mod_spec = pl.BlockSpec((None, 1, d), _mod_index(shift.shape[0]))
    return pl.pallas_call(
        _normmod_matmul_kernel,
        grid=(b, l // tm),
        in_specs=[pl.BlockSpec((None, tm, d), lambda b, i: (b, i, 0)),
                  pl.BlockSpec((1, d), lambda b, i: (0, 0)),
                  mod_spec, mod_spec,
                  pl.BlockSpec((d, n), lambda b, i: (0, 0))],
        out_specs=pl.BlockSpec((None, tm, n), lambda b, i: (b, i, 0)),
        out_shape=jax.ShapeDtypeStruct((b, l, n), F32),
        compiler_params=_params("arbitrary", "arbitrary"),
    )(x, g, shift, scale, w)


def _out_proj_kernel(a_ref, b_ref, w_ref, y_ref, gate_ref, g_ref, o_ref):
    half = a_ref.shape[-1]
    m = jnp.dot(a_ref[...].astype(BF16), w_ref[:half, :], preferred_element_type=F32)
    m = m + jnp.dot(b_ref[...].astype(BF16), w_ref[half:, :], preferred_element_type=F32)
    o_ref[...] = y_ref[...] + gate_ref[...] * _rms(m, g_ref[...])


def _out_proj(a, bm, w, y, gate, g, tm):
    b, l, d = y.shape
    wa = a.shape[-1]
    return pl.pallas_call(
        _out_proj_kernel,
        grid=(b, l // tm),
        in_specs=[pl.BlockSpec((None, tm, wa), lambda b, i: (b, i, 0)),
                  pl.BlockSpec((None, tm, wa), lambda b, i: (b, i, 0)),
                  pl.BlockSpec((2 * wa, d), lambda b, i: (0, 0)),
                  pl.BlockSpec((None, tm, d), lambda b, i: (b, i, 0)),
                  pl.BlockSpec((None, 1, d), _mod_index(gate.shape[0])),
                  pl.BlockSpec((1, d), lambda b, i: (0, 0))],
        out_specs=pl.BlockSpec((None, tm, d), lambda b, i: (b, i, 0)),
        out_shape=jax.ShapeDtypeStruct((b, l, d), F32),
        compiler_params=_params("arbitrary", "arbitrary"),
    )(a, bm, w, y, gate, g)


def _ffn_kernel(x_ref, xp_ref, xn_ref, g2_ref, sh_ref, sc_ref, ua_ref, ug_ref, cwa_ref, cwg_ref,
                cba_ref, cbg_ref, dn_ref, gate_ref, g3_ref, o_ref, h_scr, acc_scr, *, seq_len):
    i = pl.program_id(1)
    f = pl.program_id(2)
    tm = x_ref.shape[0]
    halo = xp_ref.shape[0]

    @pl.when(f == 0)
    def _():
        def nm(x):
            return (_rms(x, g2_ref[...]) * (1.0 + sc_ref[...]) + sh_ref[...]).astype(BF16)
        h_scr[0:halo, :] = nm(xp_ref[...])
        h_scr[halo:halo + tm, :] = nm(x_ref[...])
        h_scr[halo + tm:, :] = nm(xn_ref[...])
        acc_scr[...] = jnp.zeros_like(acc_scr)

    rows = tm + 2 * halo
    pos = (i * tm + lax.broadcasted_iota(jnp.int32, (tm, 1), 0)) % seq_len
    has_prev = pos != 0
    has_next = pos != seq_len - 1

    def conv(u_ref, cw_ref, cb_ref):
        u = jnp.dot(h_scr[...], u_ref[...], preferred_element_type=F32)
        up = pltpu.roll(u, 1, 0)[halo:halo + tm]
        un = pltpu.roll(u, rows - 1, 0)[halo:halo + tm]
        uc = u[halo:halo + tm]
        return (jnp.where(has_prev, up, 0.0) * cw_ref[0:1, :] + uc * cw_ref[1:2, :]
                + jnp.where(has_next, un, 0.0) * cw_ref[2:3, :] + cb_ref[...])

    a = conv(ua_ref, cwa_ref, cba_ref)
    gt = conv(ug_ref, cwg_ref, cbg_ref)
    act = (_silu(gt) * a).astype(BF16)
    acc_scr[...] += jnp.dot(act, dn_ref[...], preferred_element_type=F32)

    @pl.when(f == pl.num_programs(2) - 1)
    def _():
        o_ref[...] = x_ref[...] + gate_ref[...] * _rms(acc_scr[...], g3_ref[...])


def _conv_ffn(y, g2, shift, scale, up, cw, cb, down, gate, g3, seq_len, tm, tf):
    b, l, d = y.shape
    nf = D_FF // tf
    halo = SUBLANES
    hb = tm // halo
    last_hb = l // halo - 1
    mod_spec = pl.BlockSpec((None, 1, d), (lambda b, i, f: (b, 0, 0)) if shift.shape[0] > 1 else (lambda b, i, f: (0, 0, 0)))
    vec = lambda off: pl.BlockSpec((1, tf), lambda b, i, f: (0, off + f))
    kern = functools.partial(_ffn_kernel, seq_len=seq_len)
    return pl.pallas_call(
        kern,
        grid=(b, l // tm, nf),
        in_specs=[pl.BlockSpec((None, tm, d), lambda b, i, f: (b, i, 0)),
                  pl.BlockSpec((None, halo, d), lambda b, i, f: (b, jnp.maximum(i * hb - 1, 0), 0)),
                  pl.BlockSpec((None, halo, d), lambda b, i, f: (b, jnp.minimum((i + 1) * hb, last_hb), 0)),
                  pl.BlockSpec((1, d), lambda b, i, f: (0, 0)),
                  mod_spec, mod_spec,
                  pl.BlockSpec((d, tf), lambda b, i, f: (0, f)),
                  pl.BlockSpec((d, tf), lambda b, i, f: (0, nf + f)),
                  pl.BlockSpec((3, tf), lambda b, i, f: (0, f)),
                  pl.BlockSpec((3, tf), lambda b, i, f: (0, nf + f)),
                  vec(0), vec(nf),
                  pl.BlockSpec((tf, d), lambda b, i, f: (f, 0)),
                  mod_spec,
                  pl.BlockSpec((1, d), lambda b, i, f: (0, 0))],
        out_specs=pl.BlockSpec((None, tm, d), lambda b, i, f: (b, i, 0)),
        out_shape=jax.ShapeDtypeStruct((b, l, d), F32),
        scratch_shapes=[pltpu.VMEM((tm + 2 * halo, d), BF16), pltpu.VMEM((tm, d), F32)],
        compiler_params=_params("arbitrary", "arbitrary", "arbitrary"),
    )(y, y, y, g2, shift, scale, up, up, cw, cw, cb, cb, down, gate, g3)


def _scan_tables(c):
    nlev = int(math.log2(c))
    t = np.arange(c)[:, None]
    r = np.arange(c)[None, :]
    blocks_f = [(r <= t), (r > t)]
    masks_f = [np.eye(c, dtype=bool)]
    for lev in range(1, nlev + 1):
        bsz = 2 ** lev
        mid = (t // bsz) * bsz + bsz // 2
        upper = t >= mid
        blocks_f.append(np.where(upper, (r >= mid) & (r <= t), (r > t) & (r < mid)))
        same = (t // bsz) == (r // bsz)
        masks_f.append(same & upper & (r < mid))
    a_f = np.concatenate(blocks_f, axis=0).astype(np.float32)
    m_f = np.stack(masks_f).astype(np.float32)
    rev = np.arange(c)[::-1]
    a_b = np.concatenate([blk[rev][:, rev] for blk in blocks_f], axis=0).astype(np.float32)
    m_b = np.stack([m[rev][:, rev] for m in masks_f]).astype(np.float32)
    dup = lambda a: jnp.asarray(np.concatenate([a, a], axis=1), dtype=BF16)
    return dup(a_f), dup(a_b), jnp.asarray(m_f), jnp.asarray(m_b)


def _scan_chunk(q, k, v, lg, a_ref, m_ref, st_ref, last_row):
    c = q.shape[0]
    nlev = m_ref.shape[0] - 1
    hi = lg.astype(BF16)
    mid = (lg - hi.astype(F32)).astype(BF16)
    x = jnp.dot(a_ref[...], jnp.concatenate([hi, mid], axis=0), preferred_element_type=F32)
    e_cum = jnp.exp(x[0:c])
    e_rest = jnp.exp(x[c:2 * c])
    d_last = e_cum[last_row:last_row + 1, :]
    qe = (q * e_cum).astype(BF16)
    kd = (k * e_rest).astype(BF16)
    ql = [q.astype(BF16)]
    kl = [k.astype(BF16)]
    for lev in range(nlev):
        e = jnp.exp(x[(2 + lev) * c:(3 + lev) * c])
        ql.append((q * e).astype(BF16))
        kl.append((k * e).astype(BF16))
    outs = []
    for h in range(N_HEADS):
        sl = slice(h * HEAD_W, (h + 1) * HEAD_W)
        att = m_ref[0] * lax.dot_general(ql[0][:, sl], kl[0][:, sl], NT_DIMS, preferred_element_type=F32)
        for lev in range(1, nlev + 1):
            att = att + m_ref[lev] * lax.dot_general(ql[lev][:, sl], kl[lev][:, sl], NT_DIMS,
                                                     preferred_element_type=F32)
        st = st_ref[h]
        vh = v[:, sl]
        o_h = jnp.dot(att.astype(BF16), vh.astype(BF16), preferred_element_type=F32)
        o_h = o_h + lax.dot_general(qe[:, sl], st.astype(BF16), NT_DIMS, preferred_element_type=F32)
        st_ref[h] = st * d_last[:, sl] + jnp.dot(vh.T.astype(BF16), kd[:, sl], preferred_element_type=F32)
        outs.append(o_h)
    return jnp.concatenate(outs, axis=1)


def _bidir_scan_body(load_fwd, load_bwd, g_ref, s0_ref, norm_ref, af_ref, ab_ref, mf_ref, mb_ref,
                     o_ref, s_out_ref, of_scr, ob_scr, stf_scr, stb_scr, seq_len):
    c = SCAN_CHUNK
    j = pl.program_id(1)
    nblk = pl.num_programs(1)
    n = SCAN_ROWS // c

    @pl.when(j == 0)
    def _():
        for h in range(N_HEADS):
            if s0_ref is None:
                stf_scr[h] = jnp.zeros((HEAD_W, HEAD_W), F32)
                stb_scr[h] = jnp.zeros((HEAD_W, HEAD_W), F32)
            else:
                stf_scr[h] = s0_ref[0, h].T
                stb_scr[h] = s0_ref[1, h].T

    base_f = j * SCAN_ROWS
    base_b = (nblk - 1 - j) * SCAN_ROWS

    def step(i, carry):
        rf = pl.multiple_of(i * c, c)
        rb = pl.multiple_of((n - 1 - i) * c, c)
        q, k, v, lg = load_fwd(rf)
        of_scr[pl.ds(pl.multiple_of(base_f + rf, c), c), :] = _scan_chunk(q, k, v, lg, af_ref, mf_ref, stf_scr, c - 1)
        q, k, v, lg = load_bwd(rb)
        ob_scr[pl.ds(pl.multiple_of(base_b + rb, c), c), :] = _scan_chunk(q, k, v, lg, ab_ref, mb_ref, stb_scr, 0)
        return carry

    lax.fori_loop(0, n, step, 0)

    @pl.when(j == nblk - 1)
    def _():
        def fin(jj, carry):
            r0 = pl.multiple_of(jj * SCAN_ROWS, SCAN_ROWS)
            o = of_scr[pl.ds(r0, SCAN_ROWS), :] + ob_scr[pl.ds(r0, SCAN_ROWS), :]
            parts = []
            for h in range(N_HEADS):
                sl = slice(h * HEAD_W, (h + 1) * HEAD_W)
                parts.append(_rms(o[:, sl], norm_ref[:, sl]))
            o_ref[pl.ds(r0, SCAN_ROWS), :] = jnp.concatenate(parts, axis=1) * _silu(g_ref[pl.ds(r0, SCAN_ROWS), :])
            return carry

        lax.fori_loop(0, seq_len // SCAN_ROWS, fin, 0)
        for h in range(N_HEADS):
            s_out_ref[0, h] = stf_scr[h].T
            s_out_ref[1, h] = stb_scr[h].T


def _hgrn_kernel(*refs, layer, has_s0, seq_len):
    qf_ref, ff_ref, if_ref, qb_ref, fb_ref, ib_ref, g_ref, lbraw_ref, norm_ref, af_ref, ab_ref, mf_ref, mb_ref = refs[:13]
    s0_ref = refs[13] if has_s0 else None
    o_ref, s_out_ref, of_scr, ob_scr, stf_scr, stb_scr = refs[13 + has_s0:]
    c = SCAN_CHUNK
    raw = lbraw_ref[...]
    ex = jnp.exp(raw - jnp.max(raw, axis=0, keepdims=True))
    sm = ex / jnp.sum(ex, axis=0, keepdims=True)
    lb = sm[0]
    for j in range(1, layer + 1):
        lb = lb + sm[j]

    def load(q_ref, f_ref, i_ref, lb_row):
        def fn(r0):
            q = _silu(q_ref[pl.ds(r0, c), :]) * (HEAD_W ** -0.5)
            sig = 1.0 / (1.0 + jnp.exp(-f_ref[pl.ds(r0, c), :]))
            f = lb_row + (1.0 - lb_row) * sig
            return q, 1.0 - f, i_ref[pl.ds(r0, c), :], jnp.log(f)
        return fn

    _bidir_scan_body(load(qf_ref, ff_ref, if_ref, lb[0:1]), load(qb_ref, fb_ref, ib_ref, lb[1:2]), g_ref,
                     s0_ref, norm_ref, af_ref, ab_ref, mf_ref, mb_ref,
                     o_ref, s_out_ref, of_scr, ob_scr, stf_scr, stb_scr, seq_len)


def _gla_kernel(*refs, has_s0, seq_len):
    fwd_refs, bwd_refs = refs[0:4], refs[4:8]
    g_ref, aw_ref, ab_ref, norm_ref, af_ref, abk_ref, mf_ref, mb_ref = refs[8:16]
    s0_ref = refs[16] if has_s0 else None
    o_ref, s_out_ref, of_scr, ob_scr, stf_scr, stb_scr = refs[16 + has_s0:]
    c = SCAN_CHUNK

    def load(d, q_ref, k_ref, v_ref, da_ref):
        def fn(r0):
            q = q_ref[pl.ds(r0, c), :] * (DK_D ** -0.5)
            xa = jnp.dot(da_ref[pl.ds(r0, c), :], aw_ref[d], preferred_element_type=F32, precision=HIGHEST) + ab_ref[d]
            la = (jnp.minimum(xa, 0.0) - jnp.log(1.0 + jnp.exp(-jnp.abs(xa)))) * (1.0 / GLA_TAU)
            return q, k_ref[pl.ds(r0, c), :], v_ref[pl.ds(r0, c), :], la
        return fn

    _bidir_scan_body(load(0, *fwd_refs), load(1, *bwd_refs), g_ref,
                     s0_ref, norm_ref, af_ref, abk_ref, mf_ref, mb_ref,
                     o_ref, s_out_ref, of_scr, ob_scr, stf_scr, stb_scr, seq_len)


def _scan_call(kern, proj, streams, extra, s0, seq_len):
    b = proj.shape[0]
    nblk = seq_len // SCAN_ROWS
    tabs = _scan_tables(SCAN_CHUNK)
    full = lambda a: pl.BlockSpec(a.shape, lambda i, j, _n=a.ndim: (0,) * _n)
    in_specs = []
    for cb, w, kind in streams:
        if kind == "f":
            in_specs.append(pl.BlockSpec((None, SCAN_ROWS, w), lambda i, j, _c=cb: (i, j, _c)))
        elif kind == "b":
            in_specs.append(pl.BlockSpec((None, SCAN_ROWS, w), lambda i, j, _c=cb: (i, nblk - 1 - j, _c)))
        else:
            in_specs.append(pl.BlockSpec((None, seq_len, w), lambda i, j, _c=cb: (i, 0, _c)))
    args = [proj] * len(streams)
    for a in tuple(extra) + tabs:
        in_specs.append(full(a))
        args.append(a)
    if s0 is not None:
        in_specs.append(pl.BlockSpec((None, 2, N_HEADS, HEAD_W, HEAD_W), lambda i, j: (i, 0, 0, 0, 0)))
        args.append(s0)
    return pl.pallas_call(
        kern,
        grid=(b, nblk),
        in_specs=in_specs,
        out_specs=[pl.BlockSpec((None, seq_len, MIX_W), lambda i, j: (i, 0, 0)),
                   pl.BlockSpec((None, 2, N_HEADS, HEAD_W, HEAD_W), lambda i, j: (i, 0, 0, 0, 0))],
        out_shape=[jax.ShapeDtypeStruct((b, seq_len, MIX_W), F32),
                   jax.ShapeDtypeStruct((b, 2, N_HEADS, HEAD_W, HEAD_W), F32)],
        scratch_shapes=[pltpu.VMEM((seq_len, MIX_W), F32), pltpu.VMEM((seq_len, MIX_W), F32),
                        pltpu.VMEM((N_HEADS, HEAD_W, HEAD_W), F32), pltpu.VMEM((N_HEADS, HEAD_W, HEAD_W), F32)],
        compiler_params=_params("arbitrary", "arbitrary"),
    )(*args)


def _hgrn_mixer(proj, hgrn_lb, norm, s0, layer):
    seq_len = proj.shape[1]
    kern = functools.partial(_hgrn_kernel, layer=layer, has_s0=s0 is not None, seq_len=seq_len)
    streams = [(0, MIX_W, "f"), (1, MIX_W, "f"), (3, MIX_W, "f"),
               (0, MIX_W, "b"), (2, MIX_W, "b"), (3, MIX_W, "b"), (4, MIX_W, "w")]
    return _scan_call(kern, proj, streams, (hgrn_lb, norm.reshape(1, MIX_W)), s0, seq_len)


def _gla_mixer(proj, aw_pad, ab_pad, norm, s0):
    seq_len = proj.shape[1]
    kern = functools.partial(_gla_kernel, has_s0=s0 is not None, seq_len=seq_len)
    da = 7 * MIX_W // LANES
    streams = [(3, MIX_W, "f"), (4, MIX_W, "f"), (5, MIX_W, "f"), (da, LANES, "f"),
               (3, MIX_W, "b"), (4, MIX_W, "b"), (5, MIX_W, "b"), (da, LANES, "b"), (6, MIX_W, "w")]
    return _scan_call(kern, proj, streams, (aw_pad, ab_pad, norm.reshape(1, MIX_W)), s0, seq_len)


def _dwconv3_rows(x, w_ref, b_ref):
    l = x.shape[0]
    row = lax.broadcasted_iota(jnp.int32, (l, 1), 0)
    xp = jnp.where(row != 0, pltpu.roll(x, 1, 0), 0.0)
    xn = jnp.where(row != l - 1, pltpu.roll(x, l - 1, 0), 0.0)
    return xp * w_ref[0:1, :] + x * w_ref[1:2, :] + xn * w_ref[2:3, :] + b_ref[...]


def _hyena_filter_kernel(z_ref, w1_ref, b1_ref, w2_ref, b2_ref, w3_ref, fr_ref, dl_ref, o_ref):
    z = z_ref[...]
    fr = fr_ref[...]
    h = jnp.sin(fr * (jnp.dot(z, w1_ref[...], preferred_element_type=F32, precision=HIGHEST) + b1_ref[...]))
    h = jnp.sin(fr * (jnp.dot(h, w2_ref[...], preferred_element_type=F32, precision=HIGHEST) + b2_ref[...]))
    h = jnp.dot(h, w3_ref[...], preferred_element_type=F32, precision=HIGHEST)
    win = jnp.exp(-z[:, 0:1] * dl_ref[...])
    o_ref[...] = h * jnp.concatenate([win] * (2 * HY_ORDER), axis=1)


def _hyena_pos_features(l):
    t = jnp.linspace(0.0, 1.0, l, dtype=F32)[:, None]
    w = 2.0 * math.pi * jnp.arange(l, dtype=F32)[:, None] / l
    fb = jnp.linspace(1e-4, HY_BANDS - 1, HY_BANDS, dtype=F32)[None]
    z = jnp.concatenate([t, jnp.cos(fb * w), -jnp.sin(fb * w)], axis=-1)
    return jnp.pad(z, ((0, 0), (0, HY_FF - HY_EMB)))


def _hyena_filters(lens, w1, b1, w2, b2, w3, freq):
    z = jnp.concatenate([_hyena_pos_features(l) for l in lens], axis=0)
    rows = z.shape[0]
    tr = 256
    w1p = jnp.pad(w1, ((0, HY_FF - HY_EMB), (0, 0)))
    max_decay = math.log(HY_TARGET) / HY_FAST
    min_decay = math.log(HY_TARGET) / HY_SLOW
    deltas = jnp.abs(jnp.linspace(min_decay, max_decay, W_B, dtype=F32))[None]
    nout = w3.shape[1]
    full = lambda a: pl.BlockSpec(a.shape, lambda i, _n=a.ndim: (0,) * _n)
    ins = (w1p, b1.reshape(1, -1), w2, b2.reshape(1, -1), w3, freq.reshape(1, -1), deltas)
    return pl.pallas_call(
        _hyena_filter_kernel,
        grid=(rows // tr,),
        in_specs=[pl.BlockSpec((tr, HY_FF), lambda i: (i, 0))] + [full(a) for a in ins],
        out_specs=pl.BlockSpec((tr, nout), lambda i: (i, 0)),
        out_shape=jax.ShapeDtypeStruct((rows, nout), F32),
        compiler_params=_params("arbitrary"),
    )(z, *ins)


def _dft_tables(l):
    n = 2 * l
    k = jnp.arange(l, dtype=jnp.int32)[:, None]
    t1 = jnp.arange(LANES, dtype=jnp.int32)[None, :]
    t2 = (jnp.arange(l // LANES, dtype=jnp.int32) * LANES)[None, :]
    ang = lambda m: (m % n).astype(F32) * (2.0 * math.pi / n)
    ca, sa = jnp.cos(ang(k * t1))[:, None, :], jnp.sin(ang(k * t1))[:, None, :]
    cb, sb = jnp.cos(ang(k * t2))[:, :, None], jnp.sin(ang(k * t2))[:, :, None]
    cos_t = (ca * cb - sa * sb).reshape(l, l)
    sin_t = (sa * cb + ca * sb).reshape(l, l)
    alt = jnp.where(jnp.arange(l) % 2 == 0, 1.0, -1.0).astype(F32)
    sin_f = sin_t.at[0, :].set(alt)
    sin_i = sin_t.at[:, 0].set(alt)
    return cos_t.astype(BF16), sin_f.astype(BF16), sin_i.astype(BF16)


def _hyena_spectrum_kernel(c_ref, s_ref, f0_ref, f1_ref, kr_ref, kia_ref, krb_ref, *, seq_len):
    kt = pl.program_id(1)
    tk = c_ref.shape[0]
    row = lax.broadcasted_iota(jnp.int32, (seq_len, 1), 0)
    f0 = f0_ref[...].astype(BF16)
    f1 = jnp.where(row != 0, f1_ref[...], 0.0).astype(BF16)
    c = c_ref[...]
    s = s_ref[...]
    p0 = jnp.dot(c, f0, preferred_element_type=F32)
    q0 = jnp.dot(s, f0, preferred_element_type=F32)
    p1 = jnp.dot(c, f1, preferred_element_type=F32)
    q1 = jnp.dot(s, f1, preferred_element_type=F32)
    krow = kt * tk + lax.broadcasted_iota(jnp.int32, (tk, 1), 0)
    dc = krow == 0
    wk = jnp.where(dc, 1.0, 2.0) * (1.0 / (2 * seq_len))
    kr = p0 + p1
    kr_ref[...] = kr * wk
    kia_ref[...] = jnp.where(dc, 0.0, q1 - q0) * wk
    krb_ref[...] = jnp.where(dc, q0 + q1, kr) * wk


def _hyena_spectrum(filt, row0, seq_len, tables, tk):
    cos_t, sin_f, _ = tables
    rb = row0 // seq_len
    out = jax.ShapeDtypeStruct((HY_ORDER, seq_len, W_B), F32)
    kern = functools.partial(_hyena_spectrum_kernel, seq_len=seq_len)
    ospec = pl.BlockSpec((None, tk, W_B), lambda o, kt: (o, kt, 0))
    return pl.pallas_call(
        kern,
        grid=(HY_ORDER, seq_len // tk),
        in_specs=[pl.BlockSpec((tk, seq_len), lambda o, kt: (kt, 0)),
                  pl.BlockSpec((tk, seq_len), lambda o, kt: (kt, 0)),
                  pl.BlockSpec((seq_len, W_B), lambda o, kt: (rb, 2 * o)),
                  pl.BlockSpec((seq_len, W_B), lambda o, kt: (rb, 2 * o + 1))],
        out_specs=[ospec, ospec, ospec],
        out_shape=[out, out, out],
        compiler_params=_params("arbitrary", "arbitrary"),
    )(cos_t, sin_f, filt, filt)


def _hyena_order_kernel(zin_ref, gate_ref, cwz_ref, cbz_ref, cwg_ref, cbg_ref, d_ref, kr_ref, kia_ref, krb_ref,
                        cf_ref, sf_ref, ci_ref, si_ref, o_ref, z_scr, zb_scr, acc_scr, *, conv_input):
    kt = pl.program_id(1)

    @pl.when(kt == 0)
    def _():
        z = zin_ref[...]
        if conv_input:
            z = _dwconv3_rows(z, cwz_ref, cbz_ref)
        z_scr[...] = z
        zb_scr[...] = z.astype(BF16)
        acc_scr[...] = jnp.zeros_like(acc_scr)

    zb = zb_scr[...]
    p = jnp.dot(cf_ref[...], zb, preferred_element_type=F32)
    q = jnp.dot(sf_ref[...], zb, preferred_element_type=F32)
    kia = kia_ref[...]
    yr = (p * kr_ref[...] + q * kia).astype(BF16)
    yi = (q * krb_ref[...] - p * kia).astype(BF16)
    acc_scr[...] += (jnp.dot(ci_ref[...], yr, preferred_element_type=F32)
                     + jnp.dot(si_ref[...], yi, preferred_element_type=F32))

    @pl.when(kt == pl.num_programs(1) - 1)
    def _():
        gate = _dwconv3_rows(gate_ref[...], cwg_ref, cbg_ref)
        o_ref[...] = gate * (acc_scr[...] + z_scr[...] * d_ref[...])


def _hyena_order(zin, zin_col, proj, order, conv_w, conv_b, hy_d, spectrum, tables, tk):
    b, seq_len = proj.shape[0], proj.shape[1]
    cos_t, sin_f, sin_i = tables
    kr, kia, krb = spectrum
    hy0 = 5
    conv_input = order == 0
    cw = conv_w.reshape(3, 1 + HY_ORDER, W_B).transpose(1, 0, 2)
    cbias = conv_b.reshape(1 + HY_ORDER, 1, W_B)
    kern = functools.partial(_hyena_order_kernel, conv_input=conv_input)
    kspec = pl.BlockSpec((None, tk, W_B), lambda i, kt: (order, kt, 0))
    return pl.pallas_call(
        kern,
        grid=(b, seq_len // tk),
        in_specs=[pl.BlockSpec((None, seq_len, W_B), lambda i, kt: (i, 0, zin_col)),
                  pl.BlockSpec((None, seq_len, W_B), lambda i, kt: (i, 0, hy0 + 1 + order)),
                  pl.BlockSpec((None, 3, W_B), lambda i, kt: (0, 0, 0)),
                  pl.BlockSpec((None, 1, W_B), lambda i, kt: (0, 0, 0)),
                  pl.BlockSpec((None, 3, W_B), lambda i, kt: (1 + order, 0, 0)),
                  pl.BlockSpec((None, 1, W_B), lambda i, kt: (1 + order, 0, 0)),
                  pl.BlockSpec((None, 1, W_B), lambda i, kt: (order, 0, 0)),
                  kspec, kspec, kspec,
                  pl.BlockSpec((tk, seq_len), lambda i, kt: (kt, 0)),
                  pl.BlockSpec((tk, seq_len), lambda i, kt: (kt, 0)),
                  pl.BlockSpec((seq_len, tk), lambda i, kt: (0, kt)),
                  pl.BlockSpec((seq_len, tk), lambda i, kt: (0, kt))],
        out_specs=pl.BlockSpec((None, seq_len, W_B), lambda i, kt: (i, 0, 0)),
        out_shape=jax.ShapeDtypeStruct((b, seq_len, W_B), F32),
        scratch_shapes=[pltpu.VMEM((seq_len, W_B), F32), pltpu.VMEM((seq_len, W_B), BF16),
                        pltpu.VMEM((seq_len, W_B), F32)],
        compiler_params=_params("arbitrary", "arbitrary"),
    )(zin, proj, cw, cbias, cw, cbias, hy_d.reshape(HY_ORDER, 1, W_B), kr, kia, krb, cos_t, sin_f, cos_t, sin_i)


def _hyena_mixer(proj, conv_w, conv_b, hy_d, spectrum, tables, tk):
    z = _hyena_order(proj, 5, proj, 0, conv_w, conv_b, hy_d, spectrum, tables, tk)
    return _hyena_order(z, 0, proj, 1, conv_w, conv_b, hy_d, spectrum, tables, tk)


def _diff_lambda(lp_ref, lam_init):
    lp = lp_ref[...]
    a = jnp.sum(lp[0:1] * lp[1:2], axis=-1, keepdims=True)
    b = jnp.sum(lp[2:3] * lp[3:4], axis=-1, keepdims=True)
    return jnp.exp(a) - jnp.exp(b) + lam_init


def _diff_attend(q, keys_b, vals_b, lam):
    lane = lax.broadcasted_iota(jnp.int32, (1, 2 * DH_C), 1)
    qs = q * (DH_C ** -0.5)

    def probs(sel):
        s = lax.dot_general(jnp.where(sel, qs, 0.0).astype(BF16), keys_b, NT_DIMS, preferred_element_type=F32)
        e = jnp.exp(s - jnp.max(s, axis=-1, keepdims=True))
        return e / jnp.sum(e, axis=-1, keepdims=True)

    w = probs(lane < DH_C) - lam * probs(lane >= DH_C)
    return jnp.dot(w.astype(BF16), vals_b, preferred_element_type=F32)


def _attn_prompt_kernel(q_ref, k_ref, v_ref, lp_ref, norm_ref, o_ref, kc_ref, vc_ref, *, lam_init):
    k = k_ref[...]
    v = v_ref[...]
    lam = _diff_lambda(lp_ref, lam_init)
    o = _diff_attend(q_ref[...], k.astype(BF16), v.astype(BF16), lam)
    o_ref[...] = _rms(o, norm_ref[...]) * (1.0 - lam_init)
    kc_ref[...] = k
    vc_ref[...] = v


def _attn_prompt(proj, diff_lambda, diff_norm, lam_init):
    b, seq_len = proj.shape[0], proj.shape[1]
    hw = 2 * DH_C
    kern = functools.partial(_attn_prompt_kernel, lam_init=lam_init)
    col = lambda off: pl.BlockSpec((None, seq_len, hw), lambda i, h: (i, 0, off + h))
    cache_spec = pl.BlockSpec((None, None, seq_len, hw), lambda i, h: (i, h, 0, 0))
    cache_shape = jax.ShapeDtypeStruct((b, N_HEADS, seq_len, hw), F32)
    return pl.pallas_call(
        kern,
        grid=(b, N_HEADS),
        in_specs=[col(0), col(N_HEADS), col(2 * N_HEADS),
                  pl.BlockSpec((4, DH_C), lambda i, h: (0, 0)),
                  pl.BlockSpec((1, hw), lambda i, h: (0, h))],
        out_specs=[pl.BlockSpec((None, seq_len, hw), lambda i, h: (i, 0, h)), cache_spec, cache_spec],
        out_shape=[jax.ShapeDtypeStruct((b, seq_len, N_HEADS * hw), F32), cache_shape, cache_shape],
        compiler_params=_params("arbitrary", "arbitrary"),
    )(proj, proj, proj, diff_lambda, diff_norm.reshape(1, -1))


def _rope(x, cos, sin_signed):
    lane = lax.broadcasted_iota(jnp.int32, (1, x.shape[-1]), 1)
    first = (lane % 32) < 16
    partner = jnp.where(first, pltpu.roll(x, x.shape[-1] - 16, 1), pltpu.roll(x, 16, 1))
    return x * cos + partner * sin_signed


def _attn_sample_kernel(q_ref, k_ref, v_ref, ck_ref, cv_ref, cosq_ref, sinq_ref, cosk_ref, sink_ref,
                        lp_ref, norm_ref, o_ref, keys_scr, vals_scr, *, lam_init):
    past = ck_ref.shape[0]

    @pl.when(pl.program_id(2) == 0)
    def _():
        keys_scr[0:past, :] = ck_ref[...].astype(BF16)
        vals_scr[0:past, :] = cv_ref[...].astype(BF16)
        keys_scr[past:, :] = _rope(k_ref[...], cosk_ref[...], sink_ref[...]).astype(BF16)
        vals_scr[past:, :] = v_ref[...].astype(BF16)

    lam = _diff_lambda(lp_ref, lam_init)
    q = _rope(q_ref[...], cosq_ref[...], sinq_ref[...])
    o = _diff_attend(q, keys_scr[...], vals_scr[...], lam)
    o_ref[...] = _rms(o, norm_ref[...]) * (1.0 - lam_init)


def _rope_tables(seq_len):
    pos = jnp.arange(seq_len)
    row = (pos // GRID_W).astype(F32)
    col = (pos % GRID_W).astype(F32)
    half = DH_C // 2
    inv = ROPE_BASE ** (-jnp.arange(0, half, 2, dtype=F32) / half)
    def comp(p):
        ang = p[:, None] * inv[None]
        c, s = jnp.cos(ang), jnp.sin(ang)
        return jnp.concatenate([c, c], axis=-1), jnp.concatenate([-s, s], axis=-1)
    cr, sr = comp(row)
    cc, sc = comp(col)
    cos = jnp.concatenate([cr, cc, cr, cc], axis=-1)
    sin = jnp.concatenate([sr, sc, sr, sc], axis=-1)
    return cos, sin


def _attn_sample(proj, ctx_k, ctx_v, diff_lambda, diff_norm, lam_init, tq):
    b, seq_len = proj.shape[0], proj.shape[1]
    past = ctx_k.shape[2]
    hw = 2 * DH_C
    cos, sin = _rope_tables(seq_len)
    kern = functools.partial(_attn_sample_kernel, lam_init=lam_init)
    ctx_spec = pl.BlockSpec((None, None, past, hw), lambda i, h, j: (i, h, 0, 0))
    return pl.pallas_call(
        kern,
        grid=(b, N_HEADS, seq_len // tq),
        in_specs=[pl.BlockSpec((None, tq, hw), lambda i, h, j: (i, j, h)),
                  pl.BlockSpec((None, seq_len, hw), lambda i, h, j: (i, 0, N_HEADS + h)),
                  pl.BlockSpec((None, seq_len, hw), lambda i, h, j: (i, 0, 2 * N_HEADS + h)),
                  ctx_spec, ctx_spec,
                  pl.BlockSpec((tq, hw), lambda i, h, j: (j, 0)),
                  pl.BlockSpec((tq, hw), lambda i, h, j: (j, 0)),
                  pl.BlockSpec((seq_len, hw), lambda i, h, j: (0, 0)),
                  pl.BlockSpec((seq_len, hw), lambda i, h, j: (0, 0)),
                  pl.BlockSpec((4, DH_C), lambda i, h, j: (0, 0)),
                  pl.BlockSpec((1, hw), lambda i, h, j: (0, h))],
        out_specs=pl.BlockSpec((None, tq, hw), lambda i, h, j: (i, j, h)),
        out_shape=jax.ShapeDtypeStruct((b, seq_len, N_HEADS * hw), F32),
        scratch_shapes=[pltpu.VMEM((past + seq_len, hw), BF16), pltpu.VMEM((past + seq_len, hw), BF16)],
        compiler_params=_params("arbitrary", "arbitrary", "arbitrary"),
    )(proj, proj, proj, ctx_k, ctx_v, cos, sin, cos, sin, diff_lambda, diff_norm.reshape(1, -1))


def _pad_heads(w, axis=-1):
    w = jnp.moveaxis(w, axis, -1)
    lead = w.shape[:-1]
    w = w.reshape(*lead, N_HEADS, DK_D)
    w = jnp.pad(w, [(0, 0)] * len(lead) + [(0, 0), (0, HEAD_W - DK_D)])
    return jnp.moveaxis(w.reshape(*lead, N_HEADS * HEAD_W), -1, axis)


def _odd_w_in(w):
    cq, ck, cv, dq, dk, dv, dg, da = jnp.split(w, [512, 1024, 1536, 1792, 2048, 2560, 3072], axis=1)
    da = jnp.pad(da, ((0, 0), (0, LANES - 2 * GLA_RANK)))
    return jnp.concatenate([cq, ck, cv, _pad_heads(dq), _pad_heads(dk), dv, dg, da], axis=1)


def kernel(x_prompt, x_sample, state_hgrn, cache_diff_k, cache_diff_v, state_gla, c, c_ctx, ada_w, ada_b, norm_g, ffn_up, ffn_conv_w, ffn_conv_b, ffn_down, w_in_even, w_out_even, hgrn_lb, hgrn_norm, hy_conv_w, hy_conv_b, hy_w1, hy_b1, hy_w2, hy_b2, hy_w3, hy_freq, hy_d, w_in_odd, w_out_odd, diff_lambda, diff_norm, gla_aw, gla_ab, gla_norm):
    bp, lp, d = x_prompt.shape
    bs, ls, _ = x_sample.shape

    cvec = jnp.zeros((SUBLANES, d), F32).at[0].set(c_ctx).at[1:1 + bs].set(c)
    mod = _ada_mod(cvec, ada_w, ada_b)

    yp = x_prompt.reshape(1, bp * lp, d)
    ys = x_sample
    tm = 512

    filt = _hyena_filters((ls, lp), hy_w1[0], hy_b1[0], hy_w2[0], hy_b2[0], hy_w3[0], hy_freq[0])
    tab_p, tab_s = _dft_tables(lp), _dft_tables(ls)
    spec_s = _hyena_spectrum(filt, 0, ls, tab_s, 256)
    spec_p = _hyena_spectrum(filt, ls, lp, tab_p, lp)

    outs = {}
    for l in range(DEPTH):
        m = mod[l].reshape(SUBLANES, 6, 1, d)
        mp = [m[0:1, j] for j in range(6)]
        ms = [m[1:1 + bs, j] for j in range(6)]
        g = [norm_g[l, j].reshape(1, d) for j in range(4)]
        if l % 2 == 0:
            e = l // 2
            w_in = w_in_even[e].astype(BF16)
            w_out = w_out_even[e].astype(BF16)
            pp = _normmod_matmul(yp, g[0], mp[0], mp[1], w_in, tm).reshape(bp, lp, -1)
            ps = _normmod_matmul(ys, g[0], ms[0], ms[1], w_in, tm)
            oa_p, st_p = _hgrn_mixer(pp, hgrn_lb, hgrn_norm[e], None, l)
            oa_s, _ = _hgrn_mixer(ps, hgrn_lb, hgrn_norm[e], state_hgrn[:, e], l)
            ob_p = _hyena_mixer(pp, hy_conv_w[e], hy_conv_b[e], hy_d[e], spec_p, tab_p, lp)
            ob_s = _hyena_mixer(ps, hy_conv_w[e], hy_conv_b[e], hy_d[e], spec_s, tab_s, 256)
            outs["hgrn"] = st_p
        else:
            o = l // 2
            lam_init = 0.8 - 0.6 * math.exp(-0.3 * l)
            w_in = _odd_w_in(w_in_odd[o]).astype(BF16)
            w_out = w_out_odd[o].astype(BF16)
            aw = jnp.zeros((2, LANES, MIX_W), F32)
            aw = aw.at[0, 0:GLA_RANK].set(_pad_heads(gla_aw[o, 0])).at[1, GLA_RANK:2 * GLA_RANK].set(_pad_heads(gla_aw[o, 1]))
            ab = _pad_heads(gla_ab[o]).reshape(2, 1, MIX_W)
            s0 = jnp.pad(state_gla[:, o], ((0, 0), (0, 0), (0, 0), (0, HEAD_W - DK_D), (0, 0)))
            pp = _normmod_matmul(yp, g[0], mp[0], mp[1], w_in, tm).reshape(bp, lp, -1)
            ps = _normmod_matmul(ys, g[0], ms[0], ms[1], w_in, tm)
            oa_p, kc, vc = _attn_prompt(pp, diff_lambda[o], diff_norm[o], lam_init)
            oa_s = _attn_sample(ps, cache_diff_k[:, o], cache_diff_v[:, o], diff_lambda[o], diff_norm[o], lam_init, 256)
            ob_p, st_p = _gla_mixer(pp, aw, ab, gla_norm[o], None)
            ob_s, _ = _gla_mixer(ps, aw, ab, gla_norm[o], s0)
            outs["k"], outs["v"], outs["gla"] = kc, vc, st_p[:, :, :, :DK_D, :]
        yp = _out_proj(oa_p.reshape(1, bp * lp, -1), ob_p.reshape(1, bp * lp, -1), w_out, yp, mp[2], g[1], tm)
        ys = _out_proj(oa_s, ob_s, w_out, ys, ms[2], g[1], tm)
        up = ffn_up[l].astype(BF16)
        down = ffn_down[l].astype(BF16)
        cb = ffn_conv_b[l].reshape(1, -1)
        yp = _conv_ffn(yp, g[2], mp[3], mp[4], up, ffn_conv_w[l], cb, down, mp[5], g[3], lp, tm, 1408)
        ys = _conv_ffn(ys, g[2], ms[3], ms[4], up, ffn_conv_w[l], cb, down, ms[5], g[3], ls, tm, 1408)

    return (yp.reshape(bp, lp, d), ys, outs["hgrn"][:, None], outs["k"][:, None], outs["v"][:, None],
            outs["gla"][:, None])
```

```python
import functools
import math

import jax
import jax.numpy as jnp
import numpy as np
from jax import lax
from jax.experimental import pallas as pl
from jax.experimental.pallas import tpu as pltpu

F32 = jnp.float32
BF16 = jnp.bfloat16
HIGHEST = lax.Precision.HIGHEST

D_MODEL = 1024
DEPTH = 2
GRID_W = 64
N_HEADS = 4
HEAD_W = 128
MIX_W = N_HEADS * HEAD_W
W_B = 512
HY_ORDER = 2
HY_EMB = 33
HY_BANDS = (HY_EMB - 1) // 2
HY_FF = 64
HY_TARGET = 1e-2
HY_FAST = 0.3
HY_SLOW = 1.5
DH_C = 64
DK_D = 64
GLA_RANK = 16
GLA_TAU = 16.0
ROPE_BASE = 10000.0
D_FF = 2816
EPS = 1e-6

LANES = 128
SUBLANES = 8
VMEM_LIMIT = 56 * 1024 * 1024
SCAN_CHUNK = 64
SCAN_ROWS = 256
SCAN_GROUP = 2
NT_DIMS = (((1,), (1,)), ((), ()))
LOG2E = 1.4426950408889634


def _params(*sem):
    return pltpu.CompilerParams(dimension_semantics=sem, vmem_limit_bytes=VMEM_LIMIT)


def _silu(x):
    return x * (1.0 / (1.0 + jnp.exp(-x)))


def _rms(x, g):
    return x * lax.rsqrt(jnp.mean(x * x, axis=-1, keepdims=True) + EPS) * g


def _ada_kernel(c_ref, w_ref, b_ref, o_ref):
    s = _silu(c_ref[...])
    o_ref[...] = jnp.dot(s, w_ref[...], preferred_element_type=F32, precision=HIGHEST) + b_ref[...]


def _ada_mod(cvec, ada_w, ada_b):
    n = ada_w.shape[-1]
    tn = 1536
    return pl.pallas_call(
        _ada_kernel,
        grid=(DEPTH, n // tn),
        in_specs=[pl.BlockSpec((SUBLANES, D_MODEL), lambda l, j: (0, 0)),
                  pl.BlockSpec((None, D_MODEL, tn), lambda l, j: (l, 0, j)),
                  pl.BlockSpec((None, 1, tn), lambda l, j: (l, 0, j))],
        out_specs=pl.BlockSpec((None, SUBLANES, tn), lambda l, j: (l, 0, j)),
        out_shape=jax.ShapeDtypeStruct((DEPTH, SUBLANES, n), F32),
        compiler_params=_params("arbitrary", "arbitrary"),
    )(cvec, ada_w, ada_b.reshape(DEPTH, 1, n))


def _normmod_matmul_kernel(x_ref, g_ref, sh_ref, sc_ref, w_ref, o_ref):
    h = _rms(x_ref[...], g_ref[...]) * (1.0 + sc_ref[...]) + sh_ref[...]
    o_ref[...] = jnp.dot(h.astype(BF16), w_ref[...], preferred_element_type=F32)


def _mod_index(n_mod):
    return (lambda b, i: (b, 0, 0)) if n_mod > 1 else (lambda b, i: (0, 0, 0))


def _normmod_matmul(x, g, shift, scale, w, tm):
    b, l, d = x.shape
    n = w.shape[1]
    mod_spec = pl.BlockSpec((None, 1, d), _mod_index(shift.shape[0]))
    return pl.pallas_call(
        _normmod_matmul_kernel,
        grid=(b, l // tm),
        in_specs=[pl.BlockSpec((None, tm, d), lambda b, i: (b, i, 0)),
                  pl.BlockSpec((1, d), lambda b, i: (0, 0)),
                  mod_spec, mod_spec,
                  pl.BlockSpec((d, n), lambda b, i: (0, 0))],
        out_specs=pl.BlockSpec((None, tm, n), lambda b, i: (b, i, 0)),
        out_shape=jax.ShapeDtypeStruct((b, l, n), F32),
        compiler_params=_params("arbitrary", "arbitrary"),
    )(x, g, shift, scale, w)


def _out_proj_kernel(a_ref, b_ref, w_ref, y_ref, gate_ref, g_ref, o_ref):
    half = a_ref.shape[-1]
    m = jnp.dot(a_ref[...].astype(BF16), w_ref[:half, :], preferred_element_type=F32)
    m = m + jnp.dot(b_ref[...].astype(BF16), w_ref[half:, :], preferred_element_type=F32)
    o_ref[...] = y_ref[...] + gate_ref[...] * _rms(m, g_ref[...])


def _out_proj(a, bm, w, y, gate, g, tm):
    b, l, d = y.shape
    wa = a.shape[-1]
    return pl.pallas_call(
        _out_proj_kernel,
        grid=(b, l // tm),
        in_specs=[pl.BlockSpec((None, tm, wa), lambda b, i: (b, i, 0)),
                  pl.BlockSpec((None, tm, wa), lambda b, i: (b, i, 0)),
                  pl.BlockSpec((2 * wa, d), lambda b, i: (0, 0)),
                  pl.BlockSpec((None, tm, d), lambda b, i: (b, i, 0)),
                  pl.BlockSpec((None, 1, d), _mod_index(gate.shape[0])),
                  pl.BlockSpec((1, d), lambda b, i: (0, 0))],
        out_specs=pl.BlockSpec((None, tm, d), lambda b, i: (b, i, 0)),
        out_shape=jax.ShapeDtypeStruct((b, l, d), F32),
        compiler_params=_params("arbitrary", "arbitrary"),
    )(a, bm, w, y, gate, g)


def _ffn_kernel(x_ref, xp_ref, xn_ref, g2_ref, sh_ref, sc_ref, ua_ref, ug_ref, cwa_ref, cwg_ref,
                cba_ref, cbg_ref, dn_ref, gate_ref, g3_ref, o_ref, h_scr, acc_scr, *, seq_len):
    i = pl.program_id(1)
    f = pl.program_id(2)
    tm = x_ref.shape[0]
    halo = xp_ref.shape[0]

    @pl.when(f == 0)
    def _():
        def nm(x):
            return (_rms(x, g2_ref[...]) * (1.0 + sc_ref[...]) + sh_ref[...]).astype(BF16)
        h_scr[0:halo, :] = nm(xp_ref[...])
        h_scr[halo:halo + tm, :] = nm(x_ref[...])
        h_scr[halo + tm:, :] = nm(xn_ref[...])
        acc_scr[...] = jnp.zeros_like(acc_scr)

    rows = tm + 2 * halo
    pos = (i * tm + lax.broadcasted_iota(jnp.int32, (tm, 1), 0)) % seq_len
    has_prev = pos != 0
    has_next = pos != seq_len - 1

    def conv(u_ref, cw_ref, cb_ref):
        u = jnp.dot(h_scr[...], u_ref[...], preferred_element_type=F32)
        up = pltpu.roll(u, 1, 0)[halo:halo + tm]
        un = pltpu.roll(u, rows - 1, 0)[halo:halo + tm]
        uc = u[halo:halo + tm]
        return (jnp.where(has_prev, up, 0.0) * cw_ref[0:1, :] + uc * cw_ref[1:2, :]
                + jnp.where(has_next, un, 0.0) * cw_ref[2:3, :] + cb_ref[...])

    a = conv(ua_ref, cwa_ref, cba_ref)
    gt = conv(ug_ref, cwg_ref, cbg_ref)
    act = (_silu(gt) * a).astype(BF16)
    acc_scr[...] += jnp.dot(act, dn_ref[...], preferred_element_type=F32)

    @pl.when(f == pl.num_programs(2) - 1)
    def _():
        o_ref[...] = x_ref[...] + gate_ref[...] * _rms(acc_scr[...], g3_ref[...])


def _conv_ffn(y, g2, shift, scale, up, cw, cb, down, gate, g3, seq_len, tm, tf):
    b, l, d = y.shape
    nf = D_FF // tf
    halo = SUBLANES
    hb = tm // halo
    last_hb = l // halo - 1
    mod_spec = pl.BlockSpec((None, 1, d), (lambda b, i, f: (b, 0, 0)) if shift.shape[0] > 1 else (lambda b, i, f: (0, 0, 0)))
    vec = lambda off: pl.BlockSpec((1, tf), lambda b, i, f: (0, off + f))
    kern = functools.partial(_ffn_kernel, seq_len=seq_len)
    return pl.pallas_call(
        kern,
        grid=(b, l // tm, nf),
        in_specs=[pl.BlockSpec((None, tm, d), lambda b, i, f: (b, i, 0)),
                  pl.BlockSpec((None, halo, d), lambda b, i, f: (b, jnp.maximum(i * hb - 1, 0), 0)),
                  pl.BlockSpec((None, halo, d), lambda b, i, f: (b, jnp.minimum((i + 1) * hb, last_hb), 0)),
                  pl.BlockSpec((1, d), lambda b, i, f: (0, 0)),
                  mod_spec, mod_spec,
                  pl.BlockSpec((d, tf), lambda b, i, f: (0, f)),
                  pl.BlockSpec((d, tf), lambda b, i, f: (0, nf + f)),
                  pl.BlockSpec((3, tf), lambda b, i, f: (0, f)),
                  pl.BlockSpec((3, tf), lambda b, i, f: (0, nf + f)),
                  vec(0), vec(nf),
                  pl.BlockSpec((tf, d), lambda b, i, f: (f, 0)),
                  mod_spec,
                  pl.BlockSpec((1, d), lambda b, i, f: (0, 0))],
        out_specs=pl.BlockSpec((None, tm, d), lambda b, i, f: (b, i, 0)),
        out_shape=jax.ShapeDtypeStruct((b, l, d), F32),
        scratch_shapes=[pltpu.VMEM((tm + 2 * halo, d), BF16), pltpu.VMEM((tm, d), F32)],
        compiler_params=_params("arbitrary", "arbitrary", "arbitrary"),
    )(y, y, y, g2, shift, scale, up, up, cw, cw, cb, cb, down, gate, g3)


def _scan_tables(c):
    nlev = int(math.log2(c))
    t = np.arange(c)[:, None]
    r = np.arange(c)[None, :]
    masks_f = [np.eye(c, dtype=bool)]
    for lev in range(1, nlev + 1):
        bsz = 2 ** lev
        mid = (t // bsz) * bsz + bsz // 2
        masks_f.append(((t // bsz) == (r // bsz)) & (t >= mid) & (r < mid))
    m_f = np.stack(masks_f).astype(np.float32)
    m_b = np.transpose(m_f, (0, 2, 1))
    tri = lambda a: jnp.asarray(np.concatenate([a, a, a], axis=1).astype(np.float32), dtype=BF16)
    two = lambda m: jnp.asarray(np.concatenate([m, m], axis=2))
    return tri(r <= t), tri(r >= t), two(m_f), two(m_b)


def _level_decay(cum, ncum, lg2, lev, bwd):
    c, w = cum.shape
    b = 1 << lev
    half = b // 2
    off = half - 1 + int(bwd)
    if b == 2:
        odd = lax.broadcasted_iota(jnp.int32, (c, 1), 0) % 2 == 1
        return jnp.where(odd != bwd, lg2, 0.0)
    if b >= 2 * SUBLANES:
        pieces = []
        for b0 in range(0, c, b):
            for rows, is_upper in ((slice(b0, b0 + half), False), (slice(b0 + half, b0 + b), True)):
                src = cum if is_upper != bwd else ncum
                pieces.append(src[rows] - jnp.broadcast_to(src[b0 + off:b0 + off + 1], (half, w)))
        return jnp.concatenate(pieces, axis=0)
    cum3 = cum.reshape(c // SUBLANES, SUBLANES, w)
    sub = lax.broadcasted_iota(jnp.int32, (1, SUBLANES, 1), 1)
    if b == SUBLANES:
        ref3 = jnp.broadcast_to(cum3[:, off:off + 1], cum3.shape)
    else:
        ref3 = jnp.where(sub < b, cum3[:, off:off + 1], cum3[:, b + off:b + off + 1])
    upper = (sub % b) >= half
    sgn = jnp.where(upper != bwd, 1.0, -1.0)
    return ((cum3 - ref3) * sgn).reshape(c, w)


def _pair_stack(xb):
    low = lax.broadcasted_iota(jnp.int32, (1, 2 * HEAD_W), 1) < HEAD_W
    zero = jnp.zeros_like(xb)
    return jnp.concatenate([jnp.where(low, xb, zero), jnp.where(low, zero, xb)], axis=0)


def _scan_group(chunks):
    pw = 2 * HEAD_W
    work = []
    for q, k, v, lg2, tri_ref, m_ref, st_ref, bwd in chunks:
        hi = lg2.astype(BF16)
        r1 = lg2 - hi.astype(F32)
        mid = r1.astype(BF16)
        lo = (r1 - mid.astype(F32)).astype(BF16)
        cum = jnp.dot(tri_ref[...], jnp.concatenate([hi, mid, lo], axis=0), preferred_element_type=F32)
        work.append(dict(q=q, k=k, v=v, lg2=lg2, cum=cum, ncum=-cum, m_ref=m_ref, st_ref=st_ref, bwd=bwd,
                         att=[None, None]))
    nlev = chunks[0][5].shape[0] - 1
    for lev in range(nlev + 1):
        for w in work:
            if lev == 0:
                qb, kb = w["q"].astype(BF16), w["k"].astype(BF16)
            else:
                e = jnp.exp2(_level_decay(w["cum"], w["ncum"], w["lg2"], lev, w["bwd"]))
                qb, kb = (w["q"] * e).astype(BF16), (w["k"] * e).astype(BF16)
            for p in range(N_HEADS // 2):
                sl = slice(p * pw, (p + 1) * pw)
                prod = lax.dot_general(qb[:, sl], _pair_stack(kb[:, sl]), NT_DIMS, preferred_element_type=F32)
                term = w["m_ref"][lev] * prod
                w["att"][p] = term if w["att"][p] is None else w["att"][p] + term
    results = []
    for w in work:
        q, k, v, cum, st_ref = w["q"], w["k"], w["v"], w["cum"], w["st_ref"]
        c = q.shape[0]
        last = 0 if w["bwd"] else c - 1
        e_cum = jnp.exp2(cum)
        d_last = e_cum[last:last + 1, :]
        qe = (q * e_cum).astype(BF16)
        kd = (k * jnp.exp2(cum[last:last + 1, :] - cum)).astype(BF16)
        vb = v.astype(BF16)
        outs = []
        for p in range(N_HEADS // 2):
            sl = slice(p * pw, (p + 1) * pw)
            st = st_ref[p]
            stb = st.astype(BF16)
            o_p = jnp.dot(w["att"][p].astype(BF16), _pair_stack(vb[:, sl]), preferred_element_type=F32)
            inter = [lax.dot_general(qe[:, p * pw + j * HEAD_W:p * pw + (j + 1) * HEAD_W],
                                     stb[:, j * HEAD_W:(j + 1) * HEAD_W], NT_DIMS, preferred_element_type=F32)
                     for j in range(2)]
            outs.append(o_p + jnp.concatenate(inter, axis=1))
            vstack = jnp.concatenate([v[:, p * pw:p * pw + HEAD_W], v[:, p * pw + HEAD_W:(p + 1) * pw]], axis=0)
            st_ref[p] = st * d_last[:, sl] + jnp.dot(vstack.T.astype(BF16), _pair_stack(kd[:, sl]),
                                                      preferred_element_type=F32)
        results.append(jnp.concatenate(outs, axis=1))
    return results


def _bidir_scan_body(load_fwd, load_bwd, g_ref, s0_ref, norm_ref, af_ref, ab_ref, mf_ref, mb_ref,
                     o_ref, s_out_ref, of_scr, ob_scr, stf_scr, stb_scr, seq_len):
    c = SCAN_CHUNK
    j = pl.program_id(1)
    nblk = pl.num_programs(1)
    n = SCAN_ROWS // c

    @pl.when(j == 0)
    def _():
        for p in range(N_HEADS // 2):
            if s0_ref is None:
                stf_scr[p] = jnp.zeros((HEAD_W, 2 * HEAD_W), F32)
                stb_scr[p] = jnp.zeros((HEAD_W, 2 * HEAD_W), F32)
            else:
                stf_scr[p] = jnp.concatenate([s0_ref[0, 2 * p].T, s0_ref[0, 2 * p + 1].T], axis=1)
                stb_scr[p] = jnp.concatenate([s0_ref[1, 2 * p].T, s0_ref[1, 2 * p + 1].T], axis=1)

    base_f = j * SCAN_ROWS
    base_b = (nblk - 1 - j) * SCAN_ROWS

    def step(i, carry):
        chunks, stores = [], []
        for u in range(SCAN_GROUP):
            rf = pl.multiple_of((i * SCAN_GROUP + u) * c, c)
            chunks.append(load_fwd(rf) + (af_ref, mf_ref, stf_scr, False))
            stores.append((of_scr, pl.multiple_of(base_f + rf, c)))
        for u in range(SCAN_GROUP):
            rb = pl.multiple_of((n - 1 - i * SCAN_GROUP - u) * c, c)
            chunks.append(load_bwd(rb) + (ab_ref, mb_ref, stb_scr, True))
            stores.append((ob_scr, pl.multiple_of(base_b + rb, c)))
        for (scr, r0), o in zip(stores, _scan_group(chunks)):
            scr[pl.ds(r0, c), :] = o
        return carry

    lax.fori_loop(0, n // SCAN_GROUP, step, 0)

    @pl.when(j == nblk - 1)
    def _():
        def fin(jj, carry):
            r0 = pl.multiple_of(jj * SCAN_ROWS, SCAN_ROWS)
            o = of_scr[pl.ds(r0, SCAN_ROWS), :] + ob_scr[pl.ds(r0, SCAN_ROWS), :]
            parts = []
            for h in range(N_HEADS):
                sl = slice(h * HEAD_W, (h + 1) * HEAD_W)
                parts.append(_rms(o[:, sl], norm_ref[:, sl]))
            o_ref[pl.ds(r0, SCAN_ROWS), :] = jnp.concatenate(parts, axis=1) * _silu(g_ref[pl.ds(r0, SCAN_ROWS), :])
            return carry

        lax.fori_loop(0, seq_len // SCAN_ROWS, fin, 0)
        for h in range(N_HEADS):
            p, hs = h // 2, slice((h % 2) * HEAD_W, (h % 2 + 1) * HEAD_W)
            s_out_ref[0, h] = stf_scr[p][:, hs].T
            s_out_ref[1, h] = stb_scr[p][:, hs].T


def _hgrn_kernel(*refs, layer, has_s0, seq_len):
    qf_ref, ff_ref, if_ref, qb_ref, fb_ref, ib_ref, g_ref, lbraw_ref, norm_ref, af_ref, ab_ref, mf_ref, mb_ref = refs[:13]
    s0_ref = refs[13] if has_s0 else None
    o_ref, s_out_ref, of_scr, ob_scr, stf_scr, stb_scr = refs[13 + has_s0:]
    c = SCAN_CHUNK
    raw = lbraw_ref[...]
    ex = jnp.exp(raw - jnp.max(raw, axis=0, keepdims=True))
    sm = ex / jnp.sum(ex, axis=0, keepdims=True)
    lb = sm[0]
    for j in range(1, layer + 1):
        lb = lb + sm[j]

    def load(q_ref, f_ref, i_ref, lb_row):
        def fn(r0):
            q = _silu(q_ref[pl.ds(r0, c), :]) * (HEAD_W ** -0.5)
            sig = 1.0 / (1.0 + jnp.exp(-f_ref[pl.ds(r0, c), :]))
            f = lb_row + (1.0 - lb_row) * sig
            return q, 1.0 - f, i_ref[pl.ds(r0, c), :], jnp.log2(f)
        return fn

    _bidir_scan_body(load(qf_ref, ff_ref, if_ref, lb[0:1]), load(qb_ref, fb_ref, ib_ref, lb[1:2]), g_ref,
                     s0_ref, norm_ref, af_ref, ab_ref, mf_ref, mb_ref,
                     o_ref, s_out_ref, of_scr, ob_scr, stf_scr, stb_scr, seq_len)


def _gla_kernel(*refs, has_s0, seq_len):
    fwd_refs, bwd_refs = refs[0:4], refs[4:8]
    g_ref, aw_ref, ab_ref, norm_ref, af_ref, abk_ref, mf_ref, mb_ref = refs[8:16]
    s0_ref = refs[16] if has_s0 else None
    o_ref, s_out_ref, of_scr, ob_scr, stf_scr, stb_scr = refs[16 + has_s0:]
    c = SCAN_CHUNK

    def load(d, q_ref, k_ref, v_ref, da_ref):
        def fn(r0):
            q = q_ref[pl.ds(r0, c), :] * (DK_D ** -0.5)
            da = da_ref[pl.ds(r0, c), :]
            da_hi = da.astype(BF16)
            da_mid = (da - da_hi.astype(F32)).astype(BF16)
            xa = jnp.dot(jnp.concatenate([da_hi, da_hi, da_mid], axis=1), aw_ref[d],
                         preferred_element_type=F32) + ab_ref[d]
            la = (jnp.minimum(xa, 0.0) - jnp.log(1.0 + jnp.exp(-jnp.abs(xa)))) * (LOG2E / GLA_TAU)
            return q, k_ref[pl.ds(r0, c), :], v_ref[pl.ds(r0, c), :], la
        return fn

    _bidir_scan_body(load(0, *fwd_refs), load(1, *bwd_refs), g_ref,
                     s0_ref, norm_ref, af_ref, abk_ref, mf_ref, mb_ref,
                     o_ref, s_out_ref, of_scr, ob_scr, stf_scr, stb_scr, seq_len)


def _scan_call(kern, proj, streams, extra, s0, seq_len):
    b = proj.shape[0]
    nblk = seq_len // SCAN_ROWS
    tabs = _scan_tables(SCAN_CHUNK)
    full = lambda a: pl.BlockSpec(a.shape, lambda i, j, _n=a.ndim: (0,) * _n)
    in_specs = []
    for cb, w, kind in streams:
        if kind == "f":
            in_specs.append(pl.BlockSpec((None, SCAN_ROWS, w), lambda i, j, _c=cb: (i, j, _c)))
        elif kind == "b":
            in_specs.append(pl.BlockSpec((None, SCAN_ROWS, w), lambda i, j, _c=cb: (i, nblk - 1 - j, _c)))
        else:
            in_specs.append(pl.BlockSpec((None, seq_len, w), lambda i, j, _c=cb: (i, 0, _c)))
    args = [proj] * len(streams)
    for a in tuple(extra) + tabs:
        in_specs.append(full(a))
        args.append(a)
    if s0 is not None:
        in_specs.append(pl.BlockSpec((None, 2, N_HEADS, HEAD_W, HEAD_W), lambda i, j: (i, 0, 0, 0, 0)))
        args.append(s0)
    return pl.pallas_call(
        kern,
        grid=(b, nblk),
        in_specs=in_specs,
        out_specs=[pl.BlockSpec((None, seq_len, MIX_W), lambda i, j: (i, 0, 0)),
                   pl.BlockSpec((None, 2, N_HEADS, HEAD_W, HEAD_W), lambda i, j: (i, 0, 0, 0, 0))],
        out_shape=[jax.ShapeDtypeStruct((b, seq_len, MIX_W), F32),
                   jax.ShapeDtypeStruct((b, 2, N_HEADS, HEAD_W, HEAD_W), F32)],
        scratch_shapes=[pltpu.VMEM((seq_len, MIX_W), F32), pltpu.VMEM((seq_len, MIX_W), F32),
                        pltpu.VMEM((N_HEADS // 2, HEAD_W, 2 * HEAD_W), F32),
                        pltpu.VMEM((N_HEADS // 2, HEAD_W, 2 * HEAD_W), F32)],
        compiler_params=_params("arbitrary", "arbitrary"),
    )(*args)


def _hgrn_mixer(proj, hgrn_lb, norm, s0, layer):
    seq_len = proj.shape[1]
    kern = functools.partial(_hgrn_kernel, layer=layer, has_s0=s0 is not None, seq_len=seq_len)
    streams = [(0, MIX_W, "f"), (1, MIX_W, "f"), (3, MIX_W, "f"),
               (0, MIX_W, "b"), (2, MIX_W, "b"), (3, MIX_W, "b"), (4, MIX_W, "w")]
    return _scan_call(kern, proj, streams, (hgrn_lb, norm.reshape(1, MIX_W)), s0, seq_len)


def _gla_mixer(proj, aw_pad, ab_pad, norm, s0):
    seq_len = proj.shape[1]
    kern = functools.partial(_gla_kernel, has_s0=s0 is not None, seq_len=seq_len)
    da = 7 * MIX_W // LANES
    streams = [(3, MIX_W, "f"), (4, MIX_W, "f"), (5, MIX_W, "f"), (da, LANES, "f"),
               (3, MIX_W, "b"), (4, MIX_W, "b"), (5, MIX_W, "b"), (da, LANES, "b"), (6, MIX_W, "w")]
    return _scan_call(kern, proj, streams, (aw_pad, ab_pad, norm.reshape(1, MIX_W)), s0, seq_len)


def _dwconv3_rows(x, w_ref, b_ref):
    l = x.shape[0]
    row = lax.broadcasted_iota(jnp.int32, (l, 1), 0)
    xp = jnp.where(row != 0, pltpu.roll(x, 1, 0), 0.0)
    xn = jnp.where(row != l - 1, pltpu.roll(x, l - 1, 0), 0.0)
    return xp * w_ref[0:1, :] + x * w_ref[1:2, :] + xn * w_ref[2:3, :] + b_ref[...]


def _hyena_filter_kernel(z_ref, w1_ref, b1_ref, w2_ref, b2_ref, w3_ref, fr_ref, dl_ref, o_ref):
    z = z_ref[...]
    fr = fr_ref[...]
    h = jnp.sin(fr * (jnp.dot(z, w1_ref[...], preferred_element_type=F32, precision=HIGHEST) + b1_ref[...]))
    h = jnp.sin(fr * (jnp.dot(h, w2_ref[...], preferred_element_type=F32, precision=HIGHEST) + b2_ref[...]))
    h = jnp.dot(h, w3_ref[...], preferred_element_type=F32, precision=HIGHEST)
    win = jnp.exp(-z[:, 0:1] * dl_ref[...])
    o_ref[...] = h * jnp.concatenate([win] * (2 * HY_ORDER), axis=1)


def _hyena_pos_features(l):
    t = jnp.linspace(0.0, 1.0, l, dtype=F32)[:, None]
    w = 2.0 * math.pi * jnp.arange(l, dtype=F32)[:, None] / l
    fb = jnp.linspace(1e-4, HY_BANDS - 1, HY_BANDS, dtype=F32)[None]
    z = jnp.concatenate([t, jnp.cos(fb * w), -jnp.sin(fb * w)], axis=-1)
    return jnp.pad(z, ((0, 0), (0, HY_FF - HY_EMB)))


def _hyena_filters(lens, w1, b1, w2, b2, w3, freq):
    z = jnp.concatenate([_hyena_pos_features(l) for l in lens], axis=0)
    rows = z.shape[0]
    tr = 256
    w1p = jnp.pad(w1, ((0, HY_FF - HY_EMB), (0, 0)))
    max_decay = math.log(HY_TARGET) / HY_FAST
    min_decay = math.log(HY_TARGET) / HY_SLOW
    deltas = jnp.abs(jnp.linspace(min_decay, max_decay, W_B, dtype=F32))[None]
    nout = w3.shape[1]
    full = lambda a: pl.BlockSpec(a.shape, lambda i, _n=a.ndim: (0,) * _n)
    ins = (w1p, b1.reshape(1, -1), w2, b2.reshape(1, -1), w3, freq.reshape(1, -1), deltas)
    return pl.pallas_call(
        _hyena_filter_kernel,
        grid=(rows // tr,),
        in_specs=[pl.BlockSpec((tr, HY_FF), lambda i: (i, 0))] + [full(a) for a in ins],
        out_specs=pl.BlockSpec((tr, nout), lambda i: (i, 0)),
        out_shape=jax.ShapeDtypeStruct((rows, nout), F32),
        compiler_params=_params("arbitrary"),
    )(z, *ins)


def _dft_tables(l):
    n = 2 * l
    k = jnp.arange(l, dtype=jnp.int32)[:, None]
    t1 = jnp.arange(LANES, dtype=jnp.int32)[None, :]
    t2 = (jnp.arange(l // LANES, dtype=jnp.int32) * LANES)[None, :]
    ang = lambda m: (m % n).astype(F32) * (2.0 * math.pi / n)
    ca, sa = jnp.cos(ang(k * t1))[:, None, :], jnp.sin(ang(k * t1))[:, None, :]
    cb, sb = jnp.cos(ang(k * t2))[:, :, None], jnp.sin(ang(k * t2))[:, :, None]
    cos_t = (ca * cb - sa * sb).reshape(l, l)
    sin_t = (sa * cb + ca * sb).reshape(l, l)
    alt = jnp.where(jnp.arange(l) % 2 == 0, 1.0, -1.0).astype(F32)
    sin_f = sin_t.at[0, :].set(alt)
    sin_i = sin_t.at[:, 0].set(alt)
    return cos_t.astype(BF16), sin_f.astype(BF16), sin_i.astype(BF16)


def _hyena_spectrum_kernel(c_ref, s_ref, f0_ref, f1_ref, kr_ref, kia_ref, krb_ref, *, seq_len):
    kt = pl.program_id(1)
    tk = c_ref.shape[0]
    row = lax.broadcasted_iota(jnp.int32, (seq_len, 1), 0)
    f0 = f0_ref[...].astype(BF16)
    f1 = jnp.where(row != 0, f1_ref[...], 0.0).astype(BF16)
    c = c_ref[...]
    s = s_ref[...]
    p0 = jnp.dot(c, f0, preferred_element_type=F32)
    q0 = jnp.dot(s, f0, preferred_element_type=F32)
    p1 = jnp.dot(c, f1, preferred_element_type=F32)
    q1 = jnp.dot(s, f1, preferred_element_type=F32)
    krow = kt * tk + lax.broadcasted_iota(jnp.int32, (tk, 1), 0)
    dc = krow == 0
    wk = jnp.where(dc, 1.0, 2.0) * (1.0 / (2 * seq_len))
    kr = p0 + p1
    kr_ref[...] = kr * wk
    kia_ref[...] = jnp.where(dc, 0.0, q1 - q0) * wk
    krb_ref[...] = jnp.where(dc, q0 + q1, kr) * wk


def _hyena_spectrum(filt, row0, seq_len, tables, tk):
    cos_t, sin_f, _ = tables
    rb = row0 // seq_len
    out = jax.ShapeDtypeStruct((HY_ORDER, seq_len, W_B), F32)
    kern = functools.partial(_hyena_spectrum_kernel, seq_len=seq_len)
    ospec = pl.BlockSpec((None, tk, W_B), lambda o, kt: (o, kt, 0))
    return pl.pallas_call(
        kern,
        grid=(HY_ORDER, seq_len // tk),
        in_specs=[pl.BlockSpec((tk, seq_len), lambda o, kt: (kt, 0)),
                  pl.BlockSpec((tk, seq_len), lambda o, kt: (kt, 0)),
                  pl.BlockSpec((seq_len, W_B), lambda o, kt: (rb, 2 * o)),
                  pl.BlockSpec((seq_len, W_B), lambda o, kt: (rb, 2 * o + 1))],
        out_specs=[ospec, ospec, ospec],
        out_shape=[out, out, out],
        compiler_params=_params("arbitrary", "arbitrary"),
    )(cos_t, sin_f, filt, filt)


def _hyena_order_kernel(zin_ref, gate_ref, cwz_ref, cbz_ref, cwg_ref, cbg_ref, d_ref, kr_ref, kia_ref, krb_ref,
                        cf_ref, sf_ref, ci_ref, si_ref, o_ref, z_scr, zb_scr, acc_scr, *, conv_input):
    kt = pl.program_id(1)

    @pl.when(kt == 0)
    def _():
        z = zin_ref[...]
        if conv_input:
            z = _dwconv3_rows(z, cwz_ref, cbz_ref)
        z_scr[...] = z
        zb_scr[...] = z.astype(BF16)
        acc_scr[...] = jnp.zeros_like(acc_scr)

    zb = zb_scr[...]
    p = jnp.dot(cf_ref[...], zb, preferred_element_type=F32)
    q = jnp.dot(sf_ref[...], zb, preferred_element_type=F32)
    kia = kia_ref[...]
    yr = (p * kr_ref[...] + q * kia).astype(BF16)
    yi = (q * krb_ref[...] - p * kia).astype(BF16)
    acc_scr[...] += (jnp.dot(ci_ref[...], yr, preferred_element_type=F32)
                     + jnp.dot(si_ref[...], yi, preferred_element_type=F32))

    @pl.when(kt == pl.num_programs(1) - 1)
    def _():
        gate = _dwconv3_rows(gate_ref[...], cwg_ref, cbg_ref)
        o_ref[...] = gate * (acc_scr[...] + z_scr[...] * d_ref[...])


def _hyena_order(zin, zin_col, proj, order, conv_w, conv_b, hy_d, spectrum, tables, tk):
    b, seq_len = proj.shape[0], proj.shape[1]
    cos_t, sin_f, sin_i = tables
    kr, kia, krb = spectrum
    hy0 = 5
    conv_input = order == 0
    cw = conv_w.reshape(3, 1 + HY_ORDER, W_B).transpose(1, 0, 2)
    cbias = conv_b.reshape(1 + HY_ORDER, 1, W_B)
    kern = functools.partial(_hyena_order_kernel, conv_input=conv_input)
    kspec = pl.BlockSpec((None, tk, W_B), lambda i, kt: (order, kt, 0))
    return pl.pallas_call(
        kern,
        grid=(b, seq_len // tk),
        in_specs=[pl.BlockSpec((None, seq_len, W_B), lambda i, kt: (i, 0, zin_col)),
                  pl.BlockSpec((None, seq_len, W_B), lambda i, kt: (i, 0, hy0 + 1 + order)),
                  pl.BlockSpec((None, 3, W_B), lambda i, kt: (0, 0, 0)),
                  pl.BlockSpec((None, 1, W_B), lambda i, kt: (0, 0, 0)),
                  pl.BlockSpec((None, 3, W_B), lambda i, kt: (1 + order, 0, 0)),
                  pl.BlockSpec((None, 1, W_B), lambda i, kt: (1 + order, 0, 0)),
                  pl.BlockSpec((None, 1, W_B), lambda i, kt: (order, 0, 0)),
                  kspec, kspec, kspec,
                  pl.BlockSpec((tk, seq_len), lambda i, kt: (kt, 0)),
                  pl.BlockSpec((tk, seq_len), lambda i, kt: (kt, 0)),
                  pl.BlockSpec((seq_len, tk), lambda i, kt: (0, kt)),
                  pl.BlockSpec((seq_len, tk), lambda i, kt: (0, kt))],
        out_specs=pl.BlockSpec((None, seq_len, W_B), lambda i, kt: (i, 0, 0)),
        out_shape=jax.ShapeDtypeStruct((b, seq_len, W_B), F32),
        scratch_shapes=[pltpu.VMEM((seq_len, W_B), F32), pltpu.VMEM((seq_len, W_B), BF16),
                        pltpu.VMEM((seq_len, W_B), F32)],
        compiler_params=_params("arbitrary", "arbitrary"),
    )(zin, proj, cw, cbias, cw, cbias, hy_d.reshape(HY_ORDER, 1, W_B), kr, kia, krb, cos_t, sin_f, cos_t, sin_i)


def _hyena_mixer(proj, conv_w, conv_b, hy_d, spectrum, tables, tk):
    z = _hyena_order(proj, 5, proj, 0, conv_w, conv_b, hy_d, spectrum, tables, tk)
    return _hyena_order(z, 0, proj, 1, conv_w, conv_b, hy_d, spectrum, tables, tk)


def _diff_lambda(lp_ref, lam_init):
    lp = lp_ref[...]
    a = jnp.sum(lp[0:1] * lp[1:2], axis=-1, keepdims=True)
    b = jnp.sum(lp[2:3] * lp[3:4], axis=-1, keepdims=True)
    return jnp.exp(a) - jnp.exp(b) + lam_init


def _diff_attend(q, keys_b, vals_b, lam):
    lane = lax.broadcasted_iota(jnp.int32, (1, 2 * DH_C), 1)
    qs = q * (DH_C ** -0.5 * LOG2E)

    def attend(sel):
        s = lax.dot_general(jnp.where(sel, qs, 0.0).astype(BF16), keys_b, NT_DIMS, preferred_element_type=F32)
        e = jnp.exp2(s - jnp.max(s, axis=-1, keepdims=True))
        den = jnp.sum(e, axis=-1, keepdims=True)
        return jnp.dot(e.astype(BF16), vals_b, preferred_element_type=F32) / den

    return attend(lane < DH_C) - lam * attend(lane >= DH_C)


def _attn_prompt_kernel(q_ref, k_ref, v_ref, lp_ref, norm_ref, o_ref, kc_ref, vc_ref, *, lam_init):
    hw = 2 * DH_C
    lam = _diff_lambda(lp_ref, lam_init)
    for h in range(N_HEADS):
        sl = slice(h * hw, (h + 1) * hw)
        k = k_ref[:, sl]
        v = v_ref[:, sl]
        o = _diff_attend(q_ref[:, sl], k.astype(BF16), v.astype(BF16), lam)
        o_ref[:, sl] = _rms(o, norm_ref[:, sl]) * (1.0 - lam_init)
        kc_ref[h] = k
        vc_ref[h] = v


def _attn_prompt(proj, diff_lambda, diff_norm, lam_init):
    b, seq_len = proj.shape[0], proj.shape[1]
    hw = 2 * DH_C
    w = N_HEADS * hw
    kern = functools.partial(_attn_prompt_kernel, lam_init=lam_init)
    col = lambda j: pl.BlockSpec((None, seq_len, w), lambda i: (i, 0, j))
    cache_spec = pl.BlockSpec((None, N_HEADS, seq_len, hw), lambda i: (i, 0, 0, 0))
    cache_shape = jax.ShapeDtypeStruct((b, N_HEADS, seq_len, hw), F32)
    return pl.pallas_call(
        kern,
        grid=(b,),
        in_specs=[col(0), col(1), col(2),
                  pl.BlockSpec((4, DH_C), lambda i: (0, 0)),
                  pl.BlockSpec((1, w), lambda i: (0, 0))],
        out_specs=[pl.BlockSpec((None, seq_len, w), lambda i: (i, 0, 0)), cache_spec, cache_spec],
        out_shape=[jax.ShapeDtypeStruct((b, seq_len, w), F32), cache_shape, cache_shape],
        compiler_params=_params("arbitrary"),
    )(proj, proj, proj, diff_lambda, diff_norm.reshape(1, -1))


def _rope(x, cos, sin_signed):
    lane = lax.broadcasted_iota(jnp.int32, (1, x.shape[-1]), 1)
    first = (lane % 32) < 16
    partner = jnp.where(first, pltpu.roll(x, x.shape[-1] - 16, 1), pltpu.roll(x, 16, 1))
    return x * cos + partner * sin_signed


def _attn_sample_kernel(q_ref, k_ref, v_ref, ck_ref, cv_ref, cosq_ref, sinq_ref, cosk_ref, sink_ref,
                        lp_ref, norm_ref, o_ref, keys_scr, vals_scr, *, lam_init):
    past = ck_ref.shape[1]
    hw = 2 * DH_C

    @pl.when(pl.program_id(1) == 0)
    def _():
        for h in range(N_HEADS):
            sl = slice(h * hw, (h + 1) * hw)
            keys_scr[h, 0:past, :] = ck_ref[h].astype(BF16)
            vals_scr[h, 0:past, :] = cv_ref[h].astype(BF16)
            keys_scr[h, past:, :] = _rope(k_ref[:, sl], cosk_ref[...], sink_ref[...]).astype(BF16)
            vals_scr[h, past:, :] = v_ref[:, sl].astype(BF16)

    lam = _diff_lambda(lp_ref, lam_init)
    for h in range(N_HEADS):
        sl = slice(h * hw, (h + 1) * hw)
        q = _rope(q_ref[:, sl], cosq_ref[...], sinq_ref[...])
        o = _diff_attend(q, keys_scr[h], vals_scr[h], lam)
        o_ref[:, sl] = _rms(o, norm_ref[:, sl]) * (1.0 - lam_init)


def _rope_tables(seq_len):
    pos = jnp.arange(seq_len)
    row = (pos // GRID_W).astype(F32)
    col = (pos % GRID_W).astype(F32)
    half = DH_C // 2
    inv = ROPE_BASE ** (-jnp.arange(0, half, 2, dtype=F32) / half)
    def comp(p):
        ang = p[:, None] * inv[None]
        c, s = jnp.cos(ang), jnp.sin(ang)
        return jnp.concatenate([c, c], axis=-1), jnp.concatenate([-s, s], axis=-1)
    cr, sr = comp(row)
    cc, sc = comp(col)
    cos = jnp.concatenate([cr, cc, cr, cc], axis=-1)
    sin = jnp.concatenate([sr, sc, sr, sc], axis=-1)
    return cos, sin


def _attn_sample(proj, ctx_k, ctx_v, diff_lambda, diff_norm, lam_init, tq):
    b, seq_len = proj.shape[0], proj.shape[1]
    past = ctx_k.shape[2]
    hw = 2 * DH_C
    w = N_HEADS * hw
    cos, sin = _rope_tables(seq_len)
    kern = functools.partial(_attn_sample_kernel, lam_init=lam_init)
    ctx_spec = pl.BlockSpec((None, N_HEADS, past, hw), lambda i, j: (i, 0, 0, 0))
    return pl.pallas_call(
        kern,
        grid=(b, seq_len // tq),
        in_specs=[pl.BlockSpec((None, tq, w), lambda i, j: (i, j, 0)),
                  pl.BlockSpec((None, seq_len, w), lambda i, j: (i, 0, 1)),
                  pl.BlockSpec((None, seq_len, w), lambda i, j: (i, 0, 2)),
                  ctx_spec, ctx_spec,
                  pl.BlockSpec((tq, hw), lambda i, j: (j, 0)),
                  pl.BlockSpec((tq, hw), lambda i, j: (j, 0)),
                  pl.BlockSpec((seq_len, hw), lambda i, j: (0, 0)),
                  pl.BlockSpec((seq_len, hw), lambda i, j: (0, 0)),
                  pl.BlockSpec((4, DH_C), lambda i, j: (0, 0)),
                  pl.BlockSpec((1, w), lambda i, j: (0, 0))],
        out_specs=pl.BlockSpec((None, tq, w), lambda i, j: (i, j, 0)),
        out_shape=jax.ShapeDtypeStruct((b, seq_len, w), F32),
        scratch_shapes=[pltpu.VMEM((N_HEADS, past + seq_len, hw), BF16),
                        pltpu.VMEM((N_HEADS, past + seq_len, hw), BF16)],
        compiler_params=_params("arbitrary", "arbitrary"),
    )(proj, proj, proj, ctx_k, ctx_v, cos, sin, cos, sin, diff_lambda, diff_norm.reshape(1, -1))


def _pad_heads(w, axis=-1):
    w = jnp.moveaxis(w, axis, -1)
    lead = w.shape[:-1]
    w = w.reshape(*lead, N_HEADS, DK_D)
    w = jnp.pad(w, [(0, 0)] * len(lead) + [(0, 0), (0, HEAD_W - DK_D)])
    return jnp.moveaxis(w.reshape(*lead, N_HEADS * HEAD_W), -1, axis)


def _odd_w_in(w):
    cq, ck, cv, dq, dk, dv, dg, da = jnp.split(w, [512, 1024, 1536, 1792, 2048, 2560, 3072], axis=1)
    da = jnp.pad(da, ((0, 0), (0, LANES - 2 * GLA_RANK)))
    return jnp.concatenate([cq, ck, cv, _pad_heads(dq), _pad_heads(dk), dv, dg, da], axis=1)


def kernel(x_prompt, x_sample, state_hgrn, cache_diff_k, cache_diff_v, state_gla, c, c_ctx, ada_w, ada_b, norm_g, ffn_up, ffn_conv_w, ffn_conv_b, ffn_down, w_in_even, w_out_even, hgrn_lb, hgrn_norm, hy_conv_w, hy_conv_b, hy_w1, hy_b1, hy_w2, hy_b2, hy_w3, hy_freq, hy_d, w_in_odd, w_out_odd, diff_lambda, diff_norm, gla_aw, gla_ab, gla_norm):
    bp, lp, d = x_prompt.shape
    bs, ls, _ = x_sample.shape

    cvec = jnp.zeros((SUBLANES, d), F32).at[0].set(c_ctx).at[1:1 + bs].set(c)
    mod = _ada_mod(cvec, ada_w, ada_b)

    yp = x_prompt.reshape(1, bp * lp, d)
    ys = x_sample
    tm = 512

    filt = _hyena_filters((ls, lp), hy_w1[0], hy_b1[0], hy_w2[0], hy_b2[0], hy_w3[0], hy_freq[0])
    tab_p, tab_s = _dft_tables(lp), _dft_tables(ls)
    spec_s = _hyena_spectrum(filt, 0, ls, tab_s, 256)
    spec_p = _hyena_spectrum(filt, ls, lp, tab_p, lp)

    outs = {}
    for l in range(DEPTH):
        m = mod[l].reshape(SUBLANES, 6, 1, d)
        mp = [m[0:1, j] for j in range(6)]
        ms = [m[1:1 + bs, j] for j in range(6)]
        g = [norm_g[l, j].reshape(1, d) for j in range(4)]
        if l % 2 == 0:
            e = l // 2
            w_in = w_in_even[e].astype(BF16)
            w_out = w_out_even[e].astype(BF16)
            pp = _normmod_matmul(yp, g[0], mp[0], mp[1], w_in, tm).reshape(bp, lp, -1)
            ps = _normmod_matmul(ys, g[0], ms[0], ms[1], w_in, tm)
            oa_p, st_p = _hgrn_mixer(pp, hgrn_lb, hgrn_norm[e], None, l)
            oa_s, _ = _hgrn_mixer(ps, hgrn_lb, hgrn_norm[e], state_hgrn[:, e], l)
            ob_p = _hyena_mixer(pp, hy_conv_w[e], hy_conv_b[e], hy_d[e], spec_p, tab_p, lp)
            ob_s = _hyena_mixer(ps, hy_conv_w[e], hy_conv_b[e], hy_d[e], spec_s, tab_s, 256)
            outs["hgrn"] = st_p
        else:
            o = l // 2
            lam_init = 0.8 - 0.6 * math.exp(-0.3 * l)
            w_in = _odd_w_in(w_in_odd[o]).astype(BF16)
            w_out = w_out_odd[o].astype(BF16)
            aw = jnp.zeros((2, LANES, MIX_W), F32)
            aw = aw.at[0, 0:GLA_RANK].set(_pad_heads(gla_aw[o, 0])).at[1, GLA_RANK:2 * GLA_RANK].set(_pad_heads(gla_aw[o, 1]))
            aw_hi = aw.astype(BF16)
            aw_mid = (aw - aw_hi.astype(F32)).astype(BF16)
            aw = jnp.concatenate([aw_hi, aw_mid, aw_hi], axis=1)
            ab = _pad_heads(gla_ab[o]).reshape(2, 1, MIX_W)
            s0 = jnp.pad(state_gla[:, o], ((0, 0), (0, 0), (0, 0), (0, HEAD_W - DK_D), (0, 0)))
            pp = _normmod_matmul(yp, g[0], mp[0], mp[1], w_in, tm).reshape(bp, lp, -1)
            ps = _normmod_matmul(ys, g[0], ms[0], ms[1], w_in, tm)
            oa_p, kc, vc = _attn_prompt(pp, diff_lambda[o], diff_norm[o], lam_init)
            oa_s = _attn_sample(ps, cache_diff_k[:, o], cache_diff_v[:, o], diff_lambda[o], diff_norm[o], lam_init, 256)
            ob_p, st_p = _gla_mixer(pp, aw, ab, gla_norm[o], None)
            ob_s, _ = _gla_mixer(ps, aw, ab, gla_norm[o], s0)
            outs["k"], outs["v"], outs["gla"] = kc, vc, st_p[:, :, :, :DK_D, :]
        yp = _out_proj(oa_p.reshape(1, bp * lp, -1), ob_p.reshape(1, bp * lp, -1), w_out, yp, mp[2], g[1], tm)
        ys = _out_proj(oa_s, ob_s, w_out, ys, ms[2], g[1], tm)
        up = ffn_up[l].astype(BF16)
        down = ffn_down[l].astype(BF16)
        cb = ffn_conv_b[l].reshape(1, -1)
        yp = _conv_ffn(yp, g[2], mp[3], mp[4], up, ffn_conv_w[l], cb, down, mp[5], g[3], lp, tm, 1408)
        ys = _conv_ffn(ys, g[2], ms[3], ms[4], up, ffn_conv_w[l], cb, down, ms[5], g[3], ls, tm, 1408)

    return (yp.reshape(bp, lp, d), ys, outs["hgrn"][:, None], outs["k"][:, None], outs["v"][:, None],
            outs["gla"][:, None])
```

```python
import functools
import math

import jax
import jax.numpy as jnp
import numpy as np
from jax import lax
from jax.experimental import pallas as pl
from jax.experimental.pallas import tpu as pltpu

F32 = jnp.float32
BF16 = jnp.bfloat16
HIGHEST = lax.Precision.HIGHEST

D_MODEL = 1024
DEPTH = 2
GRID_W = 64
N_HEADS = 4
HEAD_W = 128
MIX_W = N_HEADS * HEAD_W
W_B = 512
HY_ORDER = 2
HY_EMB = 33
HY_BANDS = (HY_EMB - 1) // 2
HY_FF = 64
HY_TARGET = 1e-2
HY_FAST = 0.3
HY_SLOW = 1.5
DH_C = 64
DK_D = 64
GLA_RANK = 16
GLA_TAU = 16.0
ROPE_BASE = 10000.0
D_FF = 2816
EPS = 1e-6

LANES = 128
SUBLANES = 8
VMEM_LIMIT = 56 * 1024 * 1024
SCAN_CHUNK = 64
SCAN_ROWS = 256
SCAN_GROUP = 2
NT_DIMS = (((1,), (1,)), ((), ()))
LOG2E = 1.4426950408889634


def _params(*sem):
    return pltpu.CompilerParams(dimension_semantics=sem, vmem_limit_bytes=VMEM_LIMIT)


def _silu(x):
    return x * (1.0 / (1.0 + jnp.exp(-x)))


def _rms(x, g):
    return x * lax.rsqrt(jnp.mean(x * x, axis=-1, keepdims=True) + EPS) * g


def _ada_kernel(c_ref, w_ref, b_ref, o_ref, *, n_rows):
    s = _silu(c_ref[...])
    w = w_ref[...]
    rows = [jnp.sum(w * s[:, r:r + 1], axis=0, keepdims=True) for r in range(n_rows)]
    rows.append(jnp.zeros((SUBLANES - n_rows, w.shape[1]), F32))
    o_ref[...] = jnp.concatenate(rows, axis=0) + b_ref[...]


def _ada_mod(cvec_t, n_rows, ada_w, ada_b):
    n = ada_w.shape[-1]
    tn = 1536
    return pl.pallas_call(
        functools.partial(_ada_kernel, n_rows=n_rows),
        grid=(DEPTH, n // tn),
        in_specs=[pl.BlockSpec((D_MODEL, SUBLANES), lambda l, j: (0, 0)),
                  pl.BlockSpec((None, D_MODEL, tn), lambda l, j: (l, 0, j)),
                  pl.BlockSpec((None, 1, tn), lambda l, j: (l, 0, j))],
        out_specs=pl.BlockSpec((None, SUBLANES, tn), lambda l, j: (l, 0, j)),
        out_shape=jax.ShapeDtypeStruct((DEPTH, SUBLANES, n), F32),
        compiler_params=_params("arbitrary", "arbitrary"),
    )(cvec_t, ada_w, ada_b.reshape(DEPTH, 1, n))


def _normmod_matmul_kernel(x_ref, g_ref, sh_ref, sc_ref, w_ref, o_ref):
    h = _rms(x_ref[...], g_ref[...]) * (1.0 + sc_ref[...]) + sh_ref[...]
    o_ref[...] = jnp.dot(h.astype(BF16), w_ref[...], preferred_element_type=F32)


def _mod_index(n_mod):
    return (lambda b, i: (b, 0, 0)) if n_mod > 1 else (lambda b, i: (0, 0, 0))


def _normmod_matmul(x, g, shift, scale, w, tm):
    b, l, d = x.shape
    n = w.shape[1]
    mod_spec = pl.BlockSpec((None, 1, d), _mod_index(shift.shape[0]))
    return pl.pallas_call(
        _normmod_matmul_kernel,
        grid=(b, l // tm),
        in_specs=[pl.BlockSpec((None, tm, d), lambda b, i: (b, i, 0)),
                  pl.BlockSpec((1, d), lambda b, i: (0, 0)),
                  mod_spec, mod_spec,
                  pl.BlockSpec((d, n), lambda b, i: (0, 0))],
        out_specs=pl.BlockSpec((None, tm, n), lambda b, i: (b, i, 0)),
        out_shape=jax.ShapeDtypeStruct((b, l, n), F32),
        compiler_params=_params("arbitrary", "arbitrary"),
    )(x, g, shift, scale, w)


def _out_proj_kernel(a_ref, b_ref, w_ref, y_ref, gate_ref, g_ref, o_ref):
    half = a_ref.shape[-1]
    m = jnp.dot(a_ref[...].astype(BF16), w_ref[:half, :], preferred_element_type=F32)
    m = m + jnp.dot(b_ref[...].astype(BF16), w_ref[half:, :], preferred_element_type=F32)
    o_ref[...] = y_ref[...] + gate_ref[...] * _rms(m, g_ref[...])


def _out_proj(a, bm, w, y, gate, g, tm):
    b, l, d = y.shape
    wa = a.shape[-1]
    return pl.pallas_call(
        _out_proj_kernel,
        grid=(b, l // tm),
        in_specs=[pl.BlockSpec((None, tm, wa), lambda b, i: (b, i, 0)),
                  pl.BlockSpec((None, tm, wa), lambda b, i: (b, i, 0)),
                  pl.BlockSpec((2 * wa, d), lambda b, i: (0, 0)),
                  pl.BlockSpec((None, tm, d), lambda b, i: (b, i, 0)),
                  pl.BlockSpec((None, 1, d), _mod_index(gate.shape[0])),
                  pl.BlockSpec((1, d), lambda b, i: (0, 0))],
        out_specs=pl.BlockSpec((None, tm, d), lambda b, i: (b, i, 0)),
        out_shape=jax.ShapeDtypeStruct((b, l, d), F32),
        compiler_params=_params("arbitrary", "arbitrary"),
    )(a, bm, w, y, gate, g)


def _patch_rows(x, keep, starts):
    pieces, r = [], 0
    for s in starts:
        if s > r:
            pieces.append(x[r:s])
        pieces.append(jnp.where(keep[s:s + SUBLANES], x[s:s + SUBLANES], 0.0))
        r = s + SUBLANES
    if r < x.shape[0]:
        pieces.append(x[r:])
    return jnp.concatenate(pieces, axis=0)


def _ffn_kernel(x_ref, xp_ref, xn_ref, g2_ref, sh_ref, sc_ref, ua_ref, ug_ref, cwa_ref, cwg_ref,
                cba_ref, cbg_ref, dn_ref, gate_ref, g3_ref, o_ref, h_scr, acc_scr, *, seq_len):
    i = pl.program_id(1)
    f = pl.program_id(2)
    tm = x_ref.shape[0]
    halo = xp_ref.shape[0]

    @pl.when(f == 0)
    def _():
        def nm(x):
            return (_rms(x, g2_ref[...]) * (1.0 + sc_ref[...]) + sh_ref[...]).astype(BF16)
        h_scr[0:halo, :] = nm(xp_ref[...])
        h_scr[halo:halo + tm, :] = nm(x_ref[...])
        h_scr[halo + tm:, :] = nm(xn_ref[...])
        acc_scr[...] = jnp.zeros_like(acc_scr)

    rows = tm + 2 * halo
    pos = (i * tm + lax.broadcasted_iota(jnp.int32, (tm, 1), 0)) % seq_len
    has_prev = pos != 0
    has_next = pos != seq_len - 1
    period = math.gcd(tm, seq_len)
    first_groups = list(range(0, tm, period))
    last_groups = [s + period - SUBLANES for s in first_groups]

    def conv(u_ref, cw_ref, cb_ref):
        u = jnp.dot(h_scr[...], u_ref[...], preferred_element_type=F32)
        up = _patch_rows(pltpu.roll(u, 1, 0)[halo:halo + tm], has_prev, first_groups)
        un = _patch_rows(pltpu.roll(u, rows - 1, 0)[halo:halo + tm], has_next, last_groups)
        uc = u[halo:halo + tm]
        return up * cw_ref[0:1, :] + uc * cw_ref[1:2, :] + un * cw_ref[2:3, :] + cb_ref[...]

    a = conv(ua_ref, cwa_ref, cba_ref)
    gt = conv(ug_ref, cwg_ref, cbg_ref)
    act = (_silu(gt) * a).astype(BF16)
    acc_scr[...] += jnp.dot(act, dn_ref[...], preferred_element_type=F32)

    @pl.when(f == pl.num_programs(2) - 1)
    def _():
        o_ref[...] = x_ref[...] + gate_ref[...] * _rms(acc_scr[...], g3_ref[...])


def _conv_ffn(y, g2, shift, scale, up, cw, cb, down, layer, gate, g3, seq_len, tm, tf):
    b, l, d = y.shape
    nf = D_FF // tf
    halo = SUBLANES
    hb = tm // halo
    last_hb = l // halo - 1
    mod_spec = pl.BlockSpec((None, 1, d), (lambda b, i, f: (b, 0, 0)) if shift.shape[0] > 1 else (lambda b, i, f: (0, 0, 0)))
    vec = lambda off: pl.BlockSpec((None, 1, tf), lambda b, i, f: (layer, 0, off + f))
    kern = functools.partial(_ffn_kernel, seq_len=seq_len)
    return pl.pallas_call(
        kern,
        grid=(b, l // tm, nf),
        in_specs=[pl.BlockSpec((None, tm, d), lambda b, i, f: (b, i, 0)),
                  pl.BlockSpec((None, halo, d), lambda b, i, f: (b, jnp.maximum(i * hb - 1, 0), 0)),
                  pl.BlockSpec((None, halo, d), lambda b, i, f: (b, jnp.minimum((i + 1) * hb, last_hb), 0)),
                  pl.BlockSpec((1, d), lambda b, i, f: (0, 0)),
                  mod_spec, mod_spec,
                  pl.BlockSpec((None, d, tf), lambda b, i, f: (layer, 0, f)),
                  pl.BlockSpec((None, d, tf), lambda b, i, f: (layer, 0, nf + f)),
                  pl.BlockSpec((None, 3, tf), lambda b, i, f: (layer, 0, f)),
                  pl.BlockSpec((None, 3, tf), lambda b, i, f: (layer, 0, nf + f)),
                  vec(0), vec(nf),
                  pl.BlockSpec((None, tf, d), lambda b, i, f: (layer, f, 0)),
                  mod_spec,
                  pl.BlockSpec((1, d), lambda b, i, f: (0, 0))],
        out_specs=pl.BlockSpec((None, tm, d), lambda b, i, f: (b, i, 0)),
        out_shape=jax.ShapeDtypeStruct((b, l, d), F32),
        scratch_shapes=[pltpu.VMEM((tm + 2 * halo, d), BF16), pltpu.VMEM((tm, d), F32)],
        compiler_params=_params("arbitrary", "arbitrary", "arbitrary"),
    )(y, y, y, g2, shift, scale, up, up, cw, cw, cb, cb, down, gate, g3)


def _scan_tables(c):
    nlev = int(math.log2(c))
    t = np.arange(c)[:, None]
    r = np.arange(c)[None, :]
    masks_f = [np.eye(c, dtype=bool)]
    for lev in range(1, nlev + 1):
        bsz = 2 ** lev
        mid = (t // bsz) * bsz + bsz // 2
        masks_f.append(((t // bsz) == (r // bsz)) & (t >= mid) & (r < mid))
    m_f = np.stack(masks_f).astype(np.float32)
    m_b = np.transpose(m_f, (0, 2, 1))
    tri = lambda a: jnp.asarray(np.concatenate([a, a, a], axis=1).astype(np.float32), dtype=BF16)
    two = lambda m: jnp.asarray(np.concatenate([m, m], axis=2))
    return tri(r <= t), tri(r >= t), two(m_f), two(m_b)


def _level_decay(cum, ncum, lg2, lev, bwd):
    c, w = cum.shape
    b = 1 << lev
    half = b // 2
    off = half - 1 + int(bwd)
    if b == 2:
        odd = lax.broadcasted_iota(jnp.int32, (c, 1), 0) % 2 == 1
        return jnp.where(odd != bwd, lg2, 0.0)
    if b >= 2 * SUBLANES:
        pieces = []
        for b0 in range(0, c, b):
            for rows, is_upper in ((slice(b0, b0 + half), False), (slice(b0 + half, b0 + b), True)):
                src = cum if is_upper != bwd else ncum
                pieces.append(src[rows] - jnp.broadcast_to(src[b0 + off:b0 + off + 1], (half, w)))
        return jnp.concatenate(pieces, axis=0)
    cum3 = cum.reshape(c // SUBLANES, SUBLANES, w)
    sub = lax.broadcasted_iota(jnp.int32, (1, SUBLANES, 1), 1)
    if b == SUBLANES:
        ref3 = jnp.broadcast_to(cum3[:, off:off + 1], cum3.shape)
    else:
        ref3 = jnp.where(sub < b, cum3[:, off:off + 1], cum3[:, b + off:b + off + 1])
    upper = (sub % b) >= half
    sgn = jnp.where(upper != bwd, 1.0, -1.0)
    return ((cum3 - ref3) * sgn).reshape(c, w)


def _pair_stack(xb):
    low = lax.broadcasted_iota(jnp.int32, (1, 2 * HEAD_W), 1) < HEAD_W
    zero = jnp.zeros_like(xb)
    return jnp.concatenate([jnp.where(low, xb, zero), jnp.where(low, zero, xb)], axis=0)


def _scan_group(chunks):
    pw = 2 * HEAD_W
    work = []
    for q, k, v, lg2, tri_ref, m_ref, st_ref, bwd in chunks:
        hi = lg2.astype(BF16)
        r1 = lg2 - hi.astype(F32)
        mid = r1.astype(BF16)
        lo = (r1 - mid.astype(F32)).astype(BF16)
        cum = jnp.dot(tri_ref[...], jnp.concatenate([hi, mid, lo], axis=0), preferred_element_type=F32)
        work.append(dict(q=q, k=k, v=v, lg2=lg2, cum=cum, ncum=-cum, m_ref=m_ref, st_ref=st_ref, bwd=bwd,
                         att=[None, None]))
    nlev = chunks[0][5].shape[0] - 1
    for lev in range(nlev + 1):
        for w in work:
            if lev == 0:
                qb, kb = w["q"].astype(BF16), w["k"].astype(BF16)
            else:
                e = jnp.exp2(_level_decay(w["cum"], w["ncum"], w["lg2"], lev, w["bwd"]))
                qb, kb = (w["q"] * e).astype(BF16), (w["k"] * e).astype(BF16)
            for p in range(N_HEADS // 2):
                sl = slice(p * pw, (p + 1) * pw)
                prod = lax.dot_general(qb[:, sl], _pair_stack(kb[:, sl]), NT_DIMS, preferred_element_type=F32)
                term = w["m_ref"][lev] * prod
                w["att"][p] = term if w["att"][p] is None else w["att"][p] + term
    results = []
    for w in work:
        q, k, v, cum, st_ref = w["q"], w["k"], w["v"], w["cum"], w["st_ref"]
        c = q.shape[0]
        last = 0 if w["bwd"] else c - 1
        e_cum = jnp.exp2(cum)
        d_last = e_cum[last:last + 1, :]
        qe = (q * e_cum).astype(BF16)
        kd = (k * jnp.exp2(cum[last:last + 1, :] - cum)).astype(BF16)
        vb = v.astype(BF16)
        outs = []
        for p in range(N_HEADS // 2):
            sl = slice(p * pw, (p + 1) * pw)
            st = st_ref[p]
            stb = st.astype(BF16)
            o_p = jnp.dot(w["att"][p].astype(BF16), _pair_stack(vb[:, sl]), preferred_element_type=F32)
            inter = [lax.dot_general(qe[:, p * pw + j * HEAD_W:p * pw + (j + 1) * HEAD_W],
                                     stb[:, j * HEAD_W:(j + 1) * HEAD_W], NT_DIMS, preferred_element_type=F32)
                     for j in range(2)]
            outs.append(o_p + jnp.concatenate(inter, axis=1))
            vstack = jnp.concatenate([v[:, p * pw:p * pw + HEAD_W], v[:, p * pw + HEAD_W:(p + 1) * pw]], axis=0)
            st_ref[p] = st * d_last[:, sl] + jnp.dot(vstack.T.astype(BF16), _pair_stack(kd[:, sl]),
                                                      preferred_element_type=F32)
        results.append(jnp.concatenate(outs, axis=1))
    return results


def _bidir_scan_body(load_fwd, load_bwd, g_ref, s0_ref, norm_ref, af_ref, ab_ref, mf_ref, mb_ref,
                     o_ref, s_out_ref, of_scr, ob_scr, stf_scr, stb_scr, seq_len):
    c = SCAN_CHUNK
    j = pl.program_id(1)
    nblk = pl.num_programs(1)
    n = SCAN_ROWS // c

    @pl.when(j == 0)
    def _():
        for p in range(N_HEADS // 2):
            if s0_ref is None:
                stf_scr[p] = jnp.zeros((HEAD_W, 2 * HEAD_W), F32)
                stb_scr[p] = jnp.zeros((HEAD_W, 2 * HEAD_W), F32)
            else:
                stf_scr[p] = jnp.concatenate([s0_ref[0, 2 * p].T, s0_ref[0, 2 * p + 1].T], axis=1)
                stb_scr[p] = jnp.concatenate([s0_ref[1, 2 * p].T, s0_ref[1, 2 * p + 1].T], axis=1)

    base_f = j * SCAN_ROWS
    base_b = (nblk - 1 - j) * SCAN_ROWS

    def step(i, carry):
        chunks, stores = [], []
        for u in range(SCAN_GROUP):
            rf = pl.multiple_of((i * SCAN_GROUP + u) * c, c)
            chunks.append(load_fwd(rf) + (af_ref, mf_ref, stf_scr, False))
            stores.append((of_scr, pl.multiple_of(base_f + rf, c)))
        for u in range(SCAN_GROUP):
            rb = pl.multiple_of((n - 1 - i * SCAN_GROUP - u) * c, c)
            chunks.append(load_bwd(rb) + (ab_ref, mb_ref, stb_scr, True))
            stores.append((ob_scr, pl.multiple_of(base_b + rb, c)))
        for (scr, r0), o in zip(stores, _scan_group(chunks)):
            scr[pl.ds(r0, c), :] = o
        return carry

    lax.fori_loop(0, n // SCAN_GROUP, step, 0)

    @pl.when(j == nblk - 1)
    def _():
        def fin(jj, carry):
            r0 = pl.multiple_of(jj * SCAN_ROWS, SCAN_ROWS)
            o = of_scr[pl.ds(r0, SCAN_ROWS), :] + ob_scr[pl.ds(r0, SCAN_ROWS), :]
            parts = []
            for h in range(N_HEADS):
                sl = slice(h * HEAD_W, (h + 1) * HEAD_W)
                parts.append(_rms(o[:, sl], norm_ref[:, sl]))
            o_ref[pl.ds(r0, SCAN_ROWS), :] = jnp.concatenate(parts, axis=1) * _silu(g_ref[pl.ds(r0, SCAN_ROWS), :])
            return carry

        lax.fori_loop(0, seq_len // SCAN_ROWS, fin, 0)
        for h in range(N_HEADS):
            p, hs = h // 2, slice((h % 2) * HEAD_W, (h % 2 + 1) * HEAD_W)
            dk = s_out_ref.shape[2]
            s_out_ref[0, h] = stf_scr[p][:, hs].T[:dk]
            s_out_ref[1, h] = stb_scr[p][:, hs].T[:dk]


def _hgrn_kernel(*refs, layer, has_s0, seq_len):
    qf_ref, ff_ref, if_ref, qb_ref, fb_ref, ib_ref, g_ref, lbraw_ref, norm_ref, af_ref, ab_ref, mf_ref, mb_ref = refs[:13]
    s0_ref = refs[13] if has_s0 else None
    o_ref, s_out_ref, of_scr, ob_scr, stf_scr, stb_scr = refs[13 + has_s0:]
    c = SCAN_CHUNK
    raw = lbraw_ref[...]
    ex = jnp.exp(raw - jnp.max(raw, axis=0, keepdims=True))
    sm = ex / jnp.sum(ex, axis=0, keepdims=True)
    lb = sm[0]
    for j in range(1, layer + 1):
        lb = lb + sm[j]

    def load(q_ref, f_ref, i_ref, lb_row):
        def fn(r0):
            q = _silu(q_ref[pl.ds(r0, c), :]) * (HEAD_W ** -0.5)
            sig = 1.0 / (1.0 + jnp.exp(-f_ref[pl.ds(r0, c), :]))
            f = lb_row + (1.0 - lb_row) * sig
            return q, 1.0 - f, i_ref[pl.ds(r0, c), :], jnp.log2(f)
        return fn

    _bidir_scan_body(load(qf_ref, ff_ref, if_ref, lb[0:1]), load(qb_ref, fb_ref, ib_ref, lb[1:2]), g_ref,
                     s0_ref, norm_ref, af_ref, ab_ref, mf_ref, mb_ref,
                     o_ref, s_out_ref, of_scr, ob_scr, stf_scr, stb_scr, seq_len)


def _gla_kernel(*refs, has_s0, seq_len):
    fwd_refs, bwd_refs = refs[0:4], refs[4:8]
    g_ref, aw_ref, ab_ref, norm_ref, af_ref, abk_ref, mf_ref, mb_ref = refs[8:16]
    s0_ref = refs[16] if has_s0 else None
    o_ref, s_out_ref, of_scr, ob_scr, stf_scr, stb_scr = refs[16 + has_s0:]
    c = SCAN_CHUNK

    def load(d, q_ref, k_ref, v_ref, da_ref):
        def fn(r0):
            q = q_ref[pl.ds(r0, c), :] * (DK_D ** -0.5)
            da = da_ref[pl.ds(r0, c), :]
            da_hi = da.astype(BF16)
            da_mid = (da - da_hi.astype(F32)).astype(BF16)
            xa = jnp.dot(jnp.concatenate([da_hi, da_hi, da_mid], axis=1), aw_ref[d],
                         preferred_element_type=F32) + ab_ref[d]
            la = (jnp.minimum(xa, 0.0) - jnp.log(1.0 + jnp.exp(-jnp.abs(xa)))) * (LOG2E / GLA_TAU)
            return q, k_ref[pl.ds(r0, c), :], v_ref[pl.ds(r0, c), :], la
        return fn

    _bidir_scan_body(load(0, *fwd_refs), load(1, *bwd_refs), g_ref,
                     s0_ref, norm_ref, af_ref, abk_ref, mf_ref, mb_ref,
                     o_ref, s_out_ref, of_scr, ob_scr, stf_scr, stb_scr, seq_len)


def _scan_call(kern, proj, streams, extra, s0, seq_len, dk_out):
    b = proj.shape[0]
    nblk = seq_len // SCAN_ROWS
    tabs = _scan_tables(SCAN_CHUNK)
    full = lambda a: pl.BlockSpec(a.shape, lambda i, j, _n=a.ndim: (0,) * _n)
    in_specs = []
    for cb, w, kind in streams:
        if kind == "f":
            in_specs.append(pl.BlockSpec((None, SCAN_ROWS, w), lambda i, j, _c=cb: (i, j, _c)))
        elif kind == "b":
            in_specs.append(pl.BlockSpec((None, SCAN_ROWS, w), lambda i, j, _c=cb: (i, nblk - 1 - j, _c)))
        else:
            in_specs.append(pl.BlockSpec((None, seq_len, w), lambda i, j, _c=cb: (i, 0, _c)))
    args = [proj] * len(streams)
    for a in tuple(extra) + tabs:
        in_specs.append(full(a))
        args.append(a)
    if s0 is not None:
        in_specs.append(pl.BlockSpec((None, 2, N_HEADS, HEAD_W, HEAD_W), lambda i, j: (i, 0, 0, 0, 0)))
        args.append(s0)
    return pl.pallas_call(
        kern,
        grid=(b, nblk),
        in_specs=in_specs,
        out_specs=[pl.BlockSpec((None, seq_len, MIX_W), lambda i, j: (i, 0, 0)),
                   pl.BlockSpec((None, None, 2, N_HEADS, dk_out, HEAD_W), lambda i, j: (i, 0, 0, 0, 0, 0))],
        out_shape=[jax.ShapeDtypeStruct((b, seq_len, MIX_W), F32),
                   jax.ShapeDtypeStruct((b, 1, 2, N_HEADS, dk_out, HEAD_W), F32)],
        scratch_shapes=[pltpu.VMEM((seq_len, MIX_W), F32), pltpu.VMEM((seq_len, MIX_W), F32),
                        pltpu.VMEM((N_HEADS // 2, HEAD_W, 2 * HEAD_W), F32),
                        pltpu.VMEM((N_HEADS // 2, HEAD_W, 2 * HEAD_W), F32)],
        compiler_params=_params("arbitrary", "arbitrary"),
    )(*args)


def _hgrn_mixer(proj, hgrn_lb, norm, s0, layer):
    seq_len = proj.shape[1]
    kern = functools.partial(_hgrn_kernel, layer=layer, has_s0=s0 is not None, seq_len=seq_len)
    streams = [(0, MIX_W, "f"), (1, MIX_W, "f"), (3, MIX_W, "f"),
               (0, MIX_W, "b"), (2, MIX_W, "b"), (3, MIX_W, "b"), (4, MIX_W, "w")]
    return _scan_call(kern, proj, streams, (hgrn_lb, norm.reshape(1, MIX_W)), s0, seq_len, HEAD_W)


def _gla_mixer(proj, aw_pad, ab_pad, norm, s0):
    seq_len = proj.shape[1]
    kern = functools.partial(_gla_kernel, has_s0=s0 is not None, seq_len=seq_len)
    da = 7 * MIX_W // LANES
    streams = [(3, MIX_W, "f"), (4, MIX_W, "f"), (5, MIX_W, "f"), (da, LANES, "f"),
               (3, MIX_W, "b"), (4, MIX_W, "b"), (5, MIX_W, "b"), (da, LANES, "b"), (6, MIX_W, "w")]
    return _scan_call(kern, proj, streams, (aw_pad, ab_pad, norm.reshape(1, MIX_W)), s0, seq_len, DK_D)


def _dwconv3_rows(x, w_ref, b_ref):
    l = x.shape[0]
    row = lax.broadcasted_iota(jnp.int32, (l, 1), 0)
    xp = jnp.where(row != 0, pltpu.roll(x, 1, 0), 0.0)
    xn = jnp.where(row != l - 1, pltpu.roll(x, l - 1, 0), 0.0)
    return xp * w_ref[0:1, :] + x * w_ref[1:2, :] + xn * w_ref[2:3, :] + b_ref[...]


def _hyena_filter_kernel(z_ref, w1_ref, b1_ref, w2_ref, b2_ref, w3_ref, fr_ref, dl_ref, o_ref):
    z = z_ref[...]
    fr = fr_ref[...]
    h = jnp.sin(fr * (jnp.dot(z, w1_ref[...], preferred_element_type=F32, precision=HIGHEST) + b1_ref[...]))
    h = jnp.sin(fr * (jnp.dot(h, w2_ref[...], preferred_element_type=F32, precision=HIGHEST) + b2_ref[...]))
    h_hi = h.astype(BF16)
    h_mid = (h - h_hi.astype(F32)).astype(BF16)
    h = jnp.dot(jnp.concatenate([h_hi, h_hi, h_mid], axis=1), w3_ref[...],
                preferred_element_type=F32)
    win = jnp.exp(-z[:, 0:1] * dl_ref[...])
    o_ref[...] = h * jnp.concatenate([win] * (2 * HY_ORDER), axis=1)


def _hyena_pos_features(l):
    t = jnp.linspace(0.0, 1.0, l, dtype=F32)[:, None]
    w = 2.0 * math.pi * jnp.arange(l, dtype=F32)[:, None] / l
    fb = jnp.linspace(1e-4, HY_BANDS - 1, HY_BANDS, dtype=F32)[None]
    z = jnp.concatenate([t, jnp.cos(fb * w), -jnp.sin(fb * w)], axis=-1)
    return jnp.pad(z, ((0, 0), (0, HY_FF - HY_EMB)))


def _hyena_filters(lens, w1, b1, w2, b2, w3, freq):
    z = jnp.concatenate([_hyena_pos_features(l) for l in lens], axis=0)
    rows = z.shape[0]
    tr = 256
    w1p = jnp.pad(w1, ((0, HY_FF - HY_EMB), (0, 0)))
    max_decay = math.log(HY_TARGET) / HY_FAST
    min_decay = math.log(HY_TARGET) / HY_SLOW
    deltas = jnp.abs(jnp.linspace(min_decay, max_decay, W_B, dtype=F32))[None]
    nout = w3.shape[1]
    w3_hi = w3.astype(BF16)
    w3_mid = (w3 - w3_hi.astype(F32)).astype(BF16)
    w3s = jnp.concatenate([w3_hi, w3_mid, w3_hi], axis=0)
    full = lambda a: pl.BlockSpec(a.shape, lambda i, _n=a.ndim: (0,) * _n)
    ins = (w1p, b1.reshape(1, -1), w2, b2.reshape(1, -1), w3s, freq.reshape(1, -1), deltas)
    return pl.pallas_call(
        _hyena_filter_kernel,
        grid=(rows // tr,),
        in_specs=[pl.BlockSpec((tr, HY_FF), lambda i: (i, 0))] + [full(a) for a in ins],
        out_specs=pl.BlockSpec((tr, nout), lambda i: (i, 0)),
        out_shape=jax.ShapeDtypeStruct((rows, nout), F32),
        compiler_params=_params("arbitrary"),
    )(z, *ins)


def _dft_tables(l):
    n = 2 * l
    k = jnp.arange(l, dtype=jnp.int32)[:, None]
    t1 = jnp.arange(LANES, dtype=jnp.int32)[None, :]
    t2 = (jnp.arange(l // LANES, dtype=jnp.int32) * LANES)[None, :]
    ang = lambda m: (m % n).astype(F32) * (2.0 * math.pi / n)
    ca, sa = jnp.cos(ang(k * t1))[:, None, :], jnp.sin(ang(k * t1))[:, None, :]
    cb, sb = jnp.cos(ang(k * t2))[:, :, None], jnp.sin(ang(k * t2))[:, :, None]
    cos_t = (ca * cb - sa * sb).reshape(l, l)
    sin_t = (sa * cb + ca * sb).reshape(l, l)
    row = lax.broadcasted_iota(jnp.int32, (l, 1), 0)
    col = lax.broadcasted_iota(jnp.int32, (1, l), 1)
    sin_f = jnp.where(row == 0, jnp.where(col % 2 == 0, 1.0, -1.0), sin_t)
    sin_i = jnp.where(col == 0, jnp.where(row % 2 == 0, 1.0, -1.0), sin_t)
    return cos_t.astype(BF16), sin_f.astype(BF16), sin_i.astype(BF16)


def _hyena_spectrum_kernel(c_ref, s_ref, f0_ref, f1_ref, kr_ref, kia_ref, krb_ref, *, seq_len):
    kt = pl.program_id(1)
    tk = c_ref.shape[0]
    row = lax.broadcasted_iota(jnp.int32, (seq_len, 1), 0)
    f0 = f0_ref[...].astype(BF16)
    f1 = jnp.where(row != 0, f1_ref[...], 0.0).astype(BF16)
    c = c_ref[...]
    s = s_ref[...]
    p0 = jnp.dot(c, f0, preferred_element_type=F32)
    q0 = jnp.dot(s, f0, preferred_element_type=F32)
    p1 = jnp.dot(c, f1, preferred_element_type=F32)
    q1 = jnp.dot(s, f1, preferred_element_type=F32)
    krow = kt * tk + lax.broadcasted_iota(jnp.int32, (tk, 1), 0)
    dc = krow == 0
    wk = jnp.where(dc, 1.0, 2.0) * (1.0 / (2 * seq_len))
    kr = p0 + p1
    kr_ref[...] = kr * wk
    kia_ref[...] = jnp.where(dc, 0.0, q1 - q0) * wk
    krb_ref[...] = jnp.where(dc, q0 + q1, kr) * wk


def _hyena_spectrum(filt, row0, seq_len, tables, tk):
    cos_t, sin_f, _ = tables
    rb = row0 // seq_len
    out = jax.ShapeDtypeStruct((HY_ORDER, seq_len, W_B), F32)
    kern = functools.partial(_hyena_spectrum_kernel, seq_len=seq_len)
    ospec = pl.BlockSpec((None, tk, W_B), lambda o, kt: (o, kt, 0))
    return pl.pallas_call(
        kern,
        grid=(HY_ORDER, seq_len // tk),
        in_specs=[pl.BlockSpec((tk, seq_len), lambda o, kt: (kt, 0)),
                  pl.BlockSpec((tk, seq_len), lambda o, kt: (kt, 0)),
                  pl.BlockSpec((seq_len, W_B), lambda o, kt: (rb, 2 * o)),
                  pl.BlockSpec((seq_len, W_B), lambda o, kt: (rb, 2 * o + 1))],
        out_specs=[ospec, ospec, ospec],
        out_shape=[out, out, out],
        compiler_params=_params("arbitrary", "arbitrary"),
    )(cos_t, sin_f, filt, filt)


def _hyena_order_kernel(zin_ref, gate_ref, cwz_ref, cbz_ref, cwg_ref, cbg_ref, d_ref, kr_ref, kia_ref, krb_ref,
                        cf_ref, sf_ref, ci_ref, si_ref, o_ref, z_scr, zb_scr, acc_scr, *, conv_input):
    kt = pl.program_id(1)

    @pl.when(kt == 0)
    def _():
        z = zin_ref[...]
        if conv_input:
            z = _dwconv3_rows(z, cwz_ref, cbz_ref)
        z_scr[...] = z
        zb_scr[...] = z.astype(BF16)
        acc_scr[...] = jnp.zeros_like(acc_scr)

    zb = zb_scr[...]
    p = jnp.dot(cf_ref[...], zb, preferred_element_type=F32)
    q = jnp.dot(sf_ref[...], zb, preferred_element_type=F32)
    kia = kia_ref[...]
    yr = (p * kr_ref[...] + q * kia).astype(BF16)
    yi = (q * krb_ref[...] - p * kia).astype(BF16)
    acc_scr[...] += (jnp.dot(ci_ref[...], yr, preferred_element_type=F32)
                     + jnp.dot(si_ref[...], yi, preferred_element_type=F32))

    @pl.when(kt == pl.num_programs(1) - 1)
    def _():
        gate = _dwconv3_rows(gate_ref[...], cwg_ref, cbg_ref)
        o_ref[...] = gate * (acc_scr[...] + z_scr[...] * d_ref[...])


def _hyena_order(zin, zin_col, proj, order, conv_w, conv_b, hy_d, spectrum, tables, tk):
    b, seq_len = proj.shape[0], proj.shape[1]
    cos_t, sin_f, sin_i = tables
    kr, kia, krb = spectrum
    hy0 = 5
    conv_input = order == 0
    cw = conv_w.reshape(3, 1 + HY_ORDER, W_B).transpose(1, 0, 2)
    cbias = conv_b.reshape(1 + HY_ORDER, 1, W_B)
    kern = functools.partial(_hyena_order_kernel, conv_input=conv_input)
    kspec = pl.BlockSpec((None, tk, W_B), lambda i, kt: (order, kt, 0))
    return pl.pallas_call(
        kern,
        grid=(b, seq_len // tk),
        in_specs=[pl.BlockSpec((None, seq_len, W_B), lambda i, kt: (i, 0, zin_col)),
                  pl.BlockSpec((None, seq_len, W_B), lambda i, kt: (i, 0, hy0 + 1 + order)),
                  pl.BlockSpec((None, 3, W_B), lambda i, kt: (0, 0, 0)),
                  pl.BlockSpec((None, 1, W_B), lambda i, kt: (0, 0, 0)),
                  pl.BlockSpec((None, 3, W_B), lambda i, kt: (1 + order, 0, 0)),
                  pl.BlockSpec((None, 1, W_B), lambda i, kt: (1 + order, 0, 0)),
                  pl.BlockSpec((None, 1, W_B), lambda i, kt: (order, 0, 0)),
                  kspec, kspec, kspec,
                  pl.BlockSpec((tk, seq_len), lambda i, kt: (kt, 0)),
                  pl.BlockSpec((tk, seq_len), lambda i, kt: (kt, 0)),
                  pl.BlockSpec((seq_len, tk), lambda i, kt: (0, kt)),
                  pl.BlockSpec((seq_len, tk), lambda i, kt: (0, kt))],
        out_specs=pl.BlockSpec((None, seq_len, W_B), lambda i, kt: (i, 0, 0)),
        out_shape=jax.ShapeDtypeStruct((b, seq_len, W_B), F32),
        scratch_shapes=[pltpu.VMEM((seq_len, W_B), F32), pltpu.VMEM((seq_len, W_B), BF16),
                        pltpu.VMEM((seq_len, W_B), F32)],
        compiler_params=_params("arbitrary", "arbitrary"),
    )(zin, proj, cw, cbias, cw, cbias, hy_d.reshape(HY_ORDER, 1, W_B), kr, kia, krb, cos_t, sin_f, cos_t, sin_i)


def _hyena_short_kernel(v_ref, x1_ref, x2_ref, cw_ref, cb_ref, d_ref, kr_ref, kia_ref, krb_ref,
                        cf_ref, sf_ref, si_ref, o_ref):
    cf, sf, si = cf_ref[...], sf_ref[...], si_ref[...]
    z = _dwconv3_rows(v_ref[...], cw_ref.at[0], cb_ref.at[0])
    for order, gate_ref in enumerate((x1_ref, x2_ref)):
        zb = z.astype(BF16)
        p = jnp.dot(cf, zb, preferred_element_type=F32)
        q = jnp.dot(sf, zb, preferred_element_type=F32)
        kia = kia_ref[order]
        yr = (p * kr_ref[order] + q * kia).astype(BF16)
        yi = (q * krb_ref[order] - p * kia).astype(BF16)
        conv = jnp.dot(cf, yr, preferred_element_type=F32) + jnp.dot(si, yi, preferred_element_type=F32)
        gate = _dwconv3_rows(gate_ref[...], cw_ref.at[1 + order], cb_ref.at[1 + order])
        z = gate * (conv + z * d_ref[order])
    o_ref[...] = z


def _hyena_short(proj, conv_w, conv_b, hy_d, spectrum, tables):
    b, seq_len = proj.shape[0], proj.shape[1]
    cos_t, sin_f, sin_i = tables
    hy0 = 5
    cw = conv_w.reshape(3, 1 + HY_ORDER, W_B).transpose(1, 0, 2)
    cbias = conv_b.reshape(1 + HY_ORDER, 1, W_B)
    full = lambda a: pl.BlockSpec(a.shape, lambda i, _n=a.ndim: (0,) * _n)
    col = lambda j: pl.BlockSpec((None, seq_len, W_B), lambda i: (i, 0, hy0 + j))
    consts = (cw, cbias, hy_d.reshape(HY_ORDER, 1, W_B)) + tuple(spectrum) + (cos_t, sin_f, sin_i)
    return pl.pallas_call(
        _hyena_short_kernel,
        grid=(b,),
        in_specs=[col(0), col(1), col(2)] + [full(a) for a in consts],
        out_specs=pl.BlockSpec((None, seq_len, W_B), lambda i: (i, 0, 0)),
        out_shape=jax.ShapeDtypeStruct((b, seq_len, W_B), F32),
        compiler_params=_params("arbitrary"),
    )(proj, proj, proj, *consts)


def _hyena_mixer(proj, conv_w, conv_b, hy_d, spectrum, tables, tk):
    if tk == proj.shape[1]:
        return _hyena_short(proj, conv_w, conv_b, hy_d, spectrum, tables)
    z = _hyena_order(proj, 5, proj, 0, conv_w, conv_b, hy_d, spectrum, tables, tk)
    return _hyena_order(z, 0, proj, 1, conv_w, conv_b, hy_d, spectrum, tables, tk)


def _diff_lambda(lp_ref, lam_init):
    lp = lp_ref[...]
    a = jnp.sum(lp[0:1] * lp[1:2], axis=-1, keepdims=True)
    b = jnp.sum(lp[2:3] * lp[3:4], axis=-1, keepdims=True)
    return jnp.exp(a) - jnp.exp(b) + lam_init


def _diff_attend(q, keys_b, vals_b, lam):
    lane = lax.broadcasted_iota(jnp.int32, (1, 2 * DH_C), 1)
    qs = q * (DH_C ** -0.5 * LOG2E)

    def attend(sel):
        s = lax.dot_general(jnp.where(sel, qs, 0.0).astype(BF16), keys_b, NT_DIMS, preferred_element_type=F32)
        e = jnp.exp2(s - jnp.max(s, axis=-1, keepdims=True))
        den = jnp.sum(e, axis=-1, keepdims=True)
        return jnp.dot(e.astype(BF16), vals_b, preferred_element_type=F32) / den

    return attend(lane < DH_C) - lam * attend(lane >= DH_C)


def _attn_prompt_kernel(q_ref, k_ref, v_ref, lp_ref, norm_ref, o_ref, kc_ref, vc_ref, *, lam_init):
    hw = 2 * DH_C
    lam = _diff_lambda(lp_ref, lam_init)
    for h in range(N_HEADS):
        sl = slice(h * hw, (h + 1) * hw)
        k = k_ref[:, sl]
        v = v_ref[:, sl]
        o = _diff_attend(q_ref[:, sl], k.astype(BF16), v.astype(BF16), lam)
        o_ref[:, sl] = _rms(o, norm_ref[:, sl]) * (1.0 - lam_init)
        kc_ref[h] = k
        vc_ref[h] = v


def _attn_prompt(proj, diff_lambda, diff_norm, lam_init):
    b, seq_len = proj.shape[0], proj.shape[1]
    hw = 2 * DH_C
    w = N_HEADS * hw
    kern = functools.partial(_attn_prompt_kernel, lam_init=lam_init)
    col = lambda j: pl.BlockSpec((None, seq_len, w), lambda i: (i, 0, j))
    cache_spec = pl.BlockSpec((None, None, N_HEADS, seq_len, hw), lambda i: (i, 0, 0, 0, 0))
    cache_shape = jax.ShapeDtypeStruct((b, 1, N_HEADS, seq_len, hw), F32)
    return pl.pallas_call(
        kern,
        grid=(b,),
        in_specs=[col(0), col(1), col(2),
                  pl.BlockSpec((4, DH_C), lambda i: (0, 0)),
                  pl.BlockSpec((1, w), lambda i: (0, 0))],
        out_specs=[pl.BlockSpec((None, seq_len, w), lambda i: (i, 0, 0)), cache_spec, cache_spec],
        out_shape=[jax.ShapeDtypeStruct((b, seq_len, w), F32), cache_shape, cache_shape],
        compiler_params=_params("arbitrary"),
    )(proj, proj, proj, diff_lambda, diff_norm.reshape(1, -1))


def _rope(x, cos, sin_signed):
    lane = lax.broadcasted_iota(jnp.int32, (1, x.shape[-1]), 1)
    first = (lane % 32) < 16
    partner = jnp.where(first, pltpu.roll(x, x.shape[-1] - 16, 1), pltpu.roll(x, 16, 1))
    return x * cos + partner * sin_signed


def _attn_sample_kernel(q_ref, k_ref, v_ref, ck_ref, cv_ref, cosq_ref, sinq_ref, cosk_ref, sink_ref,
                        lp_ref, norm_ref, o_ref, keys_scr, vals_scr, *, lam_init):
    past = ck_ref.shape[1]
    hw = 2 * DH_C

    @pl.when(pl.program_id(1) == 0)
    def _():
        for h in range(N_HEADS):
            sl = slice(h * hw, (h + 1) * hw)
            keys_scr[h, 0:past, :] = ck_ref[h].astype(BF16)
            vals_scr[h, 0:past, :] = cv_ref[h].astype(BF16)
            keys_scr[h, past:, :] = _rope(k_ref[:, sl], cosk_ref[...], sink_ref[...]).astype(BF16)
            vals_scr[h, past:, :] = v_ref[:, sl].astype(BF16)

    lam = _diff_lambda(lp_ref, lam_init)
    for h in range(N_HEADS):
        sl = slice(h * hw, (h + 1) * hw)
        q = _rope(q_ref[:, sl], cosq_ref[...], sinq_ref[...])
        o = _diff_attend(q, keys_scr[h], vals_scr[h], lam)
        o_ref[:, sl] = _rms(o, norm_ref[:, sl]) * (1.0 - lam_init)


def _rope_tables(seq_len):
    pos = jnp.arange(seq_len)
    row = (pos // GRID_W).astype(F32)
    col = (pos % GRID_W).astype(F32)
    half = DH_C // 2
    inv = ROPE_BASE ** (-jnp.arange(0, half, 2, dtype=F32) / half)
    def comp(p):
        ang = p[:, None] * inv[None]
        c, s = jnp.cos(ang), jnp.sin(ang)
        return jnp.concatenate([c, c], axis=-1), jnp.concatenate([-s, s], axis=-1)
    cr, sr = comp(row)
    cc, sc = comp(col)
    cos = jnp.concatenate([cr, cc, cr, cc], axis=-1)
    sin = jnp.concatenate([sr, sc, sr, sc], axis=-1)
    return cos, sin


def _attn_sample(proj, ctx_k, ctx_v, diff_lambda, diff_norm, lam_init, tq):
    b, seq_len = proj.shape[0], proj.shape[1]
    past = ctx_k.shape[2]
    hw = 2 * DH_C
    w = N_HEADS * hw
    cos, sin = _rope_tables(seq_len)
    kern = functools.partial(_attn_sample_kernel, lam_init=lam_init)
    ctx_spec = pl.BlockSpec((None, N_HEADS, past, hw), lambda i, j: (i, 0, 0, 0))
    return pl.pallas_call(
        kern,
        grid=(b, seq_len // tq),
        in_specs=[pl.BlockSpec((None, tq, w), lambda i, j: (i, j, 0)),
                  pl.BlockSpec((None, seq_len, w), lambda i, j: (i, 0, 1)),
                  pl.BlockSpec((None, seq_len, w), lambda i, j: (i, 0, 2)),
                  ctx_spec, ctx_spec,
                  pl.BlockSpec((tq, hw), lambda i, j: (j, 0)),
                  pl.BlockSpec((tq, hw), lambda i, j: (j, 0)),
                  pl.BlockSpec((seq_len, hw), lambda i, j: (0, 0)),
                  pl.BlockSpec((seq_len, hw), lambda i, j: (0, 0)),
                  pl.BlockSpec((4, DH_C), lambda i, j: (0, 0)),
                  pl.BlockSpec((1, w), lambda i, j: (0, 0))],
        out_specs=pl.BlockSpec((None, tq, w), lambda i, j: (i, j, 0)),
        out_shape=jax.ShapeDtypeStruct((b, seq_len, w), F32),
        scratch_shapes=[pltpu.VMEM((N_HEADS, past + seq_len, hw), BF16),
                        pltpu.VMEM((N_HEADS, past + seq_len, hw), BF16)],
        compiler_params=_params("arbitrary", "arbitrary"),
    )(proj, proj, proj, ctx_k, ctx_v, cos, sin, cos, sin, diff_lambda, diff_norm.reshape(1, -1))


def _pad_heads(w, axis=-1):
    w = jnp.moveaxis(w, axis, -1)
    lead = w.shape[:-1]
    w = w.reshape(*lead, N_HEADS, DK_D)
    w = jnp.pad(w, [(0, 0)] * len(lead) + [(0, 0), (0, HEAD_W - DK_D)])
    return jnp.moveaxis(w.reshape(*lead, N_HEADS * HEAD_W), -1, axis)


def _odd_w_in(w):
    cq, ck, cv, dq, dk, dv, dg, da = jnp.split(w, [512, 1024, 1536, 1792, 2048, 2560, 3072], axis=1)
    da = jnp.pad(da, ((0, 0), (0, LANES - 2 * GLA_RANK)))
    return jnp.concatenate([cq, ck, cv, _pad_heads(dq), _pad_heads(dk), dv, dg, da], axis=1)


def kernel(x_prompt, x_sample, state_hgrn, cache_diff_k, cache_diff_v, state_gla, c, c_ctx, ada_w, ada_b, norm_g, ffn_up, ffn_conv_w, ffn_conv_b, ffn_down, w_in_even, w_out_even, hgrn_lb, hgrn_norm, hy_conv_w, hy_conv_b, hy_w1, hy_b1, hy_w2, hy_b2, hy_w3, hy_freq, hy_d, w_in_odd, w_out_odd, diff_lambda, diff_norm, gla_aw, gla_ab, gla_norm):
    bp, lp, d = x_prompt.shape
    bs, ls, _ = x_sample.shape

    cvec_t = jnp.zeros((d, SUBLANES), F32).at[:, 0].set(c_ctx).at[:, 1:1 + bs].set(c.T)
    mod = _ada_mod(cvec_t, 1 + bs, ada_w, ada_b)

    yp = x_prompt.reshape(1, bp * lp, d)
    ys = x_sample
    tm = 512

    filt = _hyena_filters((ls, lp), hy_w1[0], hy_b1[0], hy_w2[0], hy_b2[0], hy_w3[0], hy_freq[0])
    tab_p, tab_s = _dft_tables(lp), _dft_tables(ls)
    spec_s = _hyena_spectrum(filt, 0, ls, tab_s, 512)
    spec_p = _hyena_spectrum(filt, ls, lp, tab_p, lp)

    ffn_up_b, ffn_down_b = ffn_up.astype(BF16), ffn_down.astype(BF16)
    ffn_cb = ffn_conv_b.reshape(DEPTH, 1, -1)
    outs = {}
    for l in range(DEPTH):
        m = mod[l].reshape(SUBLANES, 6, 1, d)
        mp = [m[0:1, j] for j in range(6)]
        ms = [m[1:1 + bs, j] for j in range(6)]
        g = [norm_g[l, j].reshape(1, d) for j in range(4)]
        if l % 2 == 0:
            e = l // 2
            w_in = w_in_even[e].astype(BF16)
            w_out = w_out_even[e].astype(BF16)
            pp = _normmod_matmul(yp, g[0], mp[0], mp[1], w_in, tm).reshape(bp, lp, -1)
            ps = _normmod_matmul(ys, g[0], ms[0], ms[1], w_in, tm)
            oa_p, st_p = _hgrn_mixer(pp, hgrn_lb, hgrn_norm[e], None, l)
            oa_s, _ = _hgrn_mixer(ps, hgrn_lb, hgrn_norm[e], state_hgrn[:, e], l)
            ob_p = _hyena_mixer(pp, hy_conv_w[e], hy_conv_b[e], hy_d[e], spec_p, tab_p, lp)
            ob_s = _hyena_mixer(ps, hy_conv_w[e], hy_conv_b[e], hy_d[e], spec_s, tab_s, 256)
            outs["hgrn"] = st_p
        else:
            o = l // 2
            lam_init = 0.8 - 0.6 * math.exp(-0.3 * l)
            w_in = _odd_w_in(w_in_odd[o].astype(BF16))
            w_out = w_out_odd[o].astype(BF16)
            aw = jnp.zeros((2, LANES, MIX_W), F32)
            aw = aw.at[0, 0:GLA_RANK].set(_pad_heads(gla_aw[o, 0])).at[1, GLA_RANK:2 * GLA_RANK].set(_pad_heads(gla_aw[o, 1]))
            aw_hi = aw.astype(BF16)
            aw_mid = (aw - aw_hi.astype(F32)).astype(BF16)
            aw = jnp.concatenate([aw_hi, aw_mid, aw_hi], axis=1)
            ab = _pad_heads(gla_ab[o]).reshape(2, 1, MIX_W)
            s0 = jnp.pad(state_gla[:, o], ((0, 0), (0, 0), (0, 0), (0, HEAD_W - DK_D), (0, 0)))
            pp = _normmod_matmul(yp, g[0], mp[0], mp[1], w_in, tm).reshape(bp, lp, -1)
            ps = _normmod_matmul(ys, g[0], ms[0], ms[1], w_in, tm)
            oa_p, kc, vc = _attn_prompt(pp, diff_lambda[o], diff_norm[o], lam_init)
            oa_s = _attn_sample(ps, cache_diff_k[:, o], cache_diff_v[:, o], diff_lambda[o], diff_norm[o], lam_init, 256)
            ob_p, st_p = _gla_mixer(pp, aw, ab, gla_norm[o], None)
            ob_s, _ = _gla_mixer(ps, aw, ab, gla_norm[o], s0)
            outs["k"], outs["v"], outs["gla"] = kc, vc, st_p
        yp = _out_proj(oa_p.reshape(1, bp * lp, -1), ob_p.reshape(1, bp * lp, -1), w_out, yp, mp[2], g[1], tm)
        ys = _out_proj(oa_s, ob_s, w_out, ys, ms[2], g[1], tm)
        yp = _conv_ffn(yp, g[2], mp[3], mp[4], ffn_up_b, ffn_conv_w, ffn_cb, ffn_down_b, l, mp[5], g[3], lp, tm, 1408)
        ys = _conv_ffn(ys, g[2], ms[3], ms[4], ffn_up_b, ffn_conv_w, ffn_cb, ffn_down_b, l, ms[5], g[3], ls, tm, 1408)

    return (yp.reshape(bp, lp, d), ys, outs["hgrn"], outs["k"], outs["v"], outs["gla"])
```

```python
import functools
import math

import jax
import jax.numpy as jnp
import numpy as np
from jax import lax
from jax.experimental import pallas as pl
from jax.experimental.pallas import tpu as pltpu

F32 = jnp.float32
BF16 = jnp.bfloat16
HIGHEST = lax.Precision.HIGHEST

D_MODEL = 1024
DEPTH = 2
GRID_W = 64
N_HEADS = 4
HEAD_W = 128
MIX_W = N_HEADS * HEAD_W
W_B = 512
HY_ORDER = 2
HY_EMB = 33
HY_BANDS = (HY_EMB - 1) // 2
HY_FF = 64
HY_TARGET = 1e-2
HY_FAST = 0.3
HY_SLOW = 1.5
DH_C = 64
DK_D = 64
GLA_RANK = 16
GLA_TAU = 16.0
ROPE_BASE = 10000.0
D_FF = 2816
EPS = 1e-6

LANES = 128
SUBLANES = 8
VMEM_LIMIT = 56 * 1024 * 1024
SCAN_CHUNK = 64
SCAN_ROWS = 256
SCAN_GROUP = 2
NT_DIMS = (((1,), (1,)), ((), ()))
LOG2E = 1.4426950408889634


def _params(*sem):
    return pltpu.CompilerParams(dimension_semantics=sem, vmem_limit_bytes=VMEM_LIMIT)


def _silu(x):
    return x * (1.0 / (1.0 + jnp.exp(-x)))


def _rms(x, g):
    return x * lax.rsqrt(jnp.mean(x * x, axis=-1, keepdims=True) + EPS) * g


def _ada_kernel(c_ref, w_ref, b_ref, o_ref, *, n_rows):
    s = _silu(c_ref[...])
    w = w_ref[...]
    rows = [jnp.sum(w * s[:, r:r + 1], axis=0, keepdims=True) for r in range(n_rows)]
    rows.append(jnp.zeros((SUBLANES - n_rows, w.shape[1]), F32))
    o_ref[...] = jnp.concatenate(rows, axis=0) + b_ref[...]


def _ada_mod(cvec_t, n_rows, ada_w, ada_b):
    n = ada_w.shape[-1]
    tn = 1536
    return pl.pallas_call(
        functools.partial(_ada_kernel, n_rows=n_rows),
        grid=(DEPTH, n // tn),
        in_specs=[pl.BlockSpec((D_MODEL, SUBLANES), lambda l, j: (0, 0)),
                  pl.BlockSpec((None, D_MODEL, tn), lambda l, j: (l, 0, j)),
                  pl.BlockSpec((None, 1, tn), lambda l, j: (l, 0, j))],
        out_specs=pl.BlockSpec((None, SUBLANES, tn), lambda l, j: (l, 0, j)),
        out_shape=jax.ShapeDtypeStruct((DEPTH, SUBLANES, n), F32),
        compiler_params=_params("arbitrary", "arbitrary"),
    )(cvec_t, ada_w, ada_b.reshape(DEPTH, 1, n))


def _normmod_matmul_kernel(x_ref, g_ref, sh_ref, sc_ref, w_ref, o_ref):
    h = _rms(x_ref[...], g_ref[...]) * (1.0 + sc_ref[...]) + sh_ref[...]
    o_ref[...] = jnp.dot(h.astype(BF16), w_ref[...], preferred_element_type=F32)


def _mod_index(n_mod):
    return (lambda b, i: (b, 0, 0)) if n_mod > 1 else (lambda b, i: (0, 0, 0))


def _normmod_matmul(x, g, shift, scale, w, tm):
    b, l, d = x.shape
    n = w.shape[1]
    mod_spec = pl.BlockSpec((None, 1, d), _mod_index(shift.shape[0]))
    return pl.pallas_call(
        _normmod_matmul_kernel,
        grid=(b, l // tm),
        in_specs=[pl.BlockSpec((None, tm, d), lambda b, i: (b, i, 0)),
                  pl.BlockSpec((1, d), lambda b, i: (0, 0)),
                  mod_spec, mod_spec,
                  pl.BlockSpec((d, n), lambda b, i: (0, 0))],
        out_specs=pl.BlockSpec((None, tm, n), lambda b, i: (b, i, 0)),
        out_shape=jax.ShapeDtypeStruct((b, l, n), F32),
        compiler_params=_params("arbitrary", "arbitrary"),
    )(x, g, shift, scale, w)


def _patch_rows(x, keep, starts):
    pieces, r = [], 0
    for s in starts:
        if s > r:
            pieces.append(x[r:s])
        pieces.append(jnp.where(keep[s:s + SUBLANES], x[s:s + SUBLANES], 0.0))
        r = s + SUBLANES
    if r < x.shape[0]:
        pieces.append(x[r:])
    return jnp.concatenate(pieces, axis=0)


def _mix_ffn_kernel(y_ref, yp_ref, yn_ref, a_ref, ap_ref, an_ref, b_ref, bp_ref, bn_ref, wo_ref, gate1_ref, g1_ref,
                    g2_ref, sh_ref, sc_ref, ua_ref, ug_ref, cwa_ref, cwg_ref, cba_ref, cbg_ref, dn_ref, gate2_ref,
                    g3_ref, o_ref, y1_scr, h_scr, acc_scr, *, seq_len):
    i = pl.program_id(1)
    f = pl.program_id(2)
    tm = y_ref.shape[0]
    halo = yp_ref.shape[0]
    half = a_ref.shape[1]

    @pl.when(f == 0)
    def _():
        def mixed(y_r, a_r, b_r):
            m = jnp.dot(a_r[...].astype(BF16), wo_ref[:half, :], preferred_element_type=F32)
            m = m + jnp.dot(b_r[...].astype(BF16), wo_ref[half:, :], preferred_element_type=F32)
            return y_r[...] + gate1_ref[...] * _rms(m, g1_ref[...])

        def nm(x):
            return (_rms(x, g2_ref[...]) * (1.0 + sc_ref[...]) + sh_ref[...]).astype(BF16)
        y1 = mixed(y_ref, a_ref, b_ref)
        y1_scr[...] = y1
        h_scr[0:halo, :] = nm(mixed(yp_ref, ap_ref, bp_ref))
        h_scr[halo:halo + tm, :] = nm(y1)
        h_scr[halo + tm:, :] = nm(mixed(yn_ref, an_ref, bn_ref))
        acc_scr[...] = jnp.zeros_like(acc_scr)

    rows = tm + 2 * halo
    pos = (i * tm + lax.broadcasted_iota(jnp.int32, (tm, 1), 0)) % seq_len
    has_prev = pos != 0
    has_next = pos != seq_len - 1
    period = math.gcd(tm, seq_len)
    first_groups = list(range(0, tm, period))
    last_groups = [s + period - SUBLANES for s in first_groups]

    def conv(u_ref, cw_ref, cb_ref):
        u = jnp.dot(h_scr[...], u_ref[...], preferred_element_type=F32)
        up = _patch_rows(pltpu.roll(u, 1, 0)[halo:halo + tm], has_prev, first_groups)
        un = _patch_rows(pltpu.roll(u, rows - 1, 0)[halo:halo + tm], has_next, last_groups)
        uc = u[halo:halo + tm]
        return up * cw_ref[0:1, :] + uc * cw_ref[1:2, :] + un * cw_ref[2:3, :] + cb_ref[...]

    a = conv(ua_ref, cwa_ref, cba_ref)
    gt = conv(ug_ref, cwg_ref, cbg_ref)
    act = (_silu(gt) * a).astype(BF16)
    acc_scr[...] += jnp.dot(act, dn_ref[...], preferred_element_type=F32)

    @pl.when(f == pl.num_programs(2) - 1)
    def _():
        o_ref[...] = y1_scr[...] + gate2_ref[...] * _rms(acc_scr[...], g3_ref[...])


def _mix_ffn(y, a, bm, w_out, gate1, g1, g2, shift, scale, up, cw, cb, down, layer, gate2, g3, seq_len, tm, tf):
    b, l, d = y.shape
    wa = a.shape[-1]
    nf = D_FF // tf
    halo = SUBLANES
    hb = tm // halo
    last_hb = l // halo - 1
    n_mod = shift.shape[0]
    mod_spec = pl.BlockSpec((None, 1, d), (lambda b, i, f: (b, 0, 0)) if n_mod > 1 else (lambda b, i, f: (0, 0, 0)))
    vec = lambda off: pl.BlockSpec((None, 1, tf), lambda b, i, f: (layer, 0, off + f))
    row_d = pl.BlockSpec((1, d), lambda b, i, f: (0, 0))

    def tiles(w):
        return [pl.BlockSpec((None, tm, w), lambda b, i, f: (b, i, 0)),
                pl.BlockSpec((None, halo, w), lambda b, i, f: (b, jnp.maximum(i * hb - 1, 0), 0)),
                pl.BlockSpec((None, halo, w), lambda b, i, f: (b, jnp.minimum((i + 1) * hb, last_hb), 0))]

    kern = functools.partial(_mix_ffn_kernel, seq_len=seq_len)
    return pl.pallas_call(
        kern,
        grid=(b, l // tm, nf),
        in_specs=tiles(d) + tiles(wa) + tiles(wa) + [
                  pl.BlockSpec((2 * wa, d), lambda b, i, f: (0, 0)),
                  mod_spec, row_d,
                  row_d, mod_spec, mod_spec,
                  pl.BlockSpec((None, d, tf), lambda b, i, f: (layer, 0, f)),
                  pl.BlockSpec((None, d, tf), lambda b, i, f: (layer, 0, nf + f)),
                  pl.BlockSpec((None, 3, tf), lambda b, i, f: (layer, 0, f)),
                  pl.BlockSpec((None, 3, tf), lambda b, i, f: (layer, 0, nf + f)),
                  vec(0), vec(nf),
                  pl.BlockSpec((None, tf, d), lambda b, i, f: (layer, f, 0)),
                  mod_spec, row_d],
        out_specs=pl.BlockSpec((None, tm, d), lambda b, i, f: (b, i, 0)),
        out_shape=jax.ShapeDtypeStruct((b, l, d), F32),
        scratch_shapes=[pltpu.VMEM((tm, d), F32), pltpu.VMEM((tm + 2 * halo, d), BF16), pltpu.VMEM((tm, d), F32)],
        compiler_params=_params("arbitrary", "arbitrary", "arbitrary"),
    )(y, y, y, a, a, a, bm, bm, bm, w_out, gate1, g1, g2, shift, scale, up, up, cw, cw, cb, cb, down, gate2, g3)


def _scan_tables(c):
    nlev = int(math.log2(c))
    t = np.arange(c)[:, None]
    r = np.arange(c)[None, :]
    masks_f = [np.eye(c, dtype=bool)]
    for lev in range(1, nlev + 1):
        bsz = 2 ** lev
        mid = (t // bsz) * bsz + bsz // 2
        masks_f.append(((t // bsz) == (r // bsz)) & (t >= mid) & (r < mid))
    m_f = np.stack(masks_f).astype(np.float32)
    m_b = np.transpose(m_f, (0, 2, 1))
    tri = lambda a: jnp.asarray(np.concatenate([a, a, a], axis=1).astype(np.float32), dtype=BF16)
    two = lambda m: jnp.asarray(np.concatenate([m, m], axis=2))
    return tri(r <= t), tri(r >= t), two(m_f), two(m_b)


def _level_decay(cum, ncum, lg2, lev, bwd):
    c, w = cum.shape
    b = 1 << lev
    half = b // 2
    off = half - 1 + int(bwd)
    if b == 2:
        odd = lax.broadcasted_iota(jnp.int32, (c, 1), 0) % 2 == 1
        return jnp.where(odd != bwd, lg2, 0.0)
    if b >= 2 * SUBLANES:
        pieces = []
        for b0 in range(0, c, b):
            for rows, is_upper in ((slice(b0, b0 + half), False), (slice(b0 + half, b0 + b), True)):
                src = cum if is_upper != bwd else ncum
                pieces.append(src[rows] - jnp.broadcast_to(src[b0 + off:b0 + off + 1], (half, w)))
        return jnp.concatenate(pieces, axis=0)
    cum3 = cum.reshape(c // SUBLANES, SUBLANES, w)
    sub = lax.broadcasted_iota(jnp.int32, (1, SUBLANES, 1), 1)
    if b == SUBLANES:
        ref3 = jnp.broadcast_to(cum3[:, off:off + 1], cum3.shape)
    else:
        ref3 = jnp.where(sub < b, cum3[:, off:off + 1], cum3[:, b + off:b + off + 1])
    upper = (sub % b) >= half
    sgn = jnp.where(upper != bwd, 1.0, -1.0)
    return ((cum3 - ref3) * sgn).reshape(c, w)


def _pair_stack(xb):
    low = lax.broadcasted_iota(jnp.int32, (1, 2 * HEAD_W), 1) < HEAD_W
    zero = jnp.zeros_like(xb)
    return jnp.concatenate([jnp.where(low, xb, zero), jnp.where(low, zero, xb)], axis=0)


def _scan_group(chunks):
    pw = 2 * HEAD_W
    work = []
    for q, k, v, lg2, tri_ref, m_ref, st_ref, bwd in chunks:
        hi = lg2.astype(BF16)
        r1 = lg2 - hi.astype(F32)
        mid = r1.astype(BF16)
        lo = (r1 - mid.astype(F32)).astype(BF16)
        cum = jnp.dot(tri_ref[...], jnp.concatenate([hi, mid, lo], axis=0), preferred_element_type=F32)
        work.append(dict(q=q, k=k, v=v, lg2=lg2, cum=cum, ncum=-cum, m_ref=m_ref, st_ref=st_ref, bwd=bwd,
                         att=[None, None]))
    nlev = chunks[0][5].shape[0] - 1
    for lev in range(nlev + 1):
        for w in work:
            if lev == 0:
                qb, kb = w["q"].astype(BF16), w["k"].astype(BF16)
            else:
                e = jnp.exp2(_level_decay(w["cum"], w["ncum"], w["lg2"], lev, w["bwd"]))
                qb, kb = (w["q"] * e).astype(BF16), (w["k"] * e).astype(BF16)
            for p in range(N_HEADS // 2):
                sl = slice(p * pw, (p + 1) * pw)
                prod = lax.dot_general(qb[:, sl], _pair_stack(kb[:, sl]), NT_DIMS, preferred_element_type=F32)
                term = w["m_ref"][lev] * prod
                w["att"][p] = term if w["att"][p] is None else w["att"][p] + term
    results = []
    for w in work:
        q, k, v, cum, st_ref = w["q"], w["k"], w["v"], w["cum"], w["st_ref"]
        c = q.shape[0]
        last = 0 if w["bwd"] else c - 1
        e_cum = jnp.exp2(cum)
        d_last = e_cum[last:last + 1, :]
        qe = (q * e_cum).astype(BF16)
        kd = (k * jnp.exp2(cum[last:last + 1, :] - cum)).astype(BF16)
        vb = v.astype(BF16)
        outs = []
        for p in range(N_HEADS // 2):
            sl = slice(p * pw, (p + 1) * pw)
            st = st_ref[p]
            stb = st.astype(BF16)
            o_p = jnp.dot(w["att"][p].astype(BF16), _pair_stack(vb[:, sl]), preferred_element_type=F32)
            inter = [lax.dot_general(qe[:, p * pw + j * HEAD_W:p * pw + (j + 1) * HEAD_W],
                                     stb[:, j * HEAD_W:(j + 1) * HEAD_W], NT_DIMS, preferred_element_type=F32)
                     for j in range(2)]
            outs.append(o_p + jnp.concatenate(inter, axis=1))
            vstack = jnp.concatenate([v[:, p * pw:p * pw + HEAD_W], v[:, p * pw + HEAD_W:(p + 1) * pw]], axis=0)
            st_ref[p] = st * d_last[:, sl] + jnp.dot(vstack.T.astype(BF16), _pair_stack(kd[:, sl]),
                                                      preferred_element_type=F32)
        results.append(jnp.concatenate(outs, axis=1))
    return results


def _bidir_scan_body(load_fwd, load_bwd, g_ref, s0_ref, norm_ref, af_ref, ab_ref, mf_ref, mb_ref,
                     o_ref, s_out_ref, of_scr, ob_scr, stf_scr, stb_scr, seq_len):
    c = SCAN_CHUNK
    j = pl.program_id(1)
    nblk = pl.num_programs(1)
    n = SCAN_ROWS // c

    @pl.when(j == 0)
    def _():
        for p in range(N_HEADS // 2):
            if s0_ref is None:
                stf_scr[p] = jnp.zeros((HEAD_W, 2 * HEAD_W), F32)
                stb_scr[p] = jnp.zeros((HEAD_W, 2 * HEAD_W), F32)
            else:
                stf_scr[p] = jnp.concatenate([s0_ref[0, 2 * p].T, s0_ref[0, 2 * p + 1].T], axis=1)
                stb_scr[p] = jnp.concatenate([s0_ref[1, 2 * p].T, s0_ref[1, 2 * p + 1].T], axis=1)

    base_f = j * SCAN_ROWS
    base_b = (nblk - 1 - j) * SCAN_ROWS

    def step(i, carry):
        chunks, stores = [], []
        for u in range(SCAN_GROUP):
            rf = pl.multiple_of((i * SCAN_GROUP + u) * c, c)
            chunks.append(load_fwd(rf) + (af_ref, mf_ref, stf_scr, False))
            stores.append((of_scr, pl.multiple_of(base_f + rf, c)))
        for u in range(SCAN_GROUP):
            rb = pl.multiple_of((n - 1 - i * SCAN_GROUP - u) * c, c)
            chunks.append(load_bwd(rb) + (ab_ref, mb_ref, stb_scr, True))
            stores.append((ob_scr, pl.multiple_of(base_b + rb, c)))
        for (scr, r0), o in zip(stores, _scan_group(chunks)):
            scr[pl.ds(r0, c), :] = o
        return carry

    lax.fori_loop(0, n // SCAN_GROUP, step, 0)

    @pl.when(j == nblk - 1)
    def _():
        def fin(jj, carry):
            r0 = pl.multiple_of(jj * SCAN_ROWS, SCAN_ROWS)
            o = of_scr[pl.ds(r0, SCAN_ROWS), :] + ob_scr[pl.ds(r0, SCAN_ROWS), :]
            parts = []
            for h in range(N_HEADS):
                sl = slice(h * HEAD_W, (h + 1) * HEAD_W)
                parts.append(_rms(o[:, sl], norm_ref[:, sl]))
            o_ref[pl.ds(r0, SCAN_ROWS), :] = jnp.concatenate(parts, axis=1) * _silu(g_ref[pl.ds(r0, SCAN_ROWS), :])
            return carry

        lax.fori_loop(0, seq_len // SCAN_ROWS, fin, 0)
        for h in range(N_HEADS):
            p, hs = h // 2, slice((h % 2) * HEAD_W, (h % 2 + 1) * HEAD_W)
            dk = s_out_ref.shape[2]
            s_out_ref[0, h] = stf_scr[p][:, hs].T[:dk]
            s_out_ref[1, h] = stb_scr[p][:, hs].T[:dk]


def _hgrn_kernel(*refs, layer, has_s0, seq_len):
    qf_ref, ff_ref, if_ref, qb_ref, fb_ref, ib_ref, g_ref, lbraw_ref, norm_ref, af_ref, ab_ref, mf_ref, mb_ref = refs[:13]
    s0_ref = refs[13] if has_s0 else None
    o_ref, s_out_ref, of_scr, ob_scr, stf_scr, stb_scr = refs[13 + has_s0:]
    c = SCAN_CHUNK
    raw = lbraw_ref[...]
    ex = jnp.exp(raw - jnp.max(raw, axis=0, keepdims=True))
    sm = ex / jnp.sum(ex, axis=0, keepdims=True)
    lb = sm[0]
    for j in range(1, layer + 1):
        lb = lb + sm[j]

    def load(q_ref, f_ref, i_ref, lb_row):
        def fn(r0):
            q = _silu(q_ref[pl.ds(r0, c), :]) * (HEAD_W ** -0.5)
            sig = 1.0 / (1.0 + jnp.exp(-f_ref[pl.ds(r0, c), :]))
            f = lb_row + (1.0 - lb_row) * sig
            return q, 1.0 - f, i_ref[pl.ds(r0, c), :], jnp.log2(f)
        return fn

    _bidir_scan_body(load(qf_ref, ff_ref, if_ref, lb[0:1]), load(qb_ref, fb_ref, ib_ref, lb[1:2]), g_ref,
                     s0_ref, norm_ref, af_ref, ab_ref, mf_ref, mb_ref,
                     o_ref, s_out_ref, of_scr, ob_scr, stf_scr, stb_scr, seq_len)


def _gla_kernel(*refs, has_s0, seq_len):
    fwd_refs, bwd_refs = refs[0:4], refs[4:8]
    g_ref, aw_ref, ab_ref, norm_ref, af_ref, abk_ref, mf_ref, mb_ref = refs[8:16]
    s0_ref = refs[16] if has_s0 else None
    o_ref, s_out_ref, of_scr, ob_scr, stf_scr, stb_scr = refs[16 + has_s0:]
    c = SCAN_CHUNK

    def load(d, q_ref, k_ref, v_ref, da_ref):
        def fn(r0):
            q = q_ref[pl.ds(r0, c), :] * (DK_D ** -0.5)
            da = da_ref[pl.ds(r0, c), :]
            da_hi = da.astype(BF16)
            da_mid = (da - da_hi.astype(F32)).astype(BF16)
            xa = jnp.dot(jnp.concatenate([da_hi, da_hi, da_mid], axis=1), aw_ref[d],
                         preferred_element_type=F32) + ab_ref[d]
            la = (jnp.minimum(xa, 0.0) - jnp.log(1.0 + jnp.exp(-jnp.abs(xa)))) * (LOG2E / GLA_TAU)
            return q, k_ref[pl.ds(r0, c), :], v_ref[pl.ds(r0, c), :], la
        return fn

    _bidir_scan_body(load(0, *fwd_refs), load(1, *bwd_refs), g_ref,
                     s0_ref, norm_ref, af_ref, abk_ref, mf_ref, mb_ref,
                     o_ref, s_out_ref, of_scr, ob_scr, stf_scr, stb_scr, seq_len)


def _scan_call(kern, proj, streams, extra, s0, seq_len, dk_out):
    b = proj.shape[0]
    nblk = seq_len // SCAN_ROWS
    tabs = _scan_tables(SCAN_CHUNK)
    full = lambda a: pl.BlockSpec(a.shape, lambda i, j, _n=a.ndim: (0,) * _n)
    in_specs = []
    for cb, w, kind in streams:
        if kind == "f":
            in_specs.append(pl.BlockSpec((None, SCAN_ROWS, w), lambda i, j, _c=cb: (i, j, _c)))
        elif kind == "b":
            in_specs.append(pl.BlockSpec((None, SCAN_ROWS, w), lambda i, j, _c=cb: (i, nblk - 1 - j, _c)))
        else:
            in_specs.append(pl.BlockSpec((None, seq_len, w), lambda i, j, _c=cb: (i, 0, _c)))
    args = [proj] * len(streams)
    for a in tuple(extra) + tabs:
        in_specs.append(full(a))
        args.append(a)
    if s0 is not None:
        in_specs.append(pl.BlockSpec((None, 2, N_HEADS, HEAD_W, HEAD_W), lambda i, j: (i, 0, 0, 0, 0)))
        args.append(s0)
    return pl.pallas_call(
        kern,
        grid=(b, nblk),
        in_specs=in_specs,
        out_specs=[pl.BlockSpec((None, seq_len, MIX_W), lambda i, j: (i, 0, 0)),
                   pl.BlockSpec((None, None, 2, N_HEADS, dk_out, HEAD_W), lambda i, j: (i, 0, 0, 0, 0, 0))],
        out_shape=[jax.ShapeDtypeStruct((b, seq_len, MIX_W), F32),
                   jax.ShapeDtypeStruct((b, 1, 2, N_HEADS, dk_out, HEAD_W), F32)],
        scratch_shapes=[pltpu.VMEM((seq_len, MIX_W), F32), pltpu.VMEM((seq_len, MIX_W), F32),
                        pltpu.VMEM((N_HEADS // 2, HEAD_W, 2 * HEAD_W), F32),
                        pltpu.VMEM((N_HEADS // 2, HEAD_W, 2 * HEAD_W), F32)],
        compiler_params=_params("arbitrary", "arbitrary"),
    )(*args)


def _hgrn_mixer(proj, hgrn_lb, norm, s0, layer):
    seq_len = proj.shape[1]
    kern = functools.partial(_hgrn_kernel, layer=layer, has_s0=s0 is not None, seq_len=seq_len)
    streams = [(0, MIX_W, "f"), (1, MIX_W, "f"), (3, MIX_W, "f"),
               (0, MIX_W, "b"), (2, MIX_W, "b"), (3, MIX_W, "b"), (4, MIX_W, "w")]
    return _scan_call(kern, proj, streams, (hgrn_lb, norm.reshape(1, MIX_W)), s0, seq_len, HEAD_W)


def _gla_mixer(proj, aw_pad, ab_pad, norm, s0):
    seq_len = proj.shape[1]
    kern = functools.partial(_gla_kernel, has_s0=s0 is not None, seq_len=seq_len)
    da = 7 * MIX_W // LANES
    streams = [(3, MIX_W, "f"), (4, MIX_W, "f"), (5, MIX_W, "f"), (da, LANES, "f"),
               (3, MIX_W, "b"), (4, MIX_W, "b"), (5, MIX_W, "b"), (da, LANES, "b"), (6, MIX_W, "w")]
    return _scan_call(kern, proj, streams, (aw_pad, ab_pad, norm.reshape(1, MIX_W)), s0, seq_len, DK_D)


def _dwconv3_rows(x, w_ref, b_ref):
    l = x.shape[0]
    row = lax.broadcasted_iota(jnp.int32, (l, 1), 0)
    xp = _patch_rows(pltpu.roll(x, 1, 0), row != 0, [0])
    xn = _patch_rows(pltpu.roll(x, l - 1, 0), row != l - 1, [l - SUBLANES])
    return xp * w_ref[0:1, :] + x * w_ref[1:2, :] + xn * w_ref[2:3, :] + b_ref[...]


def _hyena_filter_kernel(z_ref, w1_ref, b1_ref, w2_ref, b2_ref, w3_ref, fr_ref, dl_ref, o_ref):
    z = z_ref[...]
    fr = fr_ref[...]
    h = jnp.sin(fr * (jnp.dot(z, w1_ref[...], preferred_element_type=F32, precision=HIGHEST) + b1_ref[...]))
    h = jnp.sin(fr * (jnp.dot(h, w2_ref[...], preferred_element_type=F32, precision=HIGHEST) + b2_ref[...]))
    h_hi = h.astype(BF16)
    h_mid = (h - h_hi.astype(F32)).astype(BF16)
    h = jnp.dot(jnp.concatenate([h_hi, h_hi, h_mid], axis=1), w3_ref[...],
                preferred_element_type=F32)
    win = jnp.exp(-z[:, 0:1] * dl_ref[...])
    o_ref[...] = h * jnp.concatenate([win] * (2 * HY_ORDER), axis=1)


def _hyena_pos_features(l):
    t = jnp.linspace(0.0, 1.0, l, dtype=F32)[:, None]
    w = 2.0 * math.pi * jnp.arange(l, dtype=F32)[:, None] / l
    fb = jnp.linspace(1e-4, HY_BANDS - 1, HY_BANDS, dtype=F32)[None]
    z = jnp.concatenate([t, jnp.cos(fb * w), -jnp.sin(fb * w)], axis=-1)
    return jnp.pad(z, ((0, 0), (0, HY_FF - HY_EMB)))


def _hyena_filters(lens, w1, b1, w2, b2, w3, freq):
    z = jnp.concatenate([_hyena_pos_features(l) for l in lens], axis=0)
    rows = z.shape[0]
    tr = 256
    w1p = jnp.pad(w1, ((0, HY_FF - HY_EMB), (0, 0)))
    max_decay = math.log(HY_TARGET) / HY_FAST
    min_decay = math.log(HY_TARGET) / HY_SLOW
    deltas = jnp.abs(jnp.linspace(min_decay, max_decay, W_B, dtype=F32))[None]
    nout = w3.shape[1]
    w3_hi = w3.astype(BF16)
    w3_mid = (w3 - w3_hi.astype(F32)).astype(BF16)
    w3s = jnp.concatenate([w3_hi, w3_mid, w3_hi], axis=0)
    full = lambda a: pl.BlockSpec(a.shape, lambda i, _n=a.ndim: (0,) * _n)
    ins = (w1p, b1.reshape(1, -1), w2, b2.reshape(1, -1), w3s, freq.reshape(1, -1), deltas)
    return pl.pallas_call(
        _hyena_filter_kernel,
        grid=(rows // tr,),
        in_specs=[pl.BlockSpec((tr, HY_FF), lambda i: (i, 0))] + [full(a) for a in ins],
        out_specs=pl.BlockSpec((tr, nout), lambda i: (i, 0)),
        out_shape=jax.ShapeDtypeStruct((rows, nout), F32),
        compiler_params=_params("arbitrary"),
    )(z, *ins)


def _dft_table_kernel(ca_ref, sa_ref, cb_ref, sb_ref, cos_ref, sinf_ref, sini_ref):
    tk = ca_ref.shape[0]
    ca, sa = ca_ref[...], sa_ref[...]
    row = pl.program_id(0) * tk + lax.broadcasted_iota(jnp.int32, (tk, 1), 0)
    lane = lax.broadcasted_iota(jnp.int32, (1, LANES), 1)
    alt_row = jnp.where(row % 2 == 0, 1.0, -1.0)
    for grp in range(cb_ref.shape[1]):
        cols = slice(grp * LANES, (grp + 1) * LANES)
        cbg, sbg = cb_ref[:, grp:grp + 1], sb_ref[:, grp:grp + 1]
        sin_t = sa * cbg + ca * sbg
        col = grp * LANES + lane
        cos_ref[:, cols] = (ca * cbg - sa * sbg).astype(BF16)
        sinf_ref[:, cols] = jnp.where(row == 0, jnp.where(col % 2 == 0, 1.0, -1.0), sin_t).astype(BF16)
        sini_ref[:, cols] = jnp.where(col == 0, alt_row, sin_t).astype(BF16)


def _dft_tables(l):
    n = 2 * l
    k = jnp.arange(l, dtype=jnp.int32)[:, None]
    t1 = jnp.arange(LANES, dtype=jnp.int32)[None, :]
    t2 = (jnp.arange(l // LANES, dtype=jnp.int32) * LANES)[None, :]
    ang = lambda m: (m % n).astype(F32) * (2.0 * math.pi / n)
    small = (jnp.cos(ang(k * t1)), jnp.sin(ang(k * t1)), jnp.cos(ang(k * t2)), jnp.sin(ang(k * t2)))
    tk = min(l, 256)
    out = jax.ShapeDtypeStruct((l, l), BF16)
    ospec = pl.BlockSpec((tk, l), lambda i: (i, 0))
    return pl.pallas_call(
        _dft_table_kernel,
        grid=(l // tk,),
        in_specs=[pl.BlockSpec((tk, a.shape[1]), lambda i: (i, 0)) for a in small],
        out_specs=[ospec, ospec, ospec],
        out_shape=[out, out, out],
        compiler_params=_params("arbitrary"),
    )(*small)


def _hyena_spectrum_kernel(c_ref, s_ref, f0_ref, f1_ref, kr_ref, kia_ref, krb_ref, *, seq_len):
    kt = pl.program_id(1)
    tk = c_ref.shape[0]
    row = lax.broadcasted_iota(jnp.int32, (seq_len, 1), 0)
    f0 = f0_ref[...].astype(BF16)
    f1 = jnp.where(row != 0, f1_ref[...], 0.0).astype(BF16)
    c = c_ref[...]
    s = s_ref[...]
    p0 = jnp.dot(c, f0, preferred_element_type=F32)
    q0 = jnp.dot(s, f0, preferred_element_type=F32)
    p1 = jnp.dot(c, f1, preferred_element_type=F32)
    q1 = jnp.dot(s, f1, preferred_element_type=F32)
    krow = kt * tk + lax.broadcasted_iota(jnp.int32, (tk, 1), 0)
    dc = krow == 0
    wk = jnp.where(dc, 1.0, 2.0) * (1.0 / (2 * seq_len))
    kr = p0 + p1
    kr_ref[...] = kr * wk
    kia_ref[...] = jnp.where(dc, 0.0, q1 - q0) * wk
    krb_ref[...] = jnp.where(dc, q0 + q1, kr) * wk


def _hyena_spectrum(filt, row0, seq_len, tables, tk):
    cos_t, sin_f, _ = tables
    rb = row0 // seq_len
    out = jax.ShapeDtypeStruct((HY_ORDER, seq_len, W_B), F32)
    kern = functools.partial(_hyena_spectrum_kernel, seq_len=seq_len)
    ospec = pl.BlockSpec((None, tk, W_B), lambda o, kt: (o, kt, 0))
    return pl.pallas_call(
        kern,
        grid=(HY_ORDER, seq_len // tk),
        in_specs=[pl.BlockSpec((tk, seq_len), lambda o, kt: (kt, 0)),
                  pl.BlockSpec((tk, seq_len), lambda o, kt: (kt, 0)),
                  pl.BlockSpec((seq_len, W_B), lambda o, kt: (rb, 2 * o)),
                  pl.BlockSpec((seq_len, W_B), lambda o, kt: (rb, 2 * o + 1))],
        out_specs=[ospec, ospec, ospec],
        out_shape=[out, out, out],
        compiler_params=_params("arbitrary", "arbitrary"),
    )(cos_t, sin_f, filt, filt)


def _hyena_order_kernel(zin_ref, gate_ref, cwz_ref, cbz_ref, cwg_ref, cbg_ref, d_ref, kr_ref, kia_ref, krb_ref,
                        cf_ref, sf_ref, ci_ref, si_ref, o_ref, z_scr, zb_scr, acc_scr, *, conv_input):
    kt = pl.program_id(1)

    @pl.when(kt == 0)
    def _():
        z = zin_ref[...]
        if conv_input:
            z = _dwconv3_rows(z, cwz_ref, cbz_ref)
        z_scr[...] = z
        zb_scr[...] = z.astype(BF16)
        acc_scr[...] = jnp.zeros_like(acc_scr)

    zb = zb_scr[...]
    p = jnp.dot(cf_ref[...], zb, preferred_element_type=F32)
    q = jnp.dot(sf_ref[...], zb, preferred_element_type=F32)
    kia = kia_ref[...]
    yr = (p * kr_ref[...] + q * kia).astype(BF16)
    yi = (q * krb_ref[...] - p * kia).astype(BF16)
    acc_scr[...] += (jnp.dot(ci_ref[...], yr, preferred_element_type=F32)
                     + jnp.dot(si_ref[...], yi, preferred_element_type=F32))

    @pl.when(kt == pl.num_programs(1) - 1)
    def _():
        gate = _dwconv3_rows(gate_ref[...], cwg_ref, cbg_ref)
        o_ref[...] = gate * (acc_scr[...] + z_scr[...] * d_ref[...])


def _hyena_order(zin, zin_col, proj, order, conv_w, conv_b, hy_d, spectrum, tables, tk):
    b, seq_len = proj.shape[0], proj.shape[1]
    cos_t, sin_f, sin_i = tables
    kr, kia, krb = spectrum
    hy0 = 5
    conv_input = order == 0
    cw = conv_w.reshape(3, 1 + HY_ORDER, W_B).transpose(1, 0, 2)
    cbias = conv_b.reshape(1 + HY_ORDER, 1, W_B)
    kern = functools.partial(_hyena_order_kernel, conv_input=conv_input)
    kspec = pl.BlockSpec((None, tk, W_B), lambda i, kt: (order, kt, 0))
    return pl.pallas_call(
        kern,
        grid=(b, seq_len // tk),
        in_specs=[pl.BlockSpec((None, seq_len, W_B), lambda i, kt: (i, 0, zin_col)),
                  pl.BlockSpec((None, seq_len, W_B), lambda i, kt: (i, 0, hy0 + 1 + order)),
                  pl.BlockSpec((None, 3, W_B), lambda i, kt: (0, 0, 0)),
                  pl.BlockSpec((None, 1, W_B), lambda i, kt: (0, 0, 0)),
                  pl.BlockSpec((None, 3, W_B), lambda i, kt: (1 + order, 0, 0)),
                  pl.BlockSpec((None, 1, W_B), lambda i, kt: (1 + order, 0, 0)),
                  pl.BlockSpec((None, 1, W_B), lambda i, kt: (order, 0, 0)),
                  kspec, kspec, kspec,
                  pl.BlockSpec((tk, seq_len), lambda i, kt: (kt, 0)),
                  pl.BlockSpec((tk, seq_len), lambda i, kt: (kt, 0)),
                  pl.BlockSpec((seq_len, tk), lambda i, kt: (0, kt)),
                  pl.BlockSpec((seq_len, tk), lambda i, kt: (0, kt))],
        out_specs=pl.BlockSpec((None, seq_len, W_B), lambda i, kt: (i, 0, 0)),
        out_shape=jax.ShapeDtypeStruct((b, seq_len, W_B), F32),
        scratch_shapes=[pltpu.VMEM((seq_len, W_B), F32), pltpu.VMEM((seq_len, W_B), BF16),
                        pltpu.VMEM((seq_len, W_B), F32)],
        compiler_params=_params("arbitrary", "arbitrary"),
    )(zin, proj, cw, cbias, cw, cbias, hy_d.reshape(HY_ORDER, 1, W_B), kr, kia, krb, cos_t, sin_f, cos_t, sin_i)


def _hyena_short_kernel(v_ref, x1_ref, x2_ref, cw_ref, cb_ref, d_ref, kr_ref, kia_ref, krb_ref,
                        cf_ref, sf_ref, si_ref, o_ref):
    cf, sf, si = cf_ref[...], sf_ref[...], si_ref[...]
    nb = v_ref.shape[0]
    zs = [_dwconv3_rows(v_ref[i], cw_ref.at[0], cb_ref.at[0]) for i in range(nb)]
    for order, gate_ref in enumerate((x1_ref, x2_ref)):
        zb = [z.astype(BF16) for z in zs]
        ps = [jnp.dot(cf, z, preferred_element_type=F32) for z in zb]
        qs = [jnp.dot(sf, z, preferred_element_type=F32) for z in zb]
        kia = kia_ref[order]
        yr = [(p * kr_ref[order] + q * kia).astype(BF16) for p, q in zip(ps, qs)]
        yi = [(q * krb_ref[order] - p * kia).astype(BF16) for p, q in zip(ps, qs)]
        conv = [jnp.dot(cf, r, preferred_element_type=F32) + jnp.dot(si, m, preferred_element_type=F32)
                for r, m in zip(yr, yi)]
        gates = [_dwconv3_rows(gate_ref[i], cw_ref.at[1 + order], cb_ref.at[1 + order]) for i in range(nb)]
        zs = [g * (c + z * d_ref[order]) for g, c, z in zip(gates, conv, zs)]
    for i in range(nb):
        o_ref[i] = zs[i]


def _hyena_short(proj, conv_w, conv_b, hy_d, spectrum, tables):
    b, seq_len = proj.shape[0], proj.shape[1]
    cos_t, sin_f, sin_i = tables
    nb = 2 if b % 2 == 0 else 1
    hy0 = 5
    cw = conv_w.reshape(3, 1 + HY_ORDER, W_B).transpose(1, 0, 2)
    cbias = conv_b.reshape(1 + HY_ORDER, 1, W_B)
    full = lambda a: pl.BlockSpec(a.shape, lambda i, _n=a.ndim: (0,) * _n)
    col = lambda j: pl.BlockSpec((nb, seq_len, W_B), lambda i: (i, 0, hy0 + j))
    consts = (cw, cbias, hy_d.reshape(HY_ORDER, 1, W_B)) + tuple(spectrum) + (cos_t, sin_f, sin_i)
    return pl.pallas_call(
        _hyena_short_kernel,
        grid=(b // nb,),
        in_specs=[col(0), col(1), col(2)] + [full(a) for a in consts],
        out_specs=pl.BlockSpec((nb, seq_len, W_B), lambda i: (i, 0, 0)),
        out_shape=jax.ShapeDtypeStruct((b, seq_len, W_B), F32),
        compiler_params=_params("arbitrary"),
    )(proj, proj, proj, *consts)


def _hyena_mixer(proj, conv_w, conv_b, hy_d, spectrum, tables, tk):
    if tk == proj.shape[1]:
        return _hyena_short(proj, conv_w, conv_b, hy_d, spectrum, tables)
    z = _hyena_order(proj, 5, proj, 0, conv_w, conv_b, hy_d, spectrum, tables, tk)
    return _hyena_order(z, 0, proj, 1, conv_w, conv_b, hy_d, spectrum, tables, tk)


def _diff_lambda(lp_ref, lam_init):
    lp = lp_ref[...]
    a = jnp.sum(lp[0:1] * lp[1:2], axis=-1, keepdims=True)
    b = jnp.sum(lp[2:3] * lp[3:4], axis=-1, keepdims=True)
    return jnp.exp(a) - jnp.exp(b) + lam_init


def _diff_attend(q, keys_b, vals_b, lam):
    lane = lax.broadcasted_iota(jnp.int32, (1, 2 * DH_C), 1)
    qs = q * (DH_C ** -0.5 * LOG2E)

    def attend(sel):
        s = lax.dot_general(jnp.where(sel, qs, 0.0).astype(BF16), keys_b, NT_DIMS, preferred_element_type=F32)
        e = jnp.exp2(s - jnp.max(s, axis=-1, keepdims=True))
        den = jnp.sum(e, axis=-1, keepdims=True)
        return jnp.dot(e.astype(BF16), vals_b, preferred_element_type=F32) / den

    return attend(lane < DH_C) - lam * attend(lane >= DH_C)


def _attn_prompt_kernel(q_ref, k_ref, v_ref, lp_ref, norm_ref, o_ref, kc_ref, vc_ref, *, lam_init):
    hw = 2 * DH_C
    lam = _diff_lambda(lp_ref, lam_init)
    for h in range(N_HEADS):
        sl = slice(h * hw, (h + 1) * hw)
        k = k_ref[:, sl]
        v = v_ref[:, sl]
        o = _diff_attend(q_ref[:, sl], k.astype(BF16), v.astype(BF16), lam)
        o_ref[:, sl] = _rms(o, norm_ref[:, sl]) * (1.0 - lam_init)
        kc_ref[h] = k
        vc_ref[h] = v


def _attn_prompt(proj, diff_lambda, diff_norm, lam_init):
    b, seq_len = proj.shape[0], proj.shape[1]
    hw = 2 * DH_C
    w = N_HEADS * hw
    kern = functools.partial(_attn_prompt_kernel, lam_init=lam_init)
    col = lambda j: pl.BlockSpec((None, seq_len, w), lambda i: (i, 0, j))
    cache_spec = pl.BlockSpec((None, None, N_HEADS, seq_len, hw), lambda i: (i, 0, 0, 0, 0))
    cache_shape = jax.ShapeDtypeStruct((b, 1, N_HEADS, seq_len, hw), F32)
    return pl.pallas_call(
        kern,
        grid=(b,),
        in_specs=[col(0), col(1), col(2),
                  pl.BlockSpec((4, DH_C), lambda i: (0, 0)),
                  pl.BlockSpec((1, w), lambda i: (0, 0))],
        out_specs=[pl.BlockSpec((None, seq_len, w), lambda i: (i, 0, 0)), cache_spec, cache_spec],
        out_shape=[jax.ShapeDtypeStruct((b, seq_len, w), F32), cache_shape, cache_shape],
        compiler_params=_params("arbitrary"),
    )(proj, proj, proj, diff_lambda, diff_norm.reshape(1, -1))


def _rope(x, cos, sin_signed):
    lane = lax.broadcasted_iota(jnp.int32, (1, x.shape[-1]), 1)
    first = (lane % 32) < 16
    partner = jnp.where(first, pltpu.roll(x, x.shape[-1] - 16, 1), pltpu.roll(x, 16, 1))
    return x * cos + partner * sin_signed


def _attn_sample_kernel(q_ref, k_ref, v_ref, ck_ref, cv_ref, cosq_ref, sinq_ref, cosk_ref, sink_ref,
                        lp_ref, norm_ref, o_ref, keys_scr, vals_scr, *, lam_init):
    past = ck_ref.shape[1]
    hw = 2 * DH_C

    @pl.when(pl.program_id(1) == 0)
    def _():
        for h in range(N_HEADS):
            sl = slice(h * hw, (h + 1) * hw)
            keys_scr[h, 0:past, :] = ck_ref[h].astype(BF16)
            vals_scr[h, 0:past, :] = cv_ref[h].astype(BF16)
            keys_scr[h, past:, :] = _rope(k_ref[:, sl], cosk_ref[...], sink_ref[...]).astype(BF16)
            vals_scr[h, past:, :] = v_ref[:, sl].astype(BF16)

    lam = _diff_lambda(lp_ref, lam_init)
    for h in range(N_HEADS):
        sl = slice(h * hw, (h + 1) * hw)
        q = _rope(q_ref[:, sl], cosq_ref[...], sinq_ref[...])
        o = _diff_attend(q, keys_scr[h], vals_scr[h], lam)
        o_ref[:, sl] = _rms(o, norm_ref[:, sl]) * (1.0 - lam_init)


def _rope_tables(seq_len):
    pos = jnp.arange(seq_len)
    row = (pos // GRID_W).astype(F32)
    col = (pos % GRID_W).astype(F32)
    half = DH_C // 2
    inv = ROPE_BASE ** (-jnp.arange(0, half, 2, dtype=F32) / half)
    def comp(p):
        ang = p[:, None] * inv[None]
        c, s = jnp.cos(ang), jnp.sin(ang)
        return jnp.concatenate([c, c], axis=-1), jnp.concatenate([-s, s], axis=-1)
    cr, sr = comp(row)
    cc, sc = comp(col)
    cos = jnp.concatenate([cr, cc, cr, cc], axis=-1)
    sin = jnp.concatenate([sr, sc, sr, sc], axis=-1)
    return cos, sin


def _attn_sample(proj, ctx_k, ctx_v, diff_lambda, diff_norm, lam_init, tq):
    b, seq_len = proj.shape[0], proj.shape[1]
    past = ctx_k.shape[2]
    hw = 2 * DH_C
    w = N_HEADS * hw
    cos, sin = _rope_tables(seq_len)
    kern = functools.partial(_attn_sample_kernel, lam_init=lam_init)
    ctx_spec = pl.BlockSpec((None, N_HEADS, past, hw), lambda i, j: (i, 0, 0, 0))
    return pl.pallas_call(
        kern,
        grid=(b, seq_len // tq),
        in_specs=[pl.BlockSpec((None, tq, w), lambda i, j: (i, j, 0)),
                  pl.BlockSpec((None, seq_len, w), lambda i, j: (i, 0, 1)),
                  pl.BlockSpec((None, seq_len, w), lambda i, j: (i, 0, 2)),
                  ctx_spec, ctx_spec,
                  pl.BlockSpec((tq, hw), lambda i, j: (j, 0)),
                  pl.BlockSpec((tq, hw), lambda i, j: (j, 0)),
                  pl.BlockSpec((seq_len, hw), lambda i, j: (0, 0)),
                  pl.BlockSpec((seq_len, hw), lambda i, j: (0, 0)),
                  pl.BlockSpec((4, DH_C), lambda i, j: (0, 0)),
                  pl.BlockSpec((1, w), lambda i, j: (0, 0))],
        out_specs=pl.BlockSpec((None, tq, w), lambda i, j: (i, j, 0)),
        out_shape=jax.ShapeDtypeStruct((b, seq_len, w), F32),
        scratch_shapes=[pltpu.VMEM((N_HEADS, past + seq_len, hw), BF16),
                        pltpu.VMEM((N_HEADS, past + seq_len, hw), BF16)],
        compiler_params=_params("arbitrary", "arbitrary"),
    )(proj, proj, proj, ctx_k, ctx_v, cos, sin, cos, sin, diff_lambda, diff_norm.reshape(1, -1))


def _pad_heads(w, axis=-1):
    w = jnp.moveaxis(w, axis, -1)
    lead = w.shape[:-1]
    w = w.reshape(*lead, N_HEADS, DK_D)
    w = jnp.pad(w, [(0, 0)] * len(lead) + [(0, 0), (0, HEAD_W - DK_D)])
    return jnp.moveaxis(w.reshape(*lead, N_HEADS * HEAD_W), -1, axis)


def _odd_w_in(w):
    cq, ck, cv, dq, dk, dv, dg, da = jnp.split(w, [512, 1024, 1536, 1792, 2048, 2560, 3072], axis=1)
    da = jnp.pad(da, ((0, 0), (0, LANES - 2 * GLA_RANK)))
    return jnp.concatenate([cq, ck, cv, _pad_heads(dq), _pad_heads(dk), dv, dg, da], axis=1)


def kernel(x_prompt, x_sample, state_hgrn, cache_diff_k, cache_diff_v, state_gla, c, c_ctx, ada_w, ada_b, norm_g, ffn_up, ffn_conv_w, ffn_conv_b, ffn_down, w_in_even, w_out_even, hgrn_lb, hgrn_norm, hy_conv_w, hy_conv_b, hy_w1, hy_b1, hy_w2, hy_b2, hy_w3, hy_freq, hy_d, w_in_odd, w_out_odd, diff_lambda, diff_norm, gla_aw, gla_ab, gla_norm):
    bp, lp, d = x_prompt.shape
    bs, ls, _ = x_sample.shape

    cvec_t = jnp.zeros((d, SUBLANES), F32).at[:, 0].set(c_ctx).at[:, 1:1 + bs].set(c.T)
    mod = _ada_mod(cvec_t, 1 + bs, ada_w, ada_b)

    yp = x_prompt.reshape(1, bp * lp, d)
    ys = x_sample
    tm = 512

    filt = _hyena_filters((ls, lp), hy_w1[0], hy_b1[0], hy_w2[0], hy_b2[0], hy_w3[0], hy_freq[0])
    tab_p, tab_s = _dft_tables(lp), _dft_tables(ls)
    spec_s = _hyena_spectrum(filt, 0, ls, tab_s, 512)
    spec_p = _hyena_spectrum(filt, ls, lp, tab_p, lp)

    ffn_up_b, ffn_down_b = ffn_up.astype(BF16), ffn_down.astype(BF16)
    ffn_cb = ffn_conv_b.reshape(DEPTH, 1, -1)
    outs = {}
    for l in range(DEPTH):
        m = mod[l].reshape(SUBLANES, 6, 1, d)
        mp = [m[0:1, j] for j in range(6)]
        ms = [m[1:1 + bs, j] for j in range(6)]
        g = [norm_g[l, j].reshape(1, d) for j in range(4)]
        if l % 2 == 0:
            e = l // 2
            w_in = w_in_even[e].astype(BF16)
            w_out = w_out_even[e].astype(BF16)
            pp = _normmod_matmul(yp, g[0], mp[0], mp[1], w_in, tm).reshape(bp, lp, -1)
            ps = _normmod_matmul(ys, g[0], ms[0], ms[1], w_in, tm)
            oa_p, st_p = _hgrn_mixer(pp, hgrn_lb, hgrn_norm[e], None, l)
            oa_s, _ = _hgrn_mixer(ps, hgrn_lb, hgrn_norm[e], state_hgrn[:, e], l)
            ob_p = _hyena_mixer(pp, hy_conv_w[e], hy_conv_b[e], hy_d[e], spec_p, tab_p, lp)
            ob_s = _hyena_mixer(ps, hy_conv_w[e], hy_conv_b[e], hy_d[e], spec_s, tab_s, 256)
            outs["hgrn"] = st_p
        else:
            o = l // 2
            lam_init = 0.8 - 0.6 * math.exp(-0.3 * l)
            w_in = _odd_w_in(w_in_odd[o].astype(BF16))
            w_out = w_out_odd[o].astype(BF16)
            aw = jnp.zeros((2, LANES, MIX_W), F32)
            aw = aw.at[0, 0:GLA_RANK].set(_pad_heads(gla_aw[o, 0])).at[1, GLA_RANK:2 * GLA_RANK].set(_pad_heads(gla_aw[o, 1]))
            aw_hi = aw.astype(BF16)
            aw_mid = (aw - aw_hi.astype(F32)).astype(BF16)
            aw = jnp.concatenate([aw_hi, aw_mid, aw_hi], axis=1)
            ab = _pad_heads(gla_ab[o]).reshape(2, 1, MIX_W)
            s0 = jnp.pad(state_gla[:, o], ((0, 0), (0, 0), (0, 0), (0, HEAD_W - DK_D), (0, 0)))
            pp = _normmod_matmul(yp, g[0], mp[0], mp[1], w_in, tm).reshape(bp, lp, -1)
            ps = _normmod_matmul(ys, g[0], ms[0], ms[1], w_in, tm)
            oa_p, kc, vc = _attn_prompt(pp, diff_lambda[o], diff_norm[o], lam_init)
            oa_s = _attn_sample(ps, cache_diff_k[:, o], cache_diff_v[:, o], diff_lambda[o], diff_norm[o], lam_init, 256)
            ob_p, st_p = _gla_mixer(pp, aw, ab, gla_norm[o], None)
            ob_s, _ = _gla_mixer(ps, aw, ab, gla_norm[o], s0)
            outs["k"], outs["v"], outs["gla"] = kc, vc, st_p
        ffn_args = (ffn_up_b, ffn_conv_w, ffn_cb, ffn_down_b, l)
        yp = _mix_ffn(yp, oa_p.reshape(1, bp * lp, -1), ob_p.reshape(1, bp * lp, -1), w_out, mp[2], g[1],
                      g[2], mp[3], mp[4], *ffn_args, mp[5], g[3], lp, tm, 1408)
        ys = _mix_ffn(ys, oa_s, ob_s, w_out, ms[2], g[1], g[2], ms[3], ms[4], *ffn_args, ms[5], g[3], ls, tm, 1408)

    return (yp.reshape(bp, lp, d), ys, outs["hgrn"], outs["k"], outs["v"], outs["gla"])
```

```python
import functools
import math

import jax
import jax.numpy as jnp
import numpy as np
from jax import lax
from jax.experimental import pallas as pl
from jax.experimental.pallas import tpu as pltpu

F32 = jnp.float32
BF16 = jnp.bfloat16
HIGHEST = lax.Precision.HIGHEST

D_MODEL = 1024
DEPTH = 2
GRID_W = 64
N_HEADS = 4
HEAD_W = 128
MIX_W = N_HEADS * HEAD_W
W_B = 512
HY_ORDER = 2
HY_EMB = 33
HY_BANDS = (HY_EMB - 1) // 2
HY_FF = 64
HY_TARGET = 1e-2
HY_FAST = 0.3
HY_SLOW = 1.5
DH_C = 64
DK_D = 64
GLA_RANK = 16
HEADS_PER_GROUP = 2
GLA_TAU = 16.0
ROPE_BASE = 10000.0
D_FF = 2816
EPS = 1e-6

LANES = 128
SUBLANES = 8
VMEM_LIMIT = 56 * 1024 * 1024
SCAN_CHUNK = 64
SCAN_ROWS = 256
SCAN_GROUP = 2
NT_DIMS = (((1,), (1,)), ((), ()))
LOG2E = 1.4426950408889634


def _params(*sem):
    return pltpu.CompilerParams(dimension_semantics=sem, vmem_limit_bytes=VMEM_LIMIT)


def _silu(x):
    return x * (1.0 / (1.0 + jnp.exp(-x)))


def _rms(x, g):
    return x * lax.rsqrt(jnp.mean(x * x, axis=-1, keepdims=True) + EPS) * g


def _ada_kernel(c_ref, w_ref, b_ref, o_ref, *, n_rows):
    s = _silu(c_ref[...])
    w = w_ref[...]
    rows = [jnp.sum(w * s[:, r:r + 1], axis=0, keepdims=True) for r in range(n_rows)]
    rows.append(jnp.zeros((SUBLANES - n_rows, w.shape[1]), F32))
    o_ref[...] = jnp.concatenate(rows, axis=0) + b_ref[...]


def _ada_mod(cvec_t, n_rows, ada_w, ada_b):
    n = ada_w.shape[-1]
    tn = 1536
    return pl.pallas_call(
        functools.partial(_ada_kernel, n_rows=n_rows),
        grid=(DEPTH, n // tn),
        in_specs=[pl.BlockSpec((D_MODEL, SUBLANES), lambda l, j: (0, 0)),
                  pl.BlockSpec((None, D_MODEL, tn), lambda l, j: (l, 0, j)),
                  pl.BlockSpec((None, 1, tn), lambda l, j: (l, 0, j))],
        out_specs=pl.BlockSpec((None, SUBLANES, tn), lambda l, j: (l, 0, j)),
        out_shape=jax.ShapeDtypeStruct((DEPTH, SUBLANES, n), F32),
        compiler_params=_params("arbitrary", "arbitrary"),
    )(cvec_t, ada_w, ada_b.reshape(DEPTH, 1, n))


def _normmod_matmul_kernel(x_ref, g_ref, sh_ref, sc_ref, w_ref, o_ref):
    h = _rms(x_ref[...], g_ref[...]) * (1.0 + sc_ref[...]) + sh_ref[...]
    o_ref[...] = jnp.dot(h.astype(BF16), w_ref[...], preferred_element_type=F32)


def _mod_index(n_mod):
    return (lambda b, i: (b, 0, 0)) if n_mod > 1 else (lambda b, i: (0, 0, 0))


def _normmod_matmul(x, g, shift, scale, w, tm):
    b, l, d = x.shape
    n = w.shape[1]
    mod_spec = pl.BlockSpec((None, 1, d), _mod_index(shift.shape[0]))
    return pl.pallas_call(
        _normmod_matmul_kernel,
        grid=(b, l // tm),
        in_specs=[pl.BlockSpec((None, tm, d), lambda b, i: (b, i, 0)),
                  pl.BlockSpec((1, d), lambda b, i: (0, 0)),
                  mod_spec, mod_spec,
                  pl.BlockSpec((d, n), lambda b, i: (0, 0))],
        out_specs=pl.BlockSpec((None, tm, n), lambda b, i: (b, i, 0)),
        out_shape=jax.ShapeDtypeStruct((b, l, n), F32),
        compiler_params=_params("arbitrary", "arbitrary"),
    )(x, g, shift, scale, w)


def _patch_rows(x, keep, starts):
    pieces, r = [], 0
    for s in starts:
        if s > r:
            pieces.append(x[r:s])
        pieces.append(jnp.where(keep[s:s + SUBLANES], x[s:s + SUBLANES], 0.0))
        r = s + SUBLANES
    if r < x.shape[0]:
        pieces.append(x[r:])
    return jnp.concatenate(pieces, axis=0)


def _mix_ffn_kernel(y_ref, yp_ref, yn_ref, a_ref, ap_ref, an_ref, b_ref, bp_ref, bn_ref, wo_ref, gate1_ref, g1_ref,
                    g2_ref, sh_ref, sc_ref, ua_ref, ug_ref, cwa_ref, cwg_ref, cba_ref, cbg_ref, dn_ref, gate2_ref,
                    g3_ref, o_ref, y1_scr, h_scr, acc_scr, *, seq_len):
    i = pl.program_id(1)
    f = pl.program_id(2)
    tm = y_ref.shape[0]
    halo = yp_ref.shape[0]
    half = a_ref.shape[1]

    @pl.when(f == 0)
    def _():
        def mixed(y_r, a_r, b_r):
            m = jnp.dot(a_r[...].astype(BF16), wo_ref[:half, :], preferred_element_type=F32)
            m = m + jnp.dot(b_r[...].astype(BF16), wo_ref[half:, :], preferred_element_type=F32)
            return y_r[...] + gate1_ref[...] * _rms(m, g1_ref[...])

        def nm(x):
            return (_rms(x, g2_ref[...]) * (1.0 + sc_ref[...]) + sh_ref[...]).astype(BF16)
        y1 = mixed(y_ref, a_ref, b_ref)
        y1_scr[...] = y1
        h_scr[0:halo, :] = nm(mixed(yp_ref, ap_ref, bp_ref))
        h_scr[halo:halo + tm, :] = nm(y1)
        h_scr[halo + tm:, :] = nm(mixed(yn_ref, an_ref, bn_ref))
        acc_scr[...] = jnp.zeros_like(acc_scr)

    rows = tm + 2 * halo
    pos = (i * tm + lax.broadcasted_iota(jnp.int32, (tm, 1), 0)) % seq_len
    has_prev = pos != 0
    has_next = pos != seq_len - 1
    period = math.gcd(tm, seq_len)
    first_groups = list(range(0, tm, period))
    last_groups = [s + period - SUBLANES for s in first_groups]

    def conv(u_ref, cw_ref, cb_ref):
        u = jnp.dot(h_scr[...], u_ref[...], preferred_element_type=F32)
        up = _patch_rows(pltpu.roll(u, 1, 0)[halo:halo + tm], has_prev, first_groups)
        un = _patch_rows(pltpu.roll(u, rows - 1, 0)[halo:halo + tm], has_next, last_groups)
        uc = u[halo:halo + tm]
        return up * cw_ref[0:1, :] + uc * cw_ref[1:2, :] + un * cw_ref[2:3, :] + cb_ref[...]

    a = conv(ua_ref, cwa_ref, cba_ref)
    gt = conv(ug_ref, cwg_ref, cbg_ref)
    act = (_silu(gt) * a).astype(BF16)
    acc_scr[...] += jnp.dot(act, dn_ref[...], preferred_element_type=F32)

    @pl.when(f == pl.num_programs(2) - 1)
    def _():
        o_ref[...] = y1_scr[...] + gate2_ref[...] * _rms(acc_scr[...], g3_ref[...])


def _mix_ffn(y, a, bm, w_out, gate1, g1, g2, shift, scale, up, cw, cb, down, layer, gate2, g3, seq_len, tm, tf):
    b, l, d = y.shape
    wa = a.shape[-1]
    nf = D_FF // tf
    halo = SUBLANES
    hb = tm // halo
    last_hb = l // halo - 1
    n_mod = shift.shape[0]
    mod_spec = pl.BlockSpec((None, 1, d), (lambda b, i, f: (b, 0, 0)) if n_mod > 1 else (lambda b, i, f: (0, 0, 0)))
    vec = lambda off: pl.BlockSpec((None, 1, tf), lambda b, i, f: (layer, 0, off + f))
    row_d = pl.BlockSpec((1, d), lambda b, i, f: (0, 0))

    def tiles(w):
        return [pl.BlockSpec((None, tm, w), lambda b, i, f: (b, i, 0)),
                pl.BlockSpec((None, halo, w), lambda b, i, f: (b, jnp.maximum(i * hb - 1, 0), 0)),
                pl.BlockSpec((None, halo, w), lambda b, i, f: (b, jnp.minimum((i + 1) * hb, last_hb), 0))]

    kern = functools.partial(_mix_ffn_kernel, seq_len=seq_len)
    return pl.pallas_call(
        kern,
        grid=(b, l // tm, nf),
        in_specs=tiles(d) + tiles(wa) + tiles(wa) + [
                  pl.BlockSpec((2 * wa, d), lambda b, i, f: (0, 0)),
                  mod_spec, row_d,
                  row_d, mod_spec, mod_spec,
                  pl.BlockSpec((None, d, tf), lambda b, i, f: (layer, 0, f)),
                  pl.BlockSpec((None, d, tf), lambda b, i, f: (layer, 0, nf + f)),
                  pl.BlockSpec((None, 3, tf), lambda b, i, f: (layer, 0, f)),
                  pl.BlockSpec((None, 3, tf), lambda b, i, f: (layer, 0, nf + f)),
                  vec(0), vec(nf),
                  pl.BlockSpec((None, tf, d), lambda b, i, f: (layer, f, 0)),
                  mod_spec, row_d],
        out_specs=pl.BlockSpec((None, tm, d), lambda b, i, f: (b, i, 0)),
        out_shape=jax.ShapeDtypeStruct((b, l, d), F32),
        scratch_shapes=[pltpu.VMEM((tm, d), F32), pltpu.VMEM((tm + 2 * halo, d), BF16), pltpu.VMEM((tm, d), F32)],
        compiler_params=_params("arbitrary", "arbitrary", "arbitrary"),
    )(y, y, y, a, a, a, bm, bm, bm, w_out, gate1, g1, g2, shift, scale, up, up, cw, cw, cb, cb, down, gate2, g3)


def _scan_tables(c, heads_per_group):
    nlev = int(math.log2(c))
    t = np.arange(c)[:, None]
    r = np.arange(c)[None, :]
    masks_f = [np.eye(c, dtype=bool)]
    for lev in range(1, nlev + 1):
        bsz = 2 ** lev
        mid = (t // bsz) * bsz + bsz // 2
        masks_f.append(((t // bsz) == (r // bsz)) & (t >= mid) & (r < mid))
    m_f = np.stack(masks_f).astype(np.float32)
    m_b = np.transpose(m_f, (0, 2, 1))
    tri = lambda a: jnp.asarray(np.concatenate([a, a, a], axis=1).astype(np.float32), dtype=BF16)
    rep = lambda m: jnp.asarray(np.concatenate([m] * heads_per_group, axis=2))
    return tri(r <= t), tri(r >= t), rep(m_f), rep(m_b)


def _level_decay(cum, ncum, lg2, lev, bwd):
    c, w = cum.shape
    b = 1 << lev
    half = b // 2
    off = half - 1 + int(bwd)
    if b == 2:
        odd = lax.broadcasted_iota(jnp.int32, (c, 1), 0) % 2 == 1
        return jnp.where(odd != bwd, lg2, 0.0)
    if b >= 2 * SUBLANES:
        pieces = []
        for b0 in range(0, c, b):
            for rows, is_upper in ((slice(b0, b0 + half), False), (slice(b0 + half, b0 + b), True)):
                src = cum if is_upper != bwd else ncum
                pieces.append(src[rows] - jnp.broadcast_to(src[b0 + off:b0 + off + 1], (half, w)))
        return jnp.concatenate(pieces, axis=0)
    cum3 = cum.reshape(c // SUBLANES, SUBLANES, w)
    sub = lax.broadcasted_iota(jnp.int32, (1, SUBLANES, 1), 1)
    if b == SUBLANES:
        ref3 = jnp.broadcast_to(cum3[:, off:off + 1], cum3.shape)
    else:
        ref3 = jnp.where(sub < b, cum3[:, off:off + 1], cum3[:, b + off:b + off + 1])
    upper = (sub % b) >= half
    sgn = jnp.where(upper != bwd, 1.0, -1.0)
    return ((cum3 - ref3) * sgn).reshape(c, w)


def _head_stack(xb, width):
    n = xb.shape[1] // width
    lane = lax.broadcasted_iota(jnp.int32, (1, xb.shape[1]), 1)
    zero = jnp.zeros_like(xb)
    return jnp.concatenate([jnp.where((lane >= j * width) & (lane < (j + 1) * width), xb, zero) for j in range(n)],
                           axis=0)


def _scan_group(chunks):
    dk = chunks[0][0].shape[1] // N_HEADS
    hpg = HEADS_PER_GROUP
    gw = hpg * dk
    ngroups = N_HEADS // hpg
    vw = hpg * HEAD_W
    work = []
    for q, k, v, lg2, tri_ref, m_ref, st_ref, bwd in chunks:
        hi = lg2.astype(BF16)
        r1 = lg2 - hi.astype(F32)
        mid = r1.astype(BF16)
        lo = (r1 - mid.astype(F32)).astype(BF16)
        cum = jnp.dot(tri_ref[...], jnp.concatenate([hi, mid, lo], axis=0), preferred_element_type=F32)
        work.append(dict(q=q, k=k, v=v, lg2=lg2, cum=cum, ncum=-cum, m_ref=m_ref, st_ref=st_ref, bwd=bwd,
                         q16=q.astype(BF16), k16=k.astype(BF16), att=[None] * ngroups))
    nlev = chunks[0][5].shape[0] - 1
    for lev in range(nlev + 1):
        for w in work:
            if lev == 0:
                qb, kb = w["q16"], w["k16"]
            else:
                e = jnp.exp2(_level_decay(w["cum"], w["ncum"], w["lg2"], lev, w["bwd"])).astype(BF16)
                qb, kb = w["q16"] * e, w["k16"] * e
            for g in range(ngroups):
                sl = slice(g * gw, (g + 1) * gw)
                prod = lax.dot_general(qb[:, sl], _head_stack(kb[:, sl], dk), NT_DIMS,
                                       preferred_element_type=F32)
                term = w["m_ref"][lev] * prod
                w["att"][g] = term if w["att"][g] is None else w["att"][g] + term
    lane = lax.broadcasted_iota(jnp.int32, (1, gw), 1)
    results = []
    for w in work:
        q, k, v, cum, st_ref = w["q"], w["k"], w["v"], w["cum"], w["st_ref"]
        c = q.shape[0]
        last = 0 if w["bwd"] else c - 1
        e_cum = jnp.exp2(cum)
        d_last = e_cum[last:last + 1, :]
        qe = (q * e_cum).astype(BF16)
        kd = (k * jnp.exp2(cum[last:last + 1, :] - cum)).astype(BF16)
        vb = v.astype(BF16)
        outs = []
        for g in range(ngroups):
            sl = slice(g * gw, (g + 1) * gw)
            st = st_ref[g]
            stb = st.astype(BF16)
            o_g = jnp.dot(w["att"][g].astype(BF16), _head_stack(vb[:, g * vw:(g + 1) * vw], HEAD_W),
                          preferred_element_type=F32)
            qe_g = qe[:, sl]
            zero = jnp.zeros_like(qe_g)
            inter = [lax.dot_general(jnp.where((lane >= j * dk) & (lane < (j + 1) * dk), qe_g, zero), stb, NT_DIMS,
                                     preferred_element_type=F32) for j in range(hpg)]
            outs.append(o_g + jnp.concatenate(inter, axis=1))
            vstack = jnp.concatenate([v[:, g * vw + j * HEAD_W:g * vw + (j + 1) * HEAD_W] for j in range(hpg)],
                                     axis=0)
            st_ref[g] = st * d_last[:, sl] + jnp.dot(vstack.T.astype(BF16), _head_stack(kd[:, sl], dk),
                                                      preferred_element_type=F32)
        results.append(jnp.concatenate(outs, axis=1))
    return results


def _bidir_scan_body(load_fwd, load_bwd, g_ref, s0_ref, norm_ref, af_ref, ab_ref, mf_ref, mb_ref,
                     o_ref, s_out_ref, of_scr, ob_scr, stf_scr, stb_scr, seq_len):
    c = SCAN_CHUNK
    j = pl.program_id(1)
    nblk = pl.num_programs(1)
    n = SCAN_ROWS // c
    ngroups, _, gw = stf_scr.shape
    hpg = N_HEADS // ngroups
    dk = gw // hpg

    @pl.when(j == 0)
    def _():
        for g in range(ngroups):
            if s0_ref is None:
                stf_scr[g] = jnp.zeros((HEAD_W, gw), F32)
                stb_scr[g] = jnp.zeros((HEAD_W, gw), F32)
            else:
                stf_scr[g] = jnp.concatenate([s0_ref[0, g * hpg + j].T for j in range(hpg)], axis=1)
                stb_scr[g] = jnp.concatenate([s0_ref[1, g * hpg + j].T for j in range(hpg)], axis=1)

    base_f = j * SCAN_ROWS
    base_b = (nblk - 1 - j) * SCAN_ROWS

    def step(i, carry):
        chunks, stores = [], []
        for u in range(SCAN_GROUP):
            rf = pl.multiple_of((i * SCAN_GROUP + u) * c, c)
            chunks.append(load_fwd(rf) + (af_ref, mf_ref, stf_scr, False))
            stores.append((of_scr, pl.multiple_of(base_f + rf, c)))
        for u in range(SCAN_GROUP):
            rb = pl.multiple_of((n - 1 - i * SCAN_GROUP - u) * c, c)
            chunks.append(load_bwd(rb) + (ab_ref, mb_ref, stb_scr, True))
            stores.append((ob_scr, pl.multiple_of(base_b + rb, c)))
        for (scr, r0), o in zip(stores, _scan_group(chunks)):
            scr[pl.ds(r0, c), :] = o
        return carry

    lax.fori_loop(0, n // SCAN_GROUP, step, 0)

    @pl.when(j == nblk - 1)
    def _():
        def fin(jj, carry):
            r0 = pl.multiple_of(jj * SCAN_ROWS, SCAN_ROWS)
            o = of_scr[pl.ds(r0, SCAN_ROWS), :] + ob_scr[pl.ds(r0, SCAN_ROWS), :]
            parts = []
            for h in range(N_HEADS):
                sl = slice(h * HEAD_W, (h + 1) * HEAD_W)
                parts.append(_rms(o[:, sl], norm_ref[:, sl]))
            o_ref[pl.ds(r0, SCAN_ROWS), :] = jnp.concatenate(parts, axis=1) * _silu(g_ref[pl.ds(r0, SCAN_ROWS), :])
            return carry

        lax.fori_loop(0, seq_len // SCAN_ROWS, fin, 0)
        for h in range(N_HEADS):
            g, hs = h // hpg, slice((h % hpg) * dk, (h % hpg + 1) * dk)
            s_out_ref[0, h] = stf_scr[g][:, hs].T
            s_out_ref[1, h] = stb_scr[g][:, hs].T


def _hgrn_kernel(*refs, layer, has_s0, seq_len):
    qf_ref, ff_ref, if_ref, qb_ref, fb_ref, ib_ref, g_ref, lbraw_ref, norm_ref, af_ref, ab_ref, mf_ref, mb_ref = refs[:13]
    s0_ref = refs[13] if has_s0 else None
    o_ref, s_out_ref, of_scr, ob_scr, stf_scr, stb_scr = refs[13 + has_s0:]
    c = SCAN_CHUNK
    raw = lbraw_ref[...]
    ex = jnp.exp(raw - jnp.max(raw, axis=0, keepdims=True))
    sm = ex / jnp.sum(ex, axis=0, keepdims=True)
    lb = sm[0]
    for j in range(1, layer + 1):
        lb = lb + sm[j]

    def load(q_ref, f_ref, i_ref, lb_row):
        def fn(r0):
            q = _silu(q_ref[pl.ds(r0, c), :]) * (HEAD_W ** -0.5)
            sig = 1.0 / (1.0 + jnp.exp(-f_ref[pl.ds(r0, c), :]))
            f = lb_row + (1.0 - lb_row) * sig
            return q, 1.0 - f, i_ref[pl.ds(r0, c), :], jnp.log2(f)
        return fn

    _bidir_scan_body(load(qf_ref, ff_ref, if_ref, lb[0:1]), load(qb_ref, fb_ref, ib_ref, lb[1:2]), g_ref,
                     s0_ref, norm_ref, af_ref, ab_ref, mf_ref, mb_ref,
                     o_ref, s_out_ref, of_scr, ob_scr, stf_scr, stb_scr, seq_len)


def _gla_kernel(*refs, has_s0, seq_len):
    fwd_refs, bwd_refs = refs[0:4], refs[4:8]
    g_ref, aw_ref, ab_ref, norm_ref, af_ref, abk_ref, mf_ref, mb_ref = refs[8:16]
    s0_ref = refs[16] if has_s0 else None
    o_ref, s_out_ref, of_scr, ob_scr, stf_scr, stb_scr = refs[16 + has_s0:]
    c = SCAN_CHUNK

    def load(d, q_ref, k_ref, v_ref, da_ref):
        def fn(r0):
            q = q_ref[pl.ds(r0, c), :] * (DK_D ** -0.5)
            da = da_ref[pl.ds(r0, c), :]
            da_hi = da.astype(BF16)
            da_mid = (da - da_hi.astype(F32)).astype(BF16)
            xa = jnp.dot(jnp.concatenate([da_hi, da_hi, da_mid], axis=1), aw_ref[d],
                         preferred_element_type=F32) + ab_ref[d]
            la = (jnp.minimum(xa, 0.0) - jnp.log(1.0 + jnp.exp(-jnp.abs(xa)))) * (LOG2E / GLA_TAU)
            return q, k_ref[pl.ds(r0, c), :], v_ref[pl.ds(r0, c), :], la
        return fn

    _bidir_scan_body(load(0, *fwd_refs), load(1, *bwd_refs), g_ref,
                     s0_ref, norm_ref, af_ref, abk_ref, mf_ref, mb_ref,
                     o_ref, s_out_ref, of_scr, ob_scr, stf_scr, stb_scr, seq_len)


def _scan_call(kern, proj, streams, extra, s0, seq_len, dk):
    b = proj.shape[0]
    nblk = seq_len // SCAN_ROWS
    hpg = HEADS_PER_GROUP
    tabs = _scan_tables(SCAN_CHUNK, hpg)
    full = lambda a: pl.BlockSpec(a.shape, lambda i, j, _n=a.ndim: (0,) * _n)
    in_specs = []
    for cb, w, kind in streams:
        if kind == "f":
            in_specs.append(pl.BlockSpec((None, SCAN_ROWS, w), lambda i, j, _c=cb: (i, j, _c)))
        elif kind == "b":
            in_specs.append(pl.BlockSpec((None, SCAN_ROWS, w), lambda i, j, _c=cb: (i, nblk - 1 - j, _c)))
        else:
            in_specs.append(pl.BlockSpec((None, seq_len, w), lambda i, j, _c=cb: (i, 0, _c)))
    args = [proj] * len(streams)
    for a in tuple(extra) + tabs:
        in_specs.append(full(a))
        args.append(a)
    if s0 is not None:
        in_specs.append(pl.BlockSpec((None, 2, N_HEADS, dk, HEAD_W), lambda i, j: (i, 0, 0, 0, 0)))
        args.append(s0)
    return pl.pallas_call(
        kern,
        grid=(b, nblk),
        in_specs=in_specs,
        out_specs=[pl.BlockSpec((None, seq_len, MIX_W), lambda i, j: (i, 0, 0)),
                   pl.BlockSpec((None, None, 2, N_HEADS, dk, HEAD_W), lambda i, j: (i, 0, 0, 0, 0, 0))],
        out_shape=[jax.ShapeDtypeStruct((b, seq_len, MIX_W), F32),
                   jax.ShapeDtypeStruct((b, 1, 2, N_HEADS, dk, HEAD_W), F32)],
        scratch_shapes=[pltpu.VMEM((seq_len, MIX_W), F32), pltpu.VMEM((seq_len, MIX_W), F32),
                        pltpu.VMEM((N_HEADS // hpg, HEAD_W, hpg * dk), F32),
                        pltpu.VMEM((N_HEADS // hpg, HEAD_W, hpg * dk), F32)],
        compiler_params=_params("arbitrary", "arbitrary"),
    )(*args)


def _hgrn_mixer(proj, hgrn_lb, norm, s0, layer):
    seq_len = proj.shape[1]
    kern = functools.partial(_hgrn_kernel, layer=layer, has_s0=s0 is not None, seq_len=seq_len)
    streams = [(0, MIX_W, "f"), (1, MIX_W, "f"), (3, MIX_W, "f"),
               (0, MIX_W, "b"), (2, MIX_W, "b"), (3, MIX_W, "b"), (4, MIX_W, "w")]
    return _scan_call(kern, proj, streams, (hgrn_lb, norm.reshape(1, MIX_W)), s0, seq_len, HEAD_W)


def _gla_mixer(proj, aw, ab, norm, s0):
    seq_len = proj.shape[1]
    kern = functools.partial(_gla_kernel, has_s0=s0 is not None, seq_len=seq_len)
    kw = N_HEADS * DK_D
    streams = [(1536 // kw, kw, "f"), (1792 // kw, kw, "f"), (2048 // MIX_W, MIX_W, "f"), (3072 // LANES, LANES, "f"),
               (1536 // kw, kw, "b"), (1792 // kw, kw, "b"), (2048 // MIX_W, MIX_W, "b"), (3072 // LANES, LANES, "b"),
               (2560 // MIX_W, MIX_W, "w")]
    return _scan_call(kern, proj, streams, (aw, ab, norm.reshape(1, MIX_W)), s0, seq_len, DK_D)


def _dwconv3_rows(x, w_ref, b_ref):
    l = x.shape[0]
    row = lax.broadcasted_iota(jnp.int32, (l, 1), 0)
    xp = _patch_rows(pltpu.roll(x, 1, 0), row != 0, [0])
    xn = _patch_rows(pltpu.roll(x, l - 1, 0), row != l - 1, [l - SUBLANES])
    return xp * w_ref[0:1, :] + x * w_ref[1:2, :] + xn * w_ref[2:3, :] + b_ref[...]


def _hyena_filter_kernel(z_ref, w1_ref, b1_ref, w2_ref, b2_ref, w3_ref, fr_ref, dl_ref, o_ref):
    z = z_ref[...]
    fr = fr_ref[...]
    h = jnp.sin(fr * (jnp.dot(z, w1_ref[...], preferred_element_type=F32, precision=HIGHEST) + b1_ref[...]))
    h = jnp.sin(fr * (jnp.dot(h, w2_ref[...], preferred_element_type=F32, precision=HIGHEST) + b2_ref[...]))
    h_hi = h.astype(BF16)
    h_mid = (h - h_hi.astype(F32)).astype(BF16)
    h = jnp.dot(jnp.concatenate([h_hi, h_hi, h_mid], axis=1), w3_ref[...],
                preferred_element_type=F32)
    win = jnp.exp(-z[:, 0:1] * dl_ref[...])
    o_ref[...] = h * jnp.concatenate([win] * (2 * HY_ORDER), axis=1)


def _hyena_pos_features(l):
    t = jnp.linspace(0.0, 1.0, l, dtype=F32)[:, None]
    w = 2.0 * math.pi * jnp.arange(l, dtype=F32)[:, None] / l
    fb = jnp.linspace(1e-4, HY_BANDS - 1, HY_BANDS, dtype=F32)[None]
    z = jnp.concatenate([t, jnp.cos(fb * w), -jnp.sin(fb * w)], axis=-1)
    return jnp.pad(z, ((0, 0), (0, HY_FF - HY_EMB)))


def _hyena_filters(lens, w1, b1, w2, b2, w3, freq):
    z = jnp.concatenate([_hyena_pos_features(l) for l in lens], axis=0)
    rows = z.shape[0]
    tr = 256
    w1p = jnp.pad(w1, ((0, HY_FF - HY_EMB), (0, 0)))
    max_decay = math.log(HY_TARGET) / HY_FAST
    min_decay = math.log(HY_TARGET) / HY_SLOW
    deltas = jnp.abs(jnp.linspace(min_decay, max_decay, W_B, dtype=F32))[None]
    nout = w3.shape[1]
    w3_hi = w3.astype(BF16)
    w3_mid = (w3 - w3_hi.astype(F32)).astype(BF16)
    w3s = jnp.concatenate([w3_hi, w3_mid, w3_hi], axis=0)
    full = lambda a: pl.BlockSpec(a.shape, lambda i, _n=a.ndim: (0,) * _n)
    ins = (w1p, b1.reshape(1, -1), w2, b2.reshape(1, -1), w3s, freq.reshape(1, -1), deltas)
    return pl.pallas_call(
        _hyena_filter_kernel,
        grid=(rows // tr,),
        in_specs=[pl.BlockSpec((tr, HY_FF), lambda i: (i, 0))] + [full(a) for a in ins],
        out_specs=pl.BlockSpec((tr, nout), lambda i: (i, 0)),
        out_shape=jax.ShapeDtypeStruct((rows, nout), F32),
        compiler_params=_params("arbitrary"),
    )(z, *ins)


def _dft_table_kernel(ca_ref, sa_ref, cb_ref, sb_ref, cos_ref, sinf_ref, sini_ref):
    tk = ca_ref.shape[0]
    ca, sa = ca_ref[...], sa_ref[...]
    row = pl.program_id(0) * tk + lax.broadcasted_iota(jnp.int32, (tk, 1), 0)
    lane = lax.broadcasted_iota(jnp.int32, (1, LANES), 1)
    alt_row = jnp.where(row % 2 == 0, 1.0, -1.0)
    for grp in range(cb_ref.shape[1]):
        cols = slice(grp * LANES, (grp + 1) * LANES)
        cbg, sbg = cb_ref[:, grp:grp + 1], sb_ref[:, grp:grp + 1]
        sin_t = sa * cbg + ca * sbg
        col = grp * LANES + lane
        cos_ref[:, cols] = (ca * cbg - sa * sbg).astype(BF16)
        sinf_ref[:, cols] = jnp.where(row == 0, jnp.where(col % 2 == 0, 1.0, -1.0), sin_t).astype(BF16)
        sini_ref[:, cols] = jnp.where(col == 0, alt_row, sin_t).astype(BF16)


def _dft_tables(l):
    n = 2 * l
    k = jnp.arange(l, dtype=jnp.int32)[:, None]
    t1 = jnp.arange(LANES, dtype=jnp.int32)[None, :]
    t2 = (jnp.arange(l // LANES, dtype=jnp.int32) * LANES)[None, :]
    ang = lambda m: (m % n).astype(F32) * (2.0 * math.pi / n)
    small = (jnp.cos(ang(k * t1)), jnp.sin(ang(k * t1)), jnp.cos(ang(k * t2)), jnp.sin(ang(k * t2)))
    tk = min(l, 256)
    out = jax.ShapeDtypeStruct((l, l), BF16)
    ospec = pl.BlockSpec((tk, l), lambda i: (i, 0))
    return pl.pallas_call(
        _dft_table_kernel,
        grid=(l // tk,),
        in_specs=[pl.BlockSpec((tk, a.shape[1]), lambda i: (i, 0)) for a in small],
        out_specs=[ospec, ospec, ospec],
        out_shape=[out, out, out],
        compiler_params=_params("arbitrary"),
    )(*small)


def _hyena_spectrum_kernel(c_ref, s_ref, f0_ref, f1_ref, kr_ref, kia_ref, krb_ref, *, seq_len):
    kt = pl.program_id(1)
    tk = c_ref.shape[0]
    row = lax.broadcasted_iota(jnp.int32, (seq_len, 1), 0)
    f0 = f0_ref[...].astype(BF16)
    f1 = jnp.where(row != 0, f1_ref[...], 0.0).astype(BF16)
    c = c_ref[...]
    s = s_ref[...]
    p0 = jnp.dot(c, f0, preferred_element_type=F32)
    q0 = jnp.dot(s, f0, preferred_element_type=F32)
    p1 = jnp.dot(c, f1, preferred_element_type=F32)
    q1 = jnp.dot(s, f1, preferred_element_type=F32)
    krow = kt * tk + lax.broadcasted_iota(jnp.int32, (tk, 1), 0)
    dc = krow == 0
    wk = jnp.where(dc, 1.0, 2.0) * (1.0 / (2 * seq_len))
    kr = p0 + p1
    kr_ref[...] = kr * wk
    kia_ref[...] = jnp.where(dc, 0.0, q1 - q0) * wk
    krb_ref[...] = jnp.where(dc, q0 + q1, kr) * wk


def _hyena_spectrum(filt, row0, seq_len, tables, tk):
    cos_t, sin_f, _ = tables
    rb = row0 // seq_len
    out = jax.ShapeDtypeStruct((HY_ORDER, seq_len, W_B), F32)
    kern = functools.partial(_hyena_spectrum_kernel, seq_len=seq_len)
    ospec = pl.BlockSpec((None, tk, W_B), lambda o, kt: (o, kt, 0))
    return pl.pallas_call(
        kern,
        grid=(HY_ORDER, seq_len // tk),
        in_specs=[pl.BlockSpec((tk, seq_len), lambda o, kt: (kt, 0)),
                  pl.BlockSpec((tk, seq_len), lambda o, kt: (kt, 0)),
                  pl.BlockSpec((seq_len, W_B), lambda o, kt: (rb, 2 * o)),
                  pl.BlockSpec((seq_len, W_B), lambda o, kt: (rb, 2 * o + 1))],
        out_specs=[ospec, ospec, ospec],
        out_shape=[out, out, out],
        compiler_params=_params("arbitrary", "arbitrary"),
    )(cos_t, sin_f, filt, filt)


def _hyena_order_kernel(zin_ref, gate_ref, cwz_ref, cbz_ref, cwg_ref, cbg_ref, d_ref, kr_ref, kia_ref, krb_ref,
                        cf_ref, sf_ref, ci_ref, si_ref, o_ref, z_scr, zb_scr, acc_scr, *, conv_input):
    kt = pl.program_id(1)

    @pl.when(kt == 0)
    def _():
        z = zin_ref[...]
        if conv_input:
            z = _dwconv3_rows(z, cwz_ref, cbz_ref)
        z_scr[...] = z
        zb_scr[...] = z.astype(BF16)
        acc_scr[...] = jnp.zeros_like(acc_scr)

    zb = zb_scr[...]
    p = jnp.dot(cf_ref[...], zb, preferred_element_type=F32)
    q = jnp.dot(sf_ref[...], zb, preferred_element_type=F32)
    kia = kia_ref[...]
    yr = (p * kr_ref[...] + q * kia).astype(BF16)
    yi = (q * krb_ref[...] - p * kia).astype(BF16)
    acc_scr[...] += (jnp.dot(ci_ref[...], yr, preferred_element_type=F32)
                     + jnp.dot(si_ref[...], yi, preferred_element_type=F32))

    @pl.when(kt == pl.num_programs(1) - 1)
    def _():
        gate = _dwconv3_rows(gate_ref[...], cwg_ref, cbg_ref)
        o_ref[...] = gate * (acc_scr[...] + z_scr[...] * d_ref[...])


def _hyena_order(zin, zin_col, proj, order, conv_w, conv_b, hy_d, spectrum, tables, tk):
    b, seq_len = proj.shape[0], proj.shape[1]
    cos_t, sin_f, sin_i = tables
    kr, kia, krb = spectrum
    hy0 = 5
    conv_input = order == 0
    cw = conv_w.reshape(3, 1 + HY_ORDER, W_B).transpose(1, 0, 2)
    cbias = conv_b.reshape(1 + HY_ORDER, 1, W_B)
    kern = functools.partial(_hyena_order_kernel, conv_input=conv_input)
    kspec = pl.BlockSpec((None, tk, W_B), lambda i, kt: (order, kt, 0))
    return pl.pallas_call(
        kern,
        grid=(b, seq_len // tk),
        in_specs=[pl.BlockSpec((None, seq_len, W_B), lambda i, kt: (i, 0, zin_col)),
                  pl.BlockSpec((None, seq_len, W_B), lambda i, kt: (i, 0, hy0 + 1 + order)),
                  pl.BlockSpec((None, 3, W_B), lambda i, kt: (0, 0, 0)),
                  pl.BlockSpec((None, 1, W_B), lambda i, kt: (0, 0, 0)),
                  pl.BlockSpec((None, 3, W_B), lambda i, kt: (1 + order, 0, 0)),
                  pl.BlockSpec((None, 1, W_B), lambda i, kt: (1 + order, 0, 0)),
                  pl.BlockSpec((None, 1, W_B), lambda i, kt: (order, 0, 0)),
                  kspec, kspec, kspec,
                  pl.BlockSpec((tk, seq_len), lambda i, kt: (kt, 0)),
                  pl.BlockSpec((tk, seq_len), lambda i, kt: (kt, 0)),
                  pl.BlockSpec((seq_len, tk), lambda i, kt: (0, kt)),
                  pl.BlockSpec((seq_len, tk), lambda i, kt: (0, kt))],
        out_specs=pl.BlockSpec((None, seq_len, W_B), lambda i, kt: (i, 0, 0)),
        out_shape=jax.ShapeDtypeStruct((b, seq_len, W_B), F32),
        scratch_shapes=[pltpu.VMEM((seq_len, W_B), F32), pltpu.VMEM((seq_len, W_B), BF16),
                        pltpu.VMEM((seq_len, W_B), F32)],
        compiler_params=_params("arbitrary", "arbitrary"),
    )(zin, proj, cw, cbias, cw, cbias, hy_d.reshape(HY_ORDER, 1, W_B), kr, kia, krb, cos_t, sin_f, cos_t, sin_i)


def _hyena_short_kernel(v_ref, x1_ref, x2_ref, cw_ref, cb_ref, d_ref, kr_ref, kia_ref, krb_ref,
                        cf_ref, sf_ref, si_ref, o_ref):
    cf, sf, si = cf_ref[...], sf_ref[...], si_ref[...]
    nb = v_ref.shape[0]
    zs = [_dwconv3_rows(v_ref[i], cw_ref.at[0], cb_ref.at[0]) for i in range(nb)]
    for order, gate_ref in enumerate((x1_ref, x2_ref)):
        zb = [z.astype(BF16) for z in zs]
        ps = [jnp.dot(cf, z, preferred_element_type=F32) for z in zb]
        qs = [jnp.dot(sf, z, preferred_element_type=F32) for z in zb]
        kia = kia_ref[order]
        yr = [(p * kr_ref[order] + q * kia).astype(BF16) for p, q in zip(ps, qs)]
        yi = [(q * krb_ref[order] - p * kia).astype(BF16) for p, q in zip(ps, qs)]
        conv = [jnp.dot(cf, r, preferred_element_type=F32) + jnp.dot(si, m, preferred_element_type=F32)
                for r, m in zip(yr, yi)]
        gates = [_dwconv3_rows(gate_ref[i], cw_ref.at[1 + order], cb_ref.at[1 + order]) for i in range(nb)]
        zs = [g * (c + z * d_ref[order]) for g, c, z in zip(gates, conv, zs)]
    for i in range(nb):
        o_ref[i] = zs[i]


def _hyena_short(proj, conv_w, conv_b, hy_d, spectrum, tables):
    b, seq_len = proj.shape[0], proj.shape[1]
    cos_t, sin_f, sin_i = tables
    nb = 2 if b % 2 == 0 else 1
    hy0 = 5
    cw = conv_w.reshape(3, 1 + HY_ORDER, W_B).transpose(1, 0, 2)
    cbias = conv_b.reshape(1 + HY_ORDER, 1, W_B)
    full = lambda a: pl.BlockSpec(a.shape, lambda i, _n=a.ndim: (0,) * _n)
    col = lambda j: pl.BlockSpec((nb, seq_len, W_B), lambda i: (i, 0, hy0 + j))
    consts = (cw, cbias, hy_d.reshape(HY_ORDER, 1, W_B)) + tuple(spectrum) + (cos_t, sin_f, sin_i)
    return pl.pallas_call(
        _hyena_short_kernel,
        grid=(b // nb,),
        in_specs=[col(0), col(1), col(2)] + [full(a) for a in consts],
        out_specs=pl.BlockSpec((nb, seq_len, W_B), lambda i: (i, 0, 0)),
        out_shape=jax.ShapeDtypeStruct((b, seq_len, W_B), F32),
        compiler_params=_params("arbitrary"),
    )(proj, proj, proj, *consts)


def _hyena_mixer(proj, conv_w, conv_b, hy_d, spectrum, tables, tk):
    if tk == proj.shape[1]:
        return _hyena_short(proj, conv_w, conv_b, hy_d, spectrum, tables)
    z = _hyena_order(proj, 5, proj, 0, conv_w, conv_b, hy_d, spectrum, tables, tk)
    return _hyena_order(z, 0, proj, 1, conv_w, conv_b, hy_d, spectrum, tables, tk)


def _diff_lambda(lp_ref, lam_init):
    lp = lp_ref[...]
    a = jnp.sum(lp[0:1] * lp[1:2], axis=-1, keepdims=True)
    b = jnp.sum(lp[2:3] * lp[3:4], axis=-1, keepdims=True)
    return jnp.exp(a) - jnp.exp(b) + lam_init


def _diff_attend(q, keys_b, vals_b, lam):
    lane = lax.broadcasted_iota(jnp.int32, (1, 2 * DH_C), 1)
    qs = q * (DH_C ** -0.5 * LOG2E)

    def attend(sel):
        s = lax.dot_general(jnp.where(sel, qs, 0.0).astype(BF16), keys_b, NT_DIMS, preferred_element_type=F32)
        e = jnp.exp2(s - jnp.max(s, axis=-1, keepdims=True))
        den = jnp.sum(e, axis=-1, keepdims=True)
        return jnp.dot(e.astype(BF16), vals_b, preferred_element_type=F32) / den

    return attend(lane < DH_C) - lam * attend(lane >= DH_C)


def _attn_prompt_kernel(q_ref, k_ref, v_ref, lp_ref, norm_ref, o_ref, kc_ref, vc_ref, *, lam_init):
    hw = 2 * DH_C
    lam = _diff_lambda(lp_ref, lam_init)
    for h in range(N_HEADS):
        sl = slice(h * hw, (h + 1) * hw)
        k = k_ref[:, sl]
        v = v_ref[:, sl]
        o = _diff_attend(q_ref[:, sl], k.astype(BF16), v.astype(BF16), lam)
        o_ref[:, sl] = _rms(o, norm_ref[:, sl]) * (1.0 - lam_init)
        kc_ref[h] = k
        vc_ref[h] = v


def _attn_prompt(proj, diff_lambda, diff_norm, lam_init):
    b, seq_len = proj.shape[0], proj.shape[1]
    hw = 2 * DH_C
    w = N_HEADS * hw
    kern = functools.partial(_attn_prompt_kernel, lam_init=lam_init)
    col = lambda j: pl.BlockSpec((None, seq_len, w), lambda i: (i, 0, j))
    cache_spec = pl.BlockSpec((None, None, N_HEADS, seq_len, hw), lambda i: (i, 0, 0, 0, 0))
    cache_shape = jax.ShapeDtypeStruct((b, 1, N_HEADS, seq_len, hw), F32)
    return pl.pallas_call(
        kern,
        grid=(b,),
        in_specs=[col(0), col(1), col(2),
                  pl.BlockSpec((4, DH_C), lambda i: (0, 0)),
                  pl.BlockSpec((1, w), lambda i: (0, 0))],
        out_specs=[pl.BlockSpec((None, seq_len, w), lambda i: (i, 0, 0)), cache_spec, cache_spec],
        out_shape=[jax.ShapeDtypeStruct((b, seq_len, w), F32), cache_shape, cache_shape],
        compiler_params=_params("arbitrary"),
    )(proj, proj, proj, diff_lambda, diff_norm.reshape(1, -1))


def _rope(x, cos, sin_signed):
    lane = lax.broadcasted_iota(jnp.int32, (1, x.shape[-1]), 1)
    first = (lane % 32) < 16
    partner = jnp.where(first, pltpu.roll(x, x.shape[-1] - 16, 1), pltpu.roll(x, 16, 1))
    return x * cos + partner * sin_signed


def _attn_sample_kernel(q_ref, k_ref, v_ref, ck_ref, cv_ref, cosq_ref, sinq_ref, cosk_ref, sink_ref,
                        lp_ref, norm_ref, o_ref, keys_scr, vals_scr, *, lam_init):
    past = ck_ref.shape[1]
    hw = 2 * DH_C

    @pl.when(pl.program_id(1) == 0)
    def _():
        for h in range(N_HEADS):
            sl = slice(h * hw, (h + 1) * hw)
            keys_scr[h, 0:past, :] = ck_ref[h].astype(BF16)
            vals_scr[h, 0:past, :] = cv_ref[h].astype(BF16)
            keys_scr[h, past:, :] = _rope(k_ref[:, sl], cosk_ref[...], sink_ref[...]).astype(BF16)
            vals_scr[h, past:, :] = v_ref[:, sl].astype(BF16)

    lam = _diff_lambda(lp_ref, lam_init)
    for h in range(N_HEADS):
        sl = slice(h * hw, (h + 1) * hw)
        q = _rope(q_ref[:, sl], cosq_ref[...], sinq_ref[...])
        o = _diff_attend(q, keys_scr[h], vals_scr[h], lam)
        o_ref[:, sl] = _rms(o, norm_ref[:, sl]) * (1.0 - lam_init)


def _rope_tables(seq_len):
    pos = jnp.arange(seq_len)
    row = (pos // GRID_W).astype(F32)
    col = (pos % GRID_W).astype(F32)
    half = DH_C // 2
    inv = ROPE_BASE ** (-jnp.arange(0, half, 2, dtype=F32) / half)
    def comp(p):
        ang = p[:, None] * inv[None]
        c, s = jnp.cos(ang), jnp.sin(ang)
        return jnp.concatenate([c, c], axis=-1), jnp.concatenate([-s, s], axis=-1)
    cr, sr = comp(row)
    cc, sc = comp(col)
    cos = jnp.concatenate([cr, cc, cr, cc], axis=-1)
    sin = jnp.concatenate([sr, sc, sr, sc], axis=-1)
    return cos, sin


def _attn_sample(proj, ctx_k, ctx_v, diff_lambda, diff_norm, lam_init, tq):
    b, seq_len = proj.shape[0], proj.shape[1]
    past = ctx_k.shape[2]
    hw = 2 * DH_C
    w = N_HEADS * hw
    cos, sin = _rope_tables(seq_len)
    kern = functools.partial(_attn_sample_kernel, lam_init=lam_init)
    ctx_spec = pl.BlockSpec((None, N_HEADS, past, hw), lambda i, j: (i, 0, 0, 0))
    return pl.pallas_call(
        kern,
        grid=(b, seq_len // tq),
        in_specs=[pl.BlockSpec((None, tq, w), lambda i, j: (i, j, 0)),
                  pl.BlockSpec((None, seq_len, w), lambda i, j: (i, 0, 1)),
                  pl.BlockSpec((None, seq_len, w), lambda i, j: (i, 0, 2)),
                  ctx_spec, ctx_spec,
                  pl.BlockSpec((tq, hw), lambda i, j: (j, 0)),
                  pl.BlockSpec((tq, hw), lambda i, j: (j, 0)),
                  pl.BlockSpec((seq_len, hw), lambda i, j: (0, 0)),
                  pl.BlockSpec((seq_len, hw), lambda i, j: (0, 0)),
                  pl.BlockSpec((4, DH_C), lambda i, j: (0, 0)),
                  pl.BlockSpec((1, w), lambda i, j: (0, 0))],
        out_specs=pl.BlockSpec((None, tq, w), lambda i, j: (i, j, 0)),
        out_shape=jax.ShapeDtypeStruct((b, seq_len, w), F32),
        scratch_shapes=[pltpu.VMEM((N_HEADS, past + seq_len, hw), BF16),
                        pltpu.VMEM((N_HEADS, past + seq_len, hw), BF16)],
        compiler_params=_params("arbitrary", "arbitrary"),
    )(proj, proj, proj, ctx_k, ctx_v, cos, sin, cos, sin, diff_lambda, diff_norm.reshape(1, -1))


def kernel(x_prompt, x_sample, state_hgrn, cache_diff_k, cache_diff_v, state_gla, c, c_ctx, ada_w, ada_b, norm_g, ffn_up, ffn_conv_w, ffn_conv_b, ffn_down, w_in_even, w_out_even, hgrn_lb, hgrn_norm, hy_conv_w, hy_conv_b, hy_w1, hy_b1, hy_w2, hy_b2, hy_w3, hy_freq, hy_d, w_in_odd, w_out_odd, diff_lambda, diff_norm, gla_aw, gla_ab, gla_norm):
    bp, lp, d = x_prompt.shape
    bs, ls, _ = x_sample.shape

    cvec_t = jnp.zeros((d, SUBLANES), F32).at[:, 0].set(c_ctx).at[:, 1:1 + bs].set(c.T)
    mod = _ada_mod(cvec_t, 1 + bs, ada_w, ada_b)

    yp = x_prompt.reshape(1, bp * lp, d)
    ys = x_sample
    tm = 512

    filt = _hyena_filters((ls, lp), hy_w1[0], hy_b1[0], hy_w2[0], hy_b2[0], hy_w3[0], hy_freq[0])
    tab_p, tab_s = _dft_tables(lp), _dft_tables(ls)
    spec_s = _hyena_spectrum(filt, 0, ls, tab_s, 512)
    spec_p = _hyena_spectrum(filt, ls, lp, tab_p, lp)

    ffn_up_b, ffn_down_b = ffn_up.astype(BF16), ffn_down.astype(BF16)
    ffn_cb = ffn_conv_b.reshape(DEPTH, 1, -1)
    outs = {}
    for l in range(DEPTH):
        m = mod[l].reshape(SUBLANES, 6, 1, d)
        mp = [m[0:1, j] for j in range(6)]
        ms = [m[1:1 + bs, j] for j in range(6)]
        g = [norm_g[l, j].reshape(1, d) for j in range(4)]
        if l % 2 == 0:
            e = l // 2
            w_in = w_in_even[e].astype(BF16)
            w_out = w_out_even[e].astype(BF16)
            pp = _normmod_matmul(yp, g[0], mp[0], mp[1], w_in, tm).reshape(bp, lp, -1)
            ps = _normmod_matmul(ys, g[0], ms[0], ms[1], w_in, tm)
            oa_p, st_p = _hgrn_mixer(pp, hgrn_lb, hgrn_norm[e], None, l)
            oa_s, _ = _hgrn_mixer(ps, hgrn_lb, hgrn_norm[e], state_hgrn[:, e], l)
            ob_p = _hyena_mixer(pp, hy_conv_w[e], hy_conv_b[e], hy_d[e], spec_p, tab_p, lp)
            ob_s = _hyena_mixer(ps, hy_conv_w[e], hy_conv_b[e], hy_d[e], spec_s, tab_s, 256)
            outs["hgrn"] = st_p
        else:
            o = l // 2
            lam_init = 0.8 - 0.6 * math.exp(-0.3 * l)
            pad_cols = -w_in_odd.shape[-1] % LANES
            w_in = jnp.pad(w_in_odd[o].astype(BF16), ((0, 0), (0, pad_cols)))
            w_out = w_out_odd[o].astype(BF16)
            kw = N_HEADS * DK_D
            aw = jnp.zeros((2, LANES, kw), F32)
            aw = aw.at[0, 0:GLA_RANK].set(gla_aw[o, 0]).at[1, GLA_RANK:2 * GLA_RANK].set(gla_aw[o, 1])
            aw_hi = aw.astype(BF16)
            aw_mid = (aw - aw_hi.astype(F32)).astype(BF16)
            aw = jnp.concatenate([aw_hi, aw_mid, aw_hi], axis=1)
            ab = gla_ab[o].reshape(2, 1, kw)
            s0 = state_gla[:, o]
            pp = _normmod_matmul(yp, g[0], mp[0], mp[1], w_in, tm).reshape(bp, lp, -1)
            ps = _normmod_matmul(ys, g[0], ms[0], ms[1], w_in, tm)
            oa_p, kc, vc = _attn_prompt(pp, diff_lambda[o], diff_norm[o], lam_init)
            oa_s = _attn_sample(ps, cache_diff_k[:, o], cache_diff_v[:, o], diff_lambda[o], diff_norm[o], lam_init, 256)
            ob_p, st_p = _gla_mixer(pp, aw, ab, gla_norm[o], None)
            ob_s, _ = _gla_mixer(ps, aw, ab, gla_norm[o], s0)
            outs["k"], outs["v"], outs["gla"] = kc, vc, st_p
        ffn_args = (ffn_up_b, ffn_conv_w, ffn_cb, ffn_down_b, l)
        yp = _mix_ffn(yp, oa_p.reshape(1, bp * lp, -1), ob_p.reshape(1, bp * lp, -1), w_out, mp[2], g[1],
                      g[2], mp[3], mp[4], *ffn_args, mp[5], g[3], lp, tm, 1408)
        ys = _mix_ffn(ys, oa_s, ob_s, w_out, ms[2], g[1], g[2], ms[3], ms[4], *ffn_args, ms[5], g[3], ls, tm, 1408)

    return (yp.reshape(bp, lp, d), ys, outs["hgrn"], outs["k"], outs["v"], outs["gla"])
```

```python
import functools
import math

import jax
import jax.numpy as jnp
import numpy as np
from jax import lax
from jax.experimental import pallas as pl
from jax.experimental.pallas import tpu as pltpu

F32 = jnp.float32
BF16 = jnp.bfloat16
HIGHEST = lax.Precision.HIGHEST

D_MODEL = 1024
DEPTH = 2
GRID_W = 64
N_HEADS = 4
HEAD_W = 128
MIX_W = N_HEADS * HEAD_W
W_B = 512
HY_ORDER = 2
HY_EMB = 33
HY_BANDS = (HY_EMB - 1) // 2
HY_FF = 64
HY_TARGET = 1e-2
HY_FAST = 0.3
HY_SLOW = 1.5
DH_C = 64
DK_D = 64
GLA_RANK = 16
HEADS_PER_GROUP = 2
GLA_TAU = 16.0
ROPE_BASE = 10000.0
D_FF = 2816
EPS = 1e-6

LANES = 128
SUBLANES = 8
VMEM_LIMIT = 56 * 1024 * 1024
SCAN_CHUNK = 64
SCAN_ROWS = 256
SCAN_GROUP = 2
NT_DIMS = (((1,), (1,)), ((), ()))
LOG2E = 1.4426950408889634


def _params(*sem):
    return pltpu.CompilerParams(dimension_semantics=sem, vmem_limit_bytes=VMEM_LIMIT)


def _silu(x):
    return x * (1.0 / (1.0 + jnp.exp(-x)))


def _rms(x, g):
    return x * lax.rsqrt(jnp.mean(x * x, axis=-1, keepdims=True) + EPS) * g


def _ada_kernel(c_ref, w_ref, b_ref, o_ref, *, n_rows):
    s = _silu(c_ref[...])
    w = w_ref[...]
    rows = [jnp.sum(w * s[:, r:r + 1], axis=0, keepdims=True) for r in range(n_rows)]
    rows.append(jnp.zeros((SUBLANES - n_rows, w.shape[1]), F32))
    o_ref[...] = jnp.concatenate(rows, axis=0) + b_ref[...]


def _ada_mod(cvec_t, n_rows, ada_w, ada_b):
    n = ada_w.shape[-1]
    tn = 1536
    return pl.pallas_call(
        functools.partial(_ada_kernel, n_rows=n_rows),
        grid=(DEPTH, n // tn),
        in_specs=[pl.BlockSpec((D_MODEL, SUBLANES), lambda l, j: (0, 0)),
                  pl.BlockSpec((None, D_MODEL, tn), lambda l, j: (l, 0, j)),
                  pl.BlockSpec((None, 1, tn), lambda l, j: (l, 0, j))],
        out_specs=pl.BlockSpec((None, SUBLANES, tn), lambda l, j: (l, 0, j)),
        out_shape=jax.ShapeDtypeStruct((DEPTH, SUBLANES, n), F32),
        compiler_params=_params("arbitrary", "arbitrary"),
    )(cvec_t, ada_w, ada_b.reshape(DEPTH, 1, n))


def _normmod_matmul_kernel(x_ref, g_ref, sh_ref, sc_ref, w_ref, o_ref):
    h = _rms(x_ref[...], g_ref[...]) * (1.0 + sc_ref[...]) + sh_ref[...]
    o_ref[...] = jnp.dot(h.astype(BF16), w_ref[...], preferred_element_type=F32)


def _mod_index(n_mod):
    return (lambda b, i: (b, 0, 0)) if n_mod > 1 else (lambda b, i: (0, 0, 0))


def _normmod_matmul(x, g, shift, scale, w, tm):
    b, l, d = x.shape
    n = w.shape[1]
    mod_spec = pl.BlockSpec((None, 1, d), _mod_index(shift.shape[0]))
    return pl.pallas_call(
        _normmod_matmul_kernel,
        grid=(b, l // tm),
        in_specs=[pl.BlockSpec((None, tm, d), lambda b, i: (b, i, 0)),
                  pl.BlockSpec((1, d), lambda b, i: (0, 0)),
                  mod_spec, mod_spec,
                  pl.BlockSpec((d, n), lambda b, i: (0, 0))],
        out_specs=pl.BlockSpec((None, tm, n), lambda b, i: (b, i, 0)),
        out_shape=jax.ShapeDtypeStruct((b, l, n), F32),
        compiler_params=_params("arbitrary", "arbitrary"),
    )(x, g, shift, scale, w)


def _patch_rows(x, keep, starts):
    pieces, r = [], 0
    for s in starts:
        if s > r:
            pieces.append(x[r:s])
        pieces.append(jnp.where(keep[s:s + SUBLANES], x[s:s + SUBLANES], 0.0))
        r = s + SUBLANES
    if r < x.shape[0]:
        pieces.append(x[r:])
    return jnp.concatenate(pieces, axis=0)


def _mix_ffn_kernel(y_ref, yp_ref, yn_ref, a_ref, ap_ref, an_ref, b_ref, bp_ref, bn_ref, wo_ref, gate1_ref, g1_ref,
                    g2_ref, sh_ref, sc_ref, ua_ref, ug_ref, cwa_ref, cwg_ref, cba_ref, cbg_ref, dn_ref, gate2_ref,
                    g3_ref, o_ref, y1_scr, h_scr, acc_scr, *, seq_len):
    i = pl.program_id(1)
    f = pl.program_id(2)
    tm = y_ref.shape[0]
    halo = yp_ref.shape[0]
    half = a_ref.shape[1]

    @pl.when(f == 0)
    def _():
        rows_of = lambda p, m, n: jnp.concatenate([p[...], m[...], n[...]], axis=0)
        a_all = rows_of(ap_ref, a_ref, an_ref).astype(BF16)
        b_all = rows_of(bp_ref, b_ref, bn_ref).astype(BF16)
        m = jnp.dot(a_all, wo_ref[:half, :], preferred_element_type=F32)
        m = m + jnp.dot(b_all, wo_ref[half:, :], preferred_element_type=F32)
        y1 = rows_of(yp_ref, y_ref, yn_ref) + gate1_ref[...] * _rms(m, g1_ref[...])
        y1_scr[...] = y1[halo:halo + tm]
        h_scr[...] = (_rms(y1, g2_ref[...]) * (1.0 + sc_ref[...]) + sh_ref[...]).astype(BF16)
        acc_scr[...] = jnp.zeros_like(acc_scr)

    rows = tm + 2 * halo
    pos = (i * tm + lax.broadcasted_iota(jnp.int32, (tm, 1), 0)) % seq_len
    has_prev = pos != 0
    has_next = pos != seq_len - 1
    period = math.gcd(tm, seq_len)
    first_groups = list(range(0, tm, period))
    last_groups = [s + period - SUBLANES for s in first_groups]

    def conv(u_ref, cw_ref, cb_ref):
        u = jnp.dot(h_scr[...], u_ref[...], preferred_element_type=F32)
        up = _patch_rows(pltpu.roll(u, 1, 0)[halo:halo + tm], has_prev, first_groups)
        un = _patch_rows(pltpu.roll(u, rows - 1, 0)[halo:halo + tm], has_next, last_groups)
        uc = u[halo:halo + tm]
        return up * cw_ref[0:1, :] + uc * cw_ref[1:2, :] + un * cw_ref[2:3, :] + cb_ref[...]

    a = conv(ua_ref, cwa_ref, cba_ref)
    gt = conv(ug_ref, cwg_ref, cbg_ref)
    act = (_silu(gt) * a).astype(BF16)
    acc_scr[...] += jnp.dot(act, dn_ref[...], preferred_element_type=F32)

    @pl.when(f == pl.num_programs(2) - 1)
    def _():
        o_ref[...] = y1_scr[...] + gate2_ref[...] * _rms(acc_scr[...], g3_ref[...])


def _mix_ffn(y, a, bm, w_out, gate1, g1, g2, shift, scale, up, cw, cb, down, layer, gate2, g3, seq_len, tm, tf):
    b, l, d = y.shape
    wa = a.shape[-1]
    nf = D_FF // tf
    halo = SUBLANES
    hb = tm // halo
    last_hb = l // halo - 1
    n_mod = shift.shape[0]
    mod_spec = pl.BlockSpec((None, 1, d), (lambda b, i, f: (b, 0, 0)) if n_mod > 1 else (lambda b, i, f: (0, 0, 0)))
    vec = lambda off: pl.BlockSpec((None, 1, tf), lambda b, i, f: (layer, 0, off + f))
    row_d = pl.BlockSpec((1, d), lambda b, i, f: (0, 0))

    def tiles(w):
        return [pl.BlockSpec((None, tm, w), lambda b, i, f: (b, i, 0)),
                pl.BlockSpec((None, halo, w), lambda b, i, f: (b, jnp.maximum(i * hb - 1, 0), 0)),
                pl.BlockSpec((None, halo, w), lambda b, i, f: (b, jnp.minimum((i + 1) * hb, last_hb), 0))]

    kern = functools.partial(_mix_ffn_kernel, seq_len=seq_len)
    return pl.pallas_call(
        kern,
        grid=(b, l // tm, nf),
        in_specs=tiles(d) + tiles(wa) + tiles(wa) + [
                  pl.BlockSpec((2 * wa, d), lambda b, i, f: (0, 0)),
                  mod_spec, row_d,
                  row_d, mod_spec, mod_spec,
                  pl.BlockSpec((None, d, tf), lambda b, i, f: (layer, 0, f)),
                  pl.BlockSpec((None, d, tf), lambda b, i, f: (layer, 0, nf + f)),
                  pl.BlockSpec((None, 3, tf), lambda b, i, f: (layer, 0, f)),
                  pl.BlockSpec((None, 3, tf), lambda b, i, f: (layer, 0, nf + f)),
                  vec(0), vec(nf),
                  pl.BlockSpec((None, tf, d), lambda b, i, f: (layer, f, 0)),
                  mod_spec, row_d],
        out_specs=pl.BlockSpec((None, tm, d), lambda b, i, f: (b, i, 0)),
        out_shape=jax.ShapeDtypeStruct((b, l, d), F32),
        scratch_shapes=[pltpu.VMEM((tm, d), F32), pltpu.VMEM((tm + 2 * halo, d), BF16), pltpu.VMEM((tm, d), F32)],
        compiler_params=_params("arbitrary", "arbitrary", "arbitrary"),
    )(y, y, y, a, a, a, bm, bm, bm, w_out, gate1, g1, g2, shift, scale, up, up, cw, cw, cb, cb, down, gate2, g3)


def _scan_tables(c, heads_per_group):
    nlev = int(math.log2(c))
    t = np.arange(c)[:, None]
    r = np.arange(c)[None, :]
    masks_f = [np.eye(c, dtype=bool)]
    for lev in range(1, nlev + 1):
        bsz = 2 ** lev
        mid = (t // bsz) * bsz + bsz // 2
        masks_f.append(((t // bsz) == (r // bsz)) & (t >= mid) & (r < mid))
    m_f = np.stack(masks_f).astype(np.float32)
    m_b = np.transpose(m_f, (0, 2, 1))
    tri = lambda a: jnp.asarray(np.concatenate([a, a, a], axis=1).astype(np.float32), dtype=BF16)
    rep = lambda m: jnp.asarray(np.concatenate([m] * heads_per_group, axis=2))
    return tri(r <= t), tri(r >= t), rep(m_f), rep(m_b)


def _level_decay(cum, ncum, lg2, lev, bwd):
    c, w = cum.shape
    b = 1 << lev
    half = b // 2
    off = half - 1 + int(bwd)
    if b == 2:
        odd = lax.broadcasted_iota(jnp.int32, (c, 1), 0) % 2 == 1
        return jnp.where(odd != bwd, lg2, 0.0)
    if b >= 2 * SUBLANES:
        pieces = []
        for b0 in range(0, c, b):
            for rows, is_upper in ((slice(b0, b0 + half), False), (slice(b0 + half, b0 + b), True)):
                src = cum if is_upper != bwd else ncum
                pieces.append(src[rows] - jnp.broadcast_to(src[b0 + off:b0 + off + 1], (half, w)))
        return jnp.concatenate(pieces, axis=0)
    cum3 = cum.reshape(c // SUBLANES, SUBLANES, w)
    sub = lax.broadcasted_iota(jnp.int32, (1, SUBLANES, 1), 1)
    if b == SUBLANES:
        ref3 = jnp.broadcast_to(cum3[:, off:off + 1], cum3.shape)
    else:
        ref3 = jnp.where(sub < b, cum3[:, off:off + 1], cum3[:, b + off:b + off + 1])
    upper = (sub % b) >= half
    sgn = jnp.where(upper != bwd, 1.0, -1.0)
    return ((cum3 - ref3) * sgn).reshape(c, w)


def _head_stack(xb, width):
    n = xb.shape[1] // width
    lane = lax.broadcasted_iota(jnp.int32, (1, xb.shape[1]), 1)
    zero = jnp.zeros_like(xb)
    return jnp.concatenate([jnp.where((lane >= j * width) & (lane < (j + 1) * width), xb, zero) for j in range(n)],
                           axis=0)


def _scan_group(chunks):
    dk = chunks[0][0].shape[1] // N_HEADS
    hpg = HEADS_PER_GROUP
    gw = hpg * dk
    ngroups = N_HEADS // hpg
    vw = hpg * HEAD_W
    work = []
    for q, k, v, lg2, tri_ref, m_ref, st_ref, bwd in chunks:
        hi = lg2.astype(BF16)
        r1 = lg2 - hi.astype(F32)
        mid = r1.astype(BF16)
        lo = (r1 - mid.astype(F32)).astype(BF16)
        cum = jnp.dot(tri_ref[...], jnp.concatenate([hi, mid, lo], axis=0), preferred_element_type=F32)
        work.append(dict(q=q, k=k, v=v, lg2=lg2, cum=cum, ncum=-cum, m_ref=m_ref, st_ref=st_ref, bwd=bwd,
                         q16=q.astype(BF16), k16=k.astype(BF16), att=[None] * ngroups))
    nlev = chunks[0][5].shape[0] - 1
    for lev in range(nlev + 1):
        for w in work:
            if lev == 0:
                qb, kb = w["q16"], w["k16"]
            else:
                e = jnp.exp2(_level_decay(w["cum"], w["ncum"], w["lg2"], lev, w["bwd"])).astype(BF16)
                qb, kb = w["q16"] * e, w["k16"] * e
            for g in range(ngroups):
                sl = slice(g * gw, (g + 1) * gw)
                prod = lax.dot_general(qb[:, sl], _head_stack(kb[:, sl], dk), NT_DIMS,
                                       preferred_element_type=F32)
                term = w["m_ref"][lev] * prod
                w["att"][g] = term if w["att"][g] is None else w["att"][g] + term
    lane = lax.broadcasted_iota(jnp.int32, (1, gw), 1)
    results = []
    for w in work:
        q, k, v, cum, st_ref = w["q"], w["k"], w["v"], w["cum"], w["st_ref"]
        c = q.shape[0]
        last = 0 if w["bwd"] else c - 1
        e_cum = jnp.exp2(cum)
        d_last = e_cum[last:last + 1, :]
        qe = (q * e_cum).astype(BF16)
        kd = (k * jnp.exp2(cum[last:last + 1, :] - cum)).astype(BF16)
        vb = v.astype(BF16)
        outs = []
        for g in range(ngroups):
            sl = slice(g * gw, (g + 1) * gw)
            st = st_ref[g]
            stb = st.astype(BF16)
            o_g = jnp.dot(w["att"][g].astype(BF16), _head_stack(vb[:, g * vw:(g + 1) * vw], HEAD_W),
                          preferred_element_type=F32)
            qe_g = qe[:, sl]
            zero = jnp.zeros_like(qe_g)
            inter = [lax.dot_general(jnp.where((lane >= j * dk) & (lane < (j + 1) * dk), qe_g, zero), stb, NT_DIMS,
                                     preferred_element_type=F32) for j in range(hpg)]
            outs.append(o_g + jnp.concatenate(inter, axis=1))
            vstack = jnp.concatenate([v[:, g * vw + j * HEAD_W:g * vw + (j + 1) * HEAD_W] for j in range(hpg)],
                                     axis=0)
            st_ref[g] = st * d_last[:, sl] + jnp.dot(vstack.T.astype(BF16), _head_stack(kd[:, sl], dk),
                                                      preferred_element_type=F32)
        results.append(jnp.concatenate(outs, axis=1))
    return results


def _bidir_scan_body(load_fwd, load_bwd, g_ref, s0_ref, norm_ref, af_ref, ab_ref, mf_ref, mb_ref,
                     o_ref, s_out_ref, of_scr, ob_scr, stf_scr, stb_scr, seq_len):
    c = SCAN_CHUNK
    j = pl.program_id(1)
    nblk = pl.num_programs(1)
    n = SCAN_ROWS // c
    ngroups, _, gw = stf_scr.shape
    hpg = N_HEADS // ngroups
    dk = gw // hpg

    @pl.when(j == 0)
    def _():
        for g in range(ngroups):
            if s0_ref is None:
                stf_scr[g] = jnp.zeros((HEAD_W, gw), F32)
                stb_scr[g] = jnp.zeros((HEAD_W, gw), F32)
            else:
                stf_scr[g] = jnp.concatenate([s0_ref[0, g * hpg + j].T for j in range(hpg)], axis=1)
                stb_scr[g] = jnp.concatenate([s0_ref[1, g * hpg + j].T for j in range(hpg)], axis=1)

    base_f = j * SCAN_ROWS
    base_b = (nblk - 1 - j) * SCAN_ROWS

    def step(i, carry):
        chunks, stores = [], []
        for u in range(SCAN_GROUP):
            rf = pl.multiple_of((i * SCAN_GROUP + u) * c, c)
            chunks.append(load_fwd(rf) + (af_ref, mf_ref, stf_scr, False))
            stores.append((of_scr, pl.multiple_of(base_f + rf, c)))
        for u in range(SCAN_GROUP):
            rb = pl.multiple_of((n - 1 - i * SCAN_GROUP - u) * c, c)
            chunks.append(load_bwd(rb) + (ab_ref, mb_ref, stb_scr, True))
            stores.append((ob_scr, pl.multiple_of(base_b + rb, c)))
        for (scr, r0), o in zip(stores, _scan_group(chunks)):
            scr[pl.ds(r0, c), :] = o
        return carry

    lax.fori_loop(0, n // SCAN_GROUP, step, 0)

    @pl.when(j == nblk - 1)
    def _():
        def fin(jj, carry):
            r0 = pl.multiple_of(jj * SCAN_ROWS, SCAN_ROWS)
            o = of_scr[pl.ds(r0, SCAN_ROWS), :] + ob_scr[pl.ds(r0, SCAN_ROWS), :]
            parts = []
            for h in range(N_HEADS):
                sl = slice(h * HEAD_W, (h + 1) * HEAD_W)
                parts.append(_rms(o[:, sl], norm_ref[:, sl]))
            o_ref[pl.ds(r0, SCAN_ROWS), :] = jnp.concatenate(parts, axis=1) * _silu(g_ref[pl.ds(r0, SCAN_ROWS), :])
            return carry

        lax.fori_loop(0, seq_len // SCAN_ROWS, fin, 0)
        for h in range(N_HEADS):
            g, hs = h // hpg, slice((h % hpg) * dk, (h % hpg + 1) * dk)
            s_out_ref[0, h] = stf_scr[g][:, hs].T
            s_out_ref[1, h] = stb_scr[g][:, hs].T


def _hgrn_kernel(*refs, layer, has_s0, seq_len):
    qf_ref, ff_ref, if_ref, qb_ref, fb_ref, ib_ref, g_ref, lbraw_ref, norm_ref, af_ref, ab_ref, mf_ref, mb_ref = refs[:13]
    s0_ref = refs[13] if has_s0 else None
    o_ref, s_out_ref, of_scr, ob_scr, stf_scr, stb_scr = refs[13 + has_s0:]
    c = SCAN_CHUNK
    raw = lbraw_ref[...]
    ex = jnp.exp(raw - jnp.max(raw, axis=0, keepdims=True))
    sm = ex / jnp.sum(ex, axis=0, keepdims=True)
    lb = sm[0]
    for j in range(1, layer + 1):
        lb = lb + sm[j]

    def load(q_ref, f_ref, i_ref, lb_row):
        def fn(r0):
            q = _silu(q_ref[pl.ds(r0, c), :]) * (HEAD_W ** -0.5)
            sig = 1.0 / (1.0 + jnp.exp(-f_ref[pl.ds(r0, c), :]))
            f = lb_row + (1.0 - lb_row) * sig
            return q, 1.0 - f, i_ref[pl.ds(r0, c), :], jnp.log2(f)
        return fn

    _bidir_scan_body(load(qf_ref, ff_ref, if_ref, lb[0:1]), load(qb_ref, fb_ref, ib_ref, lb[1:2]), g_ref,
                     s0_ref, norm_ref, af_ref, ab_ref, mf_ref, mb_ref,
                     o_ref, s_out_ref, of_scr, ob_scr, stf_scr, stb_scr, seq_len)


def _gla_kernel(*refs, has_s0, seq_len):
    fwd_refs, bwd_refs = refs[0:4], refs[4:8]
    g_ref, aw_ref, ab_ref, norm_ref, af_ref, abk_ref, mf_ref, mb_ref = refs[8:16]
    s0_ref = refs[16] if has_s0 else None
    o_ref, s_out_ref, of_scr, ob_scr, stf_scr, stb_scr = refs[16 + has_s0:]
    c = SCAN_CHUNK

    def load(d, q_ref, k_ref, v_ref, da_ref):
        def fn(r0):
            q = q_ref[pl.ds(r0, c), :] * (DK_D ** -0.5)
            da = da_ref[pl.ds(r0, c), :]
            da_hi = da.astype(BF16)
            da_mid = (da - da_hi.astype(F32)).astype(BF16)
            xa = jnp.dot(jnp.concatenate([da_hi, da_hi, da_mid], axis=1), aw_ref[d],
                         preferred_element_type=F32) + ab_ref[d]
            la = (jnp.minimum(xa, 0.0) - jnp.log(1.0 + jnp.exp(-jnp.abs(xa)))) * (LOG2E / GLA_TAU)
            return q, k_ref[pl.ds(r0, c), :], v_ref[pl.ds(r0, c), :], la
        return fn

    _bidir_scan_body(load(0, *fwd_refs), load(1, *bwd_refs), g_ref,
                     s0_ref, norm_ref, af_ref, abk_ref, mf_ref, mb_ref,
                     o_ref, s_out_ref, of_scr, ob_scr, stf_scr, stb_scr, seq_len)


def _scan_call(kern, proj, streams, extra, s0, seq_len, dk):
    b = proj.shape[0]
    nblk = seq_len // SCAN_ROWS
    hpg = HEADS_PER_GROUP
    tabs = _scan_tables(SCAN_CHUNK, hpg)
    full = lambda a: pl.BlockSpec(a.shape, lambda i, j, _n=a.ndim: (0,) * _n)
    in_specs = []
    for cb, w, kind in streams:
        if kind == "f":
            in_specs.append(pl.BlockSpec((None, SCAN_ROWS, w), lambda i, j, _c=cb: (i, j, _c)))
        elif kind == "b":
            in_specs.append(pl.BlockSpec((None, SCAN_ROWS, w), lambda i, j, _c=cb: (i, nblk - 1 - j, _c)))
        else:
            in_specs.append(pl.BlockSpec((None, seq_len, w), lambda i, j, _c=cb: (i, 0, _c)))
    args = [proj] * len(streams)
    for a in tuple(extra) + tabs:
        in_specs.append(full(a))
        args.append(a)
    if s0 is not None:
        in_specs.append(pl.BlockSpec((None, 2, N_HEADS, dk, HEAD_W), lambda i, j: (i, 0, 0, 0, 0)))
        args.append(s0)
    return pl.pallas_call(
        kern,
        grid=(b, nblk),
        in_specs=in_specs,
        out_specs=[pl.BlockSpec((None, seq_len, MIX_W), lambda i, j: (i, 0, 0)),
                   pl.BlockSpec((None, None, 2, N_HEADS, dk, HEAD_W), lambda i, j: (i, 0, 0, 0, 0, 0))],
        out_shape=[jax.ShapeDtypeStruct((b, seq_len, MIX_W), F32),
                   jax.ShapeDtypeStruct((b, 1, 2, N_HEADS, dk, HEAD_W), F32)],
        scratch_shapes=[pltpu.VMEM((seq_len, MIX_W), F32), pltpu.VMEM((seq_len, MIX_W), F32),
                        pltpu.VMEM((N_HEADS // hpg, HEAD_W, hpg * dk), F32),
                        pltpu.VMEM((N_HEADS // hpg, HEAD_W, hpg * dk), F32)],
        compiler_params=_params("arbitrary", "arbitrary"),
    )(*args)


def _hgrn_mixer(proj, hgrn_lb, norm, s0, layer):
    seq_len = proj.shape[1]
    kern = functools.partial(_hgrn_kernel, layer=layer, has_s0=s0 is not None, seq_len=seq_len)
    streams = [(0, MIX_W, "f"), (1, MIX_W, "f"), (3, MIX_W, "f"),
               (0, MIX_W, "b"), (2, MIX_W, "b"), (3, MIX_W, "b"), (4, MIX_W, "w")]
    return _scan_call(kern, proj, streams, (hgrn_lb, norm.reshape(1, MIX_W)), s0, seq_len, HEAD_W)


def _gla_mixer(proj, aw, ab, norm, s0):
    seq_len = proj.shape[1]
    kern = functools.partial(_gla_kernel, has_s0=s0 is not None, seq_len=seq_len)
    kw = N_HEADS * DK_D
    streams = [(1536 // kw, kw, "f"), (1792 // kw, kw, "f"), (2048 // MIX_W, MIX_W, "f"), (3072 // LANES, LANES, "f"),
               (1536 // kw, kw, "b"), (1792 // kw, kw, "b"), (2048 // MIX_W, MIX_W, "b"), (3072 // LANES, LANES, "b"),
               (2560 // MIX_W, MIX_W, "w")]
    return _scan_call(kern, proj, streams, (aw, ab, norm.reshape(1, MIX_W)), s0, seq_len, DK_D)


def _dwconv3_rows(x, w_ref, b_ref):
    l = x.shape[0]
    row = lax.broadcasted_iota(jnp.int32, (l, 1), 0)
    xp = _patch_rows(pltpu.roll(x, 1, 0), row != 0, [0])
    xn = _patch_rows(pltpu.roll(x, l - 1, 0), row != l - 1, [l - SUBLANES])
    return xp * w_ref[0:1, :] + x * w_ref[1:2, :] + xn * w_ref[2:3, :] + b_ref[...]


def _hyena_filter_kernel(z_ref, w1_ref, b1_ref, w2_ref, b2_ref, w3_ref, fr_ref, dl_ref, o_ref):
    z = z_ref[...]
    fr = fr_ref[...]
    h = jnp.sin(fr * (jnp.dot(z, w1_ref[...], preferred_element_type=F32, precision=HIGHEST) + b1_ref[...]))
    h = jnp.sin(fr * (jnp.dot(h, w2_ref[...], preferred_element_type=F32, precision=HIGHEST) + b2_ref[...]))
    h_hi = h.astype(BF16)
    h_mid = (h - h_hi.astype(F32)).astype(BF16)
    h = jnp.dot(jnp.concatenate([h_hi, h_hi, h_mid], axis=1), w3_ref[...],
                preferred_element_type=F32)
    win = jnp.exp(-z[:, 0:1] * dl_ref[...])
    o_ref[...] = h * jnp.concatenate([win] * (2 * HY_ORDER), axis=1)


def _hyena_pos_features(l):
    t = jnp.linspace(0.0, 1.0, l, dtype=F32)[:, None]
    w = 2.0 * math.pi * jnp.arange(l, dtype=F32)[:, None] / l
    fb = jnp.linspace(1e-4, HY_BANDS - 1, HY_BANDS, dtype=F32)[None]
    z = jnp.concatenate([t, jnp.cos(fb * w), -jnp.sin(fb * w)], axis=-1)
    return jnp.pad(z, ((0, 0), (0, HY_FF - HY_EMB)))


def _hyena_filters(lens, w1, b1, w2, b2, w3, freq):
    z = jnp.concatenate([_hyena_pos_features(l) for l in lens], axis=0)
    rows = z.shape[0]
    tr = 256
    w1p = jnp.pad(w1, ((0, HY_FF - HY_EMB), (0, 0)))
    max_decay = math.log(HY_TARGET) / HY_FAST
    min_decay = math.log(HY_TARGET) / HY_SLOW
    deltas = jnp.abs(jnp.linspace(min_decay, max_decay, W_B, dtype=F32))[None]
    nout = w3.shape[1]
    w3_hi = w3.astype(BF16)
    w3_mid = (w3 - w3_hi.astype(F32)).astype(BF16)
    w3s = jnp.concatenate([w3_hi, w3_mid, w3_hi], axis=0)
    full = lambda a: pl.BlockSpec(a.shape, lambda i, _n=a.ndim: (0,) * _n)
    ins = (w1p, b1.reshape(1, -1), w2, b2.reshape(1, -1), w3s, freq.reshape(1, -1), deltas)
    return pl.pallas_call(
        _hyena_filter_kernel,
        grid=(rows // tr,),
        in_specs=[pl.BlockSpec((tr, HY_FF), lambda i: (i, 0))] + [full(a) for a in ins],
        out_specs=pl.BlockSpec((tr, nout), lambda i: (i, 0)),
        out_shape=jax.ShapeDtypeStruct((rows, nout), F32),
        compiler_params=_params("arbitrary"),
    )(z, *ins)


def _dft_table_kernel(ca_ref, sa_ref, cb_ref, sb_ref, cos_ref, sinf_ref, sini_ref):
    tk = ca_ref.shape[0]
    ca, sa = ca_ref[...], sa_ref[...]
    row = pl.program_id(0) * tk + lax.broadcasted_iota(jnp.int32, (tk, 1), 0)
    lane = lax.broadcasted_iota(jnp.int32, (1, LANES), 1)
    alt_row = jnp.where(row % 2 == 0, 1.0, -1.0)
    for grp in range(cb_ref.shape[1]):
        cols = slice(grp * LANES, (grp + 1) * LANES)
        cbg, sbg = cb_ref[:, grp:grp + 1], sb_ref[:, grp:grp + 1]
        sin_t = sa * cbg + ca * sbg
        col = grp * LANES + lane
        cos_ref[:, cols] = (ca * cbg - sa * sbg).astype(BF16)
        sinf_ref[:, cols] = jnp.where(row == 0, jnp.where(col % 2 == 0, 1.0, -1.0), sin_t).astype(BF16)
        sini_ref[:, cols] = jnp.where(col == 0, alt_row, sin_t).astype(BF16)


def _dft_tables(l):
    n = 2 * l
    k = jnp.arange(l, dtype=jnp.int32)[:, None]
    t1 = jnp.arange(LANES, dtype=jnp.int32)[None, :]
    t2 = (jnp.arange(l // LANES, dtype=jnp.int32) * LANES)[None, :]
    ang = lambda m: (m % n).astype(F32) * (2.0 * math.pi / n)
    small = (jnp.cos(ang(k * t1)), jnp.sin(ang(k * t1)), jnp.cos(ang(k * t2)), jnp.sin(ang(k * t2)))
    tk = min(l, 256)
    out = jax.ShapeDtypeStruct((l, l), BF16)
    ospec = pl.BlockSpec((tk, l), lambda i: (i, 0))
    return pl.pallas_call(
        _dft_table_kernel,
        grid=(l // tk,),
        in_specs=[pl.BlockSpec((tk, a.shape[1]), lambda i: (i, 0)) for a in small],
        out_specs=[ospec, ospec, ospec],
        out_shape=[out, out, out],
        compiler_params=_params("arbitrary"),
    )(*small)


def _hyena_spectrum_kernel(c_ref, s_ref, f0_ref, f1_ref, kr_ref, kia_ref, krb_ref, *, seq_len):
    kt = pl.program_id(1)
    tk = c_ref.shape[0]
    row = lax.broadcasted_iota(jnp.int32, (seq_len, 1), 0)
    f0 = f0_ref[...]
    f1 = jnp.where(row != 0, f1_ref[...], 0.0)
    fsum = f0 + f1
    kr = jnp.dot(c_ref[...], fsum.astype(BF16), preferred_element_type=F32)
    ki = jnp.dot(s_ref[...], (f1 - f0).astype(BF16), preferred_element_type=F32)
    nyq = jnp.sum(jnp.where(row % 2 == 0, fsum, -fsum), axis=0, keepdims=True)
    krow = kt * tk + lax.broadcasted_iota(jnp.int32, (tk, 1), 0)
    dc = krow == 0
    wk = jnp.where(dc, 1.0, 2.0) * (1.0 / (2 * seq_len))
    kr_ref[...] = kr * wk
    kia_ref[...] = jnp.where(dc, 0.0, ki) * wk
    krb_ref[...] = jnp.where(dc, nyq, kr) * wk


def _hyena_spectrum(filt, row0, seq_len, tables, tk):
    cos_t, sin_f, _ = tables
    rb = row0 // seq_len
    out = jax.ShapeDtypeStruct((HY_ORDER, seq_len, W_B), F32)
    kern = functools.partial(_hyena_spectrum_kernel, seq_len=seq_len)
    ospec = pl.BlockSpec((None, tk, W_B), lambda o, kt: (o, kt, 0))
    return pl.pallas_call(
        kern,
        grid=(HY_ORDER, seq_len // tk),
        in_specs=[pl.BlockSpec((tk, seq_len), lambda o, kt: (kt, 0)),
                  pl.BlockSpec((tk, seq_len), lambda o, kt: (kt, 0)),
                  pl.BlockSpec((seq_len, W_B), lambda o, kt: (rb, 2 * o)),
                  pl.BlockSpec((seq_len, W_B), lambda o, kt: (rb, 2 * o + 1))],
        out_specs=[ospec, ospec, ospec],
        out_shape=[out, out, out],
        compiler_params=_params("arbitrary", "arbitrary"),
    )(cos_t, sin_f, filt, filt)


def _hyena_order_kernel(zin_ref, gate_ref, cwz_ref, cbz_ref, cwg_ref, cbg_ref, d_ref, kr_ref, kia_ref, krb_ref,
                        cf_ref, sf_ref, ci_ref, si_ref, o_ref, z_scr, zb_scr, acc_scr, *, conv_input):
    kt = pl.program_id(1)

    @pl.when(kt == 0)
    def _():
        z = zin_ref[...]
        if conv_input:
            z = _dwconv3_rows(z, cwz_ref, cbz_ref)
        z_scr[...] = z
        zb_scr[...] = z.astype(BF16)
        acc_scr[...] = jnp.zeros_like(acc_scr)

    zb = zb_scr[...]
    p = jnp.dot(cf_ref[...], zb, preferred_element_type=F32)
    q = jnp.dot(sf_ref[...], zb, preferred_element_type=F32)
    kia = kia_ref[...]
    yr = (p * kr_ref[...] + q * kia).astype(BF16)
    yi = (q * krb_ref[...] - p * kia).astype(BF16)
    acc_scr[...] += (jnp.dot(ci_ref[...], yr, preferred_element_type=F32)
                     + jnp.dot(si_ref[...], yi, preferred_element_type=F32))

    @pl.when(kt == pl.num_programs(1) - 1)
    def _():
        gate = _dwconv3_rows(gate_ref[...], cwg_ref, cbg_ref)
        o_ref[...] = gate * (acc_scr[...] + z_scr[...] * d_ref[...])


def _hyena_order(zin, zin_col, proj, order, conv_w, conv_b, hy_d, spectrum, tables, tk):
    b, seq_len = proj.shape[0], proj.shape[1]
    cos_t, sin_f, sin_i = tables
    kr, kia, krb = spectrum
    hy0 = 5
    conv_input = order == 0
    cw = conv_w.reshape(3, 1 + HY_ORDER, W_B).transpose(1, 0, 2)
    cbias = conv_b.reshape(1 + HY_ORDER, 1, W_B)
    kern = functools.partial(_hyena_order_kernel, conv_input=conv_input)
    kspec = pl.BlockSpec((None, tk, W_B), lambda i, kt: (order, kt, 0))
    return pl.pallas_call(
        kern,
        grid=(b, seq_len // tk),
        in_specs=[pl.BlockSpec((None, seq_len, W_B), lambda i, kt: (i, 0, zin_col)),
                  pl.BlockSpec((None, seq_len, W_B), lambda i, kt: (i, 0, hy0 + 1 + order)),
                  pl.BlockSpec((None, 3, W_B), lambda i, kt: (0, 0, 0)),
                  pl.BlockSpec((None, 1, W_B), lambda i, kt: (0, 0, 0)),
                  pl.BlockSpec((None, 3, W_B), lambda i, kt: (1 + order, 0, 0)),
                  pl.BlockSpec((None, 1, W_B), lambda i, kt: (1 + order, 0, 0)),
                  pl.BlockSpec((None, 1, W_B), lambda i, kt: (order, 0, 0)),
                  kspec, kspec, kspec,
                  pl.BlockSpec((tk, seq_len), lambda i, kt: (kt, 0)),
                  pl.BlockSpec((tk, seq_len), lambda i, kt: (kt, 0)),
                  pl.BlockSpec((seq_len, tk), lambda i, kt: (0, kt)),
                  pl.BlockSpec((seq_len, tk), lambda i, kt: (0, kt))],
        out_specs=pl.BlockSpec((None, seq_len, W_B), lambda i, kt: (i, 0, 0)),
        out_shape=jax.ShapeDtypeStruct((b, seq_len, W_B), F32),
        scratch_shapes=[pltpu.VMEM((seq_len, W_B), F32), pltpu.VMEM((seq_len, W_B), BF16),
                        pltpu.VMEM((seq_len, W_B), F32)],
        compiler_params=_params("arbitrary", "arbitrary"),
    )(zin, proj, cw, cbias, cw, cbias, hy_d.reshape(HY_ORDER, 1, W_B), kr, kia, krb, cos_t, sin_f, cos_t, sin_i)


def _hyena_short_kernel(v_ref, x1_ref, x2_ref, cw_ref, cb_ref, d_ref, kr_ref, kia_ref, krb_ref,
                        cf_ref, sf_ref, si_ref, o_ref):
    cf, sf, si = cf_ref[...], sf_ref[...], si_ref[...]
    nb = v_ref.shape[0]
    zs = [_dwconv3_rows(v_ref[i], cw_ref.at[0], cb_ref.at[0]) for i in range(nb)]
    for order, gate_ref in enumerate((x1_ref, x2_ref)):
        zb = [z.astype(BF16) for z in zs]
        ps = [jnp.dot(cf, z, preferred_element_type=F32) for z in zb]
        qs = [jnp.dot(sf, z, preferred_element_type=F32) for z in zb]
        kia = kia_ref[order]
        yr = [(p * kr_ref[order] + q * kia).astype(BF16) for p, q in zip(ps, qs)]
        yi = [(q * krb_ref[order] - p * kia).astype(BF16) for p, q in zip(ps, qs)]
        conv = [jnp.dot(cf, r, preferred_element_type=F32) + jnp.dot(si, m, preferred_element_type=F32)
                for r, m in zip(yr, yi)]
        gates = [_dwconv3_rows(gate_ref[i], cw_ref.at[1 + order], cb_ref.at[1 + order]) for i in range(nb)]
        zs = [g * (c + z * d_ref[order]) for g, c, z in zip(gates, conv, zs)]
    for i in range(nb):
        o_ref[i] = zs[i]


def _hyena_short(proj, conv_w, conv_b, hy_d, spectrum, tables):
    b, seq_len = proj.shape[0], proj.shape[1]
    cos_t, sin_f, sin_i = tables
    nb = 2 if b % 2 == 0 else 1
    hy0 = 5
    cw = conv_w.reshape(3, 1 + HY_ORDER, W_B).transpose(1, 0, 2)
    cbias = conv_b.reshape(1 + HY_ORDER, 1, W_B)
    full = lambda a: pl.BlockSpec(a.shape, lambda i, _n=a.ndim: (0,) * _n)
    col = lambda j: pl.BlockSpec((nb, seq_len, W_B), lambda i: (i, 0, hy0 + j))
    consts = (cw, cbias, hy_d.reshape(HY_ORDER, 1, W_B)) + tuple(spectrum) + (cos_t, sin_f, sin_i)
    return pl.pallas_call(
        _hyena_short_kernel,
        grid=(b // nb,),
        in_specs=[col(0), col(1), col(2)] + [full(a) for a in consts],
        out_specs=pl.BlockSpec((nb, seq_len, W_B), lambda i: (i, 0, 0)),
        out_shape=jax.ShapeDtypeStruct((b, seq_len, W_B), F32),
        compiler_params=_params("arbitrary"),
    )(proj, proj, proj, *consts)


def _hyena_mixer(proj, conv_w, conv_b, hy_d, spectrum, tables, tk):
    if tk == proj.shape[1]:
        return _hyena_short(proj, conv_w, conv_b, hy_d, spectrum, tables)
    z = _hyena_order(proj, 5, proj, 0, conv_w, conv_b, hy_d, spectrum, tables, tk)
    return _hyena_order(z, 0, proj, 1, conv_w, conv_b, hy_d, spectrum, tables, tk)


def _diff_lambda(lp_ref, lam_init):
    lp = lp_ref[...]
    a = jnp.sum(lp[0:1] * lp[1:2], axis=-1, keepdims=True)
    b = jnp.sum(lp[2:3] * lp[3:4], axis=-1, keepdims=True)
    return jnp.exp(a) - jnp.exp(b) + lam_init


def _diff_attend(q, keys_b, vals_b, lam):
    lane = lax.broadcasted_iota(jnp.int32, (1, 2 * DH_C), 1)
    qs = q * (DH_C ** -0.5 * LOG2E)

    def attend(sel):
        s = lax.dot_general(jnp.where(sel, qs, 0.0).astype(BF16), keys_b, NT_DIMS, preferred_element_type=F32)
        e = jnp.exp2(s - jnp.max(s, axis=-1, keepdims=True))
        den = jnp.sum(e, axis=-1, keepdims=True)
        return jnp.dot(e.astype(BF16), vals_b, preferred_element_type=F32) / den

    return attend(lane < DH_C) - lam * attend(lane >= DH_C)


def _attn_prompt_kernel(q_ref, k_ref, v_ref, lp_ref, norm_ref, o_ref, kc_ref, vc_ref, *, lam_init):
    hw = 2 * DH_C
    lam = _diff_lambda(lp_ref, lam_init)
    for h in range(N_HEADS):
        sl = slice(h * hw, (h + 1) * hw)
        k = k_ref[:, sl]
        v = v_ref[:, sl]
        o = _diff_attend(q_ref[:, sl], k.astype(BF16), v.astype(BF16), lam)
        o_ref[:, sl] = _rms(o, norm_ref[:, sl]) * (1.0 - lam_init)
        kc_ref[h] = k
        vc_ref[h] = v


def _attn_prompt(proj, diff_lambda, diff_norm, lam_init):
    b, seq_len = proj.shape[0], proj.shape[1]
    hw = 2 * DH_C
    w = N_HEADS * hw
    kern = functools.partial(_attn_prompt_kernel, lam_init=lam_init)
    col = lambda j: pl.BlockSpec((None, seq_len, w), lambda i: (i, 0, j))
    cache_spec = pl.BlockSpec((None, None, N_HEADS, seq_len, hw), lambda i: (i, 0, 0, 0, 0))
    cache_shape = jax.ShapeDtypeStruct((b, 1, N_HEADS, seq_len, hw), F32)
    return pl.pallas_call(
        kern,
        grid=(b,),
        in_specs=[col(0), col(1), col(2),
                  pl.BlockSpec((4, DH_C), lambda i: (0, 0)),
                  pl.BlockSpec((1, w), lambda i: (0, 0))],
        out_specs=[pl.BlockSpec((None, seq_len, w), lambda i: (i, 0, 0)), cache_spec, cache_spec],
        out_shape=[jax.ShapeDtypeStruct((b, seq_len, w), F32), cache_shape, cache_shape],
        compiler_params=_params("arbitrary"),
    )(proj, proj, proj, diff_lambda, diff_norm.reshape(1, -1))


def _rope(x, cos, sin_signed):
    lane = lax.broadcasted_iota(jnp.int32, (1, x.shape[-1]), 1)
    first = (lane % 32) < 16
    partner = jnp.where(first, pltpu.roll(x, x.shape[-1] - 16, 1), pltpu.roll(x, 16, 1))
    return x * cos + partner * sin_signed


def _attn_sample_kernel(q_ref, k_ref, v_ref, ck_ref, cv_ref, cosq_ref, sinq_ref, cosk_ref, sink_ref,
                        lp_ref, norm_ref, o_ref, keys_scr, vals_scr, *, lam_init):
    past = ck_ref.shape[1]
    hw = 2 * DH_C

    @pl.when(pl.program_id(1) == 0)
    def _():
        for h in range(N_HEADS):
            sl = slice(h * hw, (h + 1) * hw)
            keys_scr[h, 0:past, :] = ck_ref[h].astype(BF16)
            vals_scr[h, 0:past, :] = cv_ref[h].astype(BF16)
            keys_scr[h, past:, :] = _rope(k_ref[:, sl], cosk_ref[...], sink_ref[...]).astype(BF16)
            vals_scr[h, past:, :] = v_ref[:, sl].astype(BF16)

    lam = _diff_lambda(lp_ref, lam_init)
    for h in range(N_HEADS):
        sl = slice(h * hw, (h + 1) * hw)
        q = _rope(q_ref[:, sl], cosq_ref[...], sinq_ref[...])
        o = _diff_attend(q, keys_scr[h], vals_scr[h], lam)
        o_ref[:, sl] = _rms(o, norm_ref[:, sl]) * (1.0 - lam_init)


def _rope_tables(seq_len):
    pos = jnp.arange(seq_len)[:, None]
    half = DH_C // 2
    nfreq = half // 2
    inv = ROPE_BASE ** (-jnp.arange(0, half, 2, dtype=F32) / half)
    lane = jnp.arange(2 * DH_C)[None, :]
    p = jnp.where((lane % DH_C) < half, pos // GRID_W, pos % GRID_W).astype(F32)
    ang = p * inv[lane[0] % nfreq][None, :]
    return jnp.cos(ang), jnp.where((lane % half) < nfreq, -jnp.sin(ang), jnp.sin(ang))


def _attn_sample(proj, ctx_k, ctx_v, diff_lambda, diff_norm, lam_init, tq):
    b, seq_len = proj.shape[0], proj.shape[1]
    past = ctx_k.shape[2]
    hw = 2 * DH_C
    w = N_HEADS * hw
    cos, sin = _rope_tables(seq_len)
    kern = functools.partial(_attn_sample_kernel, lam_init=lam_init)
    ctx_spec = pl.BlockSpec((None, N_HEADS, past, hw), lambda i, j: (i, 0, 0, 0))
    return pl.pallas_call(
        kern,
        grid=(b, seq_len // tq),
        in_specs=[pl.BlockSpec((None, tq, w), lambda i, j: (i, j, 0)),
                  pl.BlockSpec((None, seq_len, w), lambda i, j: (i, 0, 1)),
                  pl.BlockSpec((None, seq_len, w), lambda i, j: (i, 0, 2)),
                  ctx_spec, ctx_spec,
                  pl.BlockSpec((tq, hw), lambda i, j: (j, 0)),
                  pl.BlockSpec((tq, hw), lambda i, j: (j, 0)),
                  pl.BlockSpec((seq_len, hw), lambda i, j: (0, 0)),
                  pl.BlockSpec((seq_len, hw), lambda i, j: (0, 0)),
                  pl.BlockSpec((4, DH_C), lambda i, j: (0, 0)),
                  pl.BlockSpec((1, w), lambda i, j: (0, 0))],
        out_specs=pl.BlockSpec((None, tq, w), lambda i, j: (i, j, 0)),
        out_shape=jax.ShapeDtypeStruct((b, seq_len, w), F32),
        scratch_shapes=[pltpu.VMEM((N_HEADS, past + seq_len, hw), BF16),
                        pltpu.VMEM((N_HEADS, past + seq_len, hw), BF16)],
        compiler_params=_params("arbitrary", "arbitrary"),
    )(proj, proj, proj, ctx_k, ctx_v, cos, sin, cos, sin, diff_lambda, diff_norm.reshape(1, -1))


def kernel(x_prompt, x_sample, state_hgrn, cache_diff_k, cache_diff_v, state_gla, c, c_ctx, ada_w, ada_b, norm_g, ffn_up, ffn_conv_w, ffn_conv_b, ffn_down, w_in_even, w_out_even, hgrn_lb, hgrn_norm, hy_conv_w, hy_conv_b, hy_w1, hy_b1, hy_w2, hy_b2, hy_w3, hy_freq, hy_d, w_in_odd, w_out_odd, diff_lambda, diff_norm, gla_aw, gla_ab, gla_norm):
    bp, lp, d = x_prompt.shape
    bs, ls, _ = x_sample.shape

    cvec_t = jnp.zeros((d, SUBLANES), F32).at[:, 0].set(c_ctx).at[:, 1:1 + bs].set(c.T)
    mod = _ada_mod(cvec_t, 1 + bs, ada_w, ada_b)

    yp = x_prompt.reshape(1, bp * lp, d)
    ys = x_sample
    tm = 512

    filt = _hyena_filters((ls, lp), hy_w1[0], hy_b1[0], hy_w2[0], hy_b2[0], hy_w3[0], hy_freq[0])
    tab_p, tab_s = _dft_tables(lp), _dft_tables(ls)
    spec_s = _hyena_spectrum(filt, 0, ls, tab_s, 512)
    spec_p = _hyena_spectrum(filt, ls, lp, tab_p, lp)

    ffn_up_b, ffn_down_b = ffn_up.astype(BF16), ffn_down.astype(BF16)
    ffn_cb = ffn_conv_b.reshape(DEPTH, 1, -1)
    outs = {}
    for l in range(DEPTH):
        m = mod[l].reshape(SUBLANES, 6, 1, d)
        mp = [m[0:1, j] for j in range(6)]
        ms = [m[1:1 + bs, j] for j in range(6)]
        g = [norm_g[l, j].reshape(1, d) for j in range(4)]
        if l % 2 == 0:
            e = l // 2
            w_in = w_in_even[e].astype(BF16)
            w_out = w_out_even[e].astype(BF16)
            pp = _normmod_matmul(yp, g[0], mp[0], mp[1], w_in, tm).reshape(bp, lp, -1)
            ps = _normmod_matmul(ys, g[0], ms[0], ms[1], w_in, tm)
            oa_p, st_p = _hgrn_mixer(pp, hgrn_lb, hgrn_norm[e], None, l)
            oa_s, _ = _hgrn_mixer(ps, hgrn_lb, hgrn_norm[e], state_hgrn[:, e], l)
            ob_p = _hyena_mixer(pp, hy_conv_w[e], hy_conv_b[e], hy_d[e], spec_p, tab_p, lp)
            ob_s = _hyena_mixer(ps, hy_conv_w[e], hy_conv_b[e], hy_d[e], spec_s, tab_s, 256)
            outs["hgrn"] = st_p
        else:
            o = l // 2
            lam_init = 0.8 - 0.6 * math.exp(-0.3 * l)
            pad_cols = -w_in_odd.shape[-1] % LANES
            w_in = jnp.pad(w_in_odd[o], ((0, 0), (0, pad_cols))).astype(BF16)
            w_out = w_out_odd[o].astype(BF16)
            kw = N_HEADS * DK_D
            aw = jnp.zeros((2, LANES, kw), F32)
            aw = aw.at[0, 0:GLA_RANK].set(gla_aw[o, 0]).at[1, GLA_RANK:2 * GLA_RANK].set(gla_aw[o, 1])
            aw_hi = aw.astype(BF16)
            aw_mid = (aw - aw_hi.astype(F32)).astype(BF16)
            aw = jnp.concatenate([aw_hi, aw_mid, aw_hi], axis=1)
            ab = gla_ab[o].reshape(2, 1, kw)
            s0 = state_gla[:, o]
            pp = _normmod_matmul(yp, g[0], mp[0], mp[1], w_in, tm).reshape(bp, lp, -1)
            ps = _normmod_matmul(ys, g[0], ms[0], ms[1], w_in, tm)
            oa_p, kc, vc = _attn_prompt(pp, diff_lambda[o], diff_norm[o], lam_init)
            oa_s = _attn_sample(ps, cache_diff_k[:, o], cache_diff_v[:, o], diff_lambda[o], diff_norm[o], lam_init, 256)
            ob_p, st_p = _gla_mixer(pp, aw, ab, gla_norm[o], None)
            ob_s, _ = _gla_mixer(ps, aw, ab, gla_norm[o], s0)
            outs["k"], outs["v"], outs["gla"] = kc, vc, st_p
        ffn_args = (ffn_up_b, ffn_conv_w, ffn_cb, ffn_down_b, l)
        yp = _mix_ffn(yp, oa_p.reshape(1, bp * lp, -1), ob_p.reshape(1, bp * lp, -1), w_out, mp[2], g[1],
                      g[2], mp[3], mp[4], *ffn_args, mp[5], g[3], lp, tm, 1408)
        ys = _mix_ffn(ys, oa_s, ob_s, w_out, ms[2], g[1], g[2], ms[3], ms[4], *ffn_args, ms[5], g[3], ls, tm, 1408)

    return (yp.reshape(bp, lp, d), ys, outs["hgrn"], outs["k"], outs["v"], outs["gla"])
```

```python
import functools
import math

import jax
import jax.numpy as jnp
import numpy as np
from jax import lax
from jax.experimental import pallas as pl
from jax.experimental.pallas import tpu as pltpu

F32 = jnp.float32
BF16 = jnp.bfloat16
HIGHEST = lax.Precision.HIGHEST

D_MODEL = 1024
DEPTH = 2
GRID_W = 64
N_HEADS = 4
HEAD_W = 128
MIX_W = N_HEADS * HEAD_W
W_B = 512
HY_ORDER = 2
HY_EMB = 33
HY_BANDS = (HY_EMB - 1) // 2
HY_FF = 64
HY_TARGET = 1e-2
HY_FAST = 0.3
HY_SLOW = 1.5
DH_C = 64
DK_D = 64
GLA_RANK = 16
HEADS_PER_GROUP = 2
GLA_TAU = 16.0
ROPE_BASE = 10000.0
D_FF = 2816
EPS = 1e-6

LANES = 128
SUBLANES = 8
VMEM_LIMIT = 56 * 1024 * 1024
SCAN_CHUNK = 64
SCAN_ROWS = 256
SCAN_GROUP = 2
NT_DIMS = (((1,), (1,)), ((), ()))
LOG2E = 1.4426950408889634


def _params(*sem):
    return pltpu.CompilerParams(dimension_semantics=sem, vmem_limit_bytes=VMEM_LIMIT)


def _silu(x):
    return x * (1.0 / (1.0 + jnp.exp(-x)))


def _rms(x, g):
    return x * lax.rsqrt(jnp.mean(x * x, axis=-1, keepdims=True) + EPS) * g


def _ada_kernel(c_ref, w_ref, b_ref, o_ref, *, n_rows):
    s = _silu(c_ref[...])
    w = w_ref[...]
    rows = [jnp.sum(w * s[:, r:r + 1], axis=0, keepdims=True) for r in range(n_rows)]
    rows.append(jnp.zeros((SUBLANES - n_rows, w.shape[1]), F32))
    o_ref[...] = jnp.concatenate(rows, axis=0) + b_ref[...]


def _ada_mod(cvec_t, n_rows, ada_w, ada_b):
    n = ada_w.shape[-1]
    tn = 3072
    return pl.pallas_call(
        functools.partial(_ada_kernel, n_rows=n_rows),
        grid=(DEPTH, n // tn),
        in_specs=[pl.BlockSpec((D_MODEL, SUBLANES), lambda l, j: (0, 0)),
                  pl.BlockSpec((None, D_MODEL, tn), lambda l, j: (l, 0, j)),
                  pl.BlockSpec((None, 1, tn), lambda l, j: (l, 0, j))],
        out_specs=pl.BlockSpec((None, SUBLANES, tn), lambda l, j: (l, 0, j)),
        out_shape=jax.ShapeDtypeStruct((DEPTH, SUBLANES, n), F32),
        compiler_params=_params("arbitrary", "arbitrary"),
    )(cvec_t, ada_w, ada_b.reshape(DEPTH, 1, n))


def _normmod_matmul_kernel(x_ref, g_ref, sh_ref, sc_ref, w_ref, o_ref):
    h = _rms(x_ref[...], g_ref[...]) * (1.0 + sc_ref[...]) + sh_ref[...]
    o_ref[...] = jnp.dot(h.astype(BF16), w_ref[...], preferred_element_type=F32)


def _mod_index(n_mod):
    return (lambda b, i: (b, 0, 0)) if n_mod > 1 else (lambda b, i: (0, 0, 0))


def _normmod_matmul(x, g, shift, scale, w, tm):
    b, l, d = x.shape
    n = w.shape[1]
    mod_spec = pl.BlockSpec((None, 1, d), _mod_index(shift.shape[0]))
    return pl.pallas_call(
        _normmod_matmul_kernel,
        grid=(b, l // tm),
        in_specs=[pl.BlockSpec((None, tm, d), lambda b, i: (b, i, 0)),
                  pl.BlockSpec((1, d), lambda b, i: (0, 0)),
                  mod_spec, mod_spec,
                  pl.BlockSpec((d, n), lambda b, i: (0, 0))],
        out_specs=pl.BlockSpec((None, tm, n), lambda b, i: (b, i, 0)),
        out_shape=jax.ShapeDtypeStruct((b, l, n), F32),
        compiler_params=_params("arbitrary", "arbitrary"),
    )(x, g, shift, scale, w)


def _patch_rows(x, keep, starts):
    pieces, r = [], 0
    for s in starts:
        if s > r:
            pieces.append(x[r:s])
        pieces.append(jnp.where(keep[s:s + SUBLANES], x[s:s + SUBLANES], 0.0))
        r = s + SUBLANES
    if r < x.shape[0]:
        pieces.append(x[r:])
    return jnp.concatenate(pieces, axis=0)


def _mix_ffn_kernel(y_ref, yp_ref, yn_ref, a_ref, ap_ref, an_ref, b_ref, bp_ref, bn_ref, wo_ref, gate1_ref, g1_ref,
                    g2_ref, sh_ref, sc_ref, ua_ref, ug_ref, cwa_ref, cwg_ref, cba_ref, cbg_ref, dn_ref, gate2_ref,
                    g3_ref, o_ref, y1_scr, h_scr, acc_scr, *, seq_len):
    i = pl.program_id(1)
    f = pl.program_id(2)
    tm = y_ref.shape[0]
    halo = yp_ref.shape[0]
    half = a_ref.shape[1]

    @pl.when(f == 0)
    def _():
        rows_of = lambda p, m, n: jnp.concatenate([p[...], m[...], n[...]], axis=0)
        a_all = rows_of(ap_ref, a_ref, an_ref).astype(BF16)
        b_all = rows_of(bp_ref, b_ref, bn_ref).astype(BF16)
        m = jnp.dot(a_all, wo_ref[:half, :], preferred_element_type=F32)
        m = m + jnp.dot(b_all, wo_ref[half:, :], preferred_element_type=F32)
        y1 = rows_of(yp_ref, y_ref, yn_ref) + gate1_ref[...] * _rms(m, g1_ref[...])
        y1_scr[...] = y1[halo:halo + tm]
        h_scr[...] = (_rms(y1, g2_ref[...]) * (1.0 + sc_ref[...]) + sh_ref[...]).astype(BF16)
        acc_scr[...] = jnp.zeros_like(acc_scr)

    rows = tm + 2 * halo
    pos = (i * tm + lax.broadcasted_iota(jnp.int32, (tm, 1), 0)) % seq_len
    has_prev = pos != 0
    has_next = pos != seq_len - 1
    period = math.gcd(tm, seq_len)
    first_groups = list(range(0, tm, period))
    last_groups = [s + period - SUBLANES for s in first_groups]

    def conv(u_ref, cw_ref, cb_ref):
        u = jnp.dot(h_scr[...], u_ref[...], preferred_element_type=F32)
        up = _patch_rows(pltpu.roll(u, 1, 0)[halo:halo + tm], has_prev, first_groups)
        un = _patch_rows(pltpu.roll(u, rows - 1, 0)[halo:halo + tm], has_next, last_groups)
        uc = u[halo:halo + tm]
        return up * cw_ref[0:1, :] + uc * cw_ref[1:2, :] + un * cw_ref[2:3, :] + cb_ref[...]

    a = conv(ua_ref, cwa_ref, cba_ref)
    gt = conv(ug_ref, cwg_ref, cbg_ref)
    act = (_silu(gt) * a).astype(BF16)
    acc_scr[...] += jnp.dot(act, dn_ref[...], preferred_element_type=F32)

    @pl.when(f == pl.num_programs(2) - 1)
    def _():
        o_ref[...] = y1_scr[...] + gate2_ref[...] * _rms(acc_scr[...], g3_ref[...])


def _mix_ffn(y, a, bm, w_out, gate1, g1, g2, shift, scale, up, cw, cb, down, layer, gate2, g3, seq_len, tm, tf):
    b, l, d = y.shape
    wa = a.shape[-1]
    nf = D_FF // tf
    halo = SUBLANES
    hb = tm // halo
    last_hb = l // halo - 1
    n_mod = shift.shape[0]
    mod_spec = pl.BlockSpec((None, 1, d), (lambda b, i, f: (b, 0, 0)) if n_mod > 1 else (lambda b, i, f: (0, 0, 0)))
    vec = lambda off: pl.BlockSpec((None, 1, tf), lambda b, i, f: (layer, 0, off + f))
    row_d = pl.BlockSpec((1, d), lambda b, i, f: (0, 0))

    def tiles(w):
        return [pl.BlockSpec((None, tm, w), lambda b, i, f: (b, i, 0)),
                pl.BlockSpec((None, halo, w), lambda b, i, f: (b, jnp.maximum(i * hb - 1, 0), 0)),
                pl.BlockSpec((None, halo, w), lambda b, i, f: (b, jnp.minimum((i + 1) * hb, last_hb), 0))]

    kern = functools.partial(_mix_ffn_kernel, seq_len=seq_len)
    return pl.pallas_call(
        kern,
        grid=(b, l // tm, nf),
        in_specs=tiles(d) + tiles(wa) + tiles(wa) + [
                  pl.BlockSpec((2 * wa, d), lambda b, i, f: (0, 0)),
                  mod_spec, row_d,
                  row_d, mod_spec, mod_spec,
                  pl.BlockSpec((None, d, tf), lambda b, i, f: (layer, 0, f)),
                  pl.BlockSpec((None, d, tf), lambda b, i, f: (layer, 0, nf + f)),
                  pl.BlockSpec((None, 3, tf), lambda b, i, f: (layer, 0, f)),
                  pl.BlockSpec((None, 3, tf), lambda b, i, f: (layer, 0, nf + f)),
                  vec(0), vec(nf),
                  pl.BlockSpec((None, tf, d), lambda b, i, f: (layer, f, 0)),
                  mod_spec, row_d],
        out_specs=pl.BlockSpec((None, tm, d), lambda b, i, f: (b, i, 0)),
        out_shape=jax.ShapeDtypeStruct((b, l, d), F32),
        scratch_shapes=[pltpu.VMEM((tm, d), F32), pltpu.VMEM((tm + 2 * halo, d), BF16), pltpu.VMEM((tm, d), F32)],
        compiler_params=_params("arbitrary", "arbitrary", "arbitrary"),
    )(y, y, y, a, a, a, bm, bm, bm, w_out, gate1, g1, g2, shift, scale, up, up, cw, cw, cb, cb, down, gate2, g3)


def _scan_tables(c, heads_per_group):
    nlev = int(math.log2(c))
    t = np.arange(c)[:, None]
    r = np.arange(c)[None, :]
    masks_f = [np.eye(c, dtype=bool)]
    for lev in range(1, nlev + 1):
        bsz = 2 ** lev
        mid = (t // bsz) * bsz + bsz // 2
        masks_f.append(((t // bsz) == (r // bsz)) & (t >= mid) & (r < mid))
    m_f = np.stack(masks_f).astype(np.float32)
    m_b = np.transpose(m_f, (0, 2, 1))
    tri = lambda a: jnp.asarray(np.concatenate([a, a, a], axis=1).astype(np.float32), dtype=BF16)
    rep = lambda m: jnp.asarray(np.concatenate([m] * heads_per_group, axis=2))
    return tri(r <= t), tri(r >= t), rep(m_f), rep(m_b)


def _level_decay(cum, ncum, lg2, lev, bwd):
    c, w = cum.shape
    b = 1 << lev
    half = b // 2
    off = half - 1 + int(bwd)
    if b == 2:
        odd = lax.broadcasted_iota(jnp.int32, (c, 1), 0) % 2 == 1
        return jnp.where(odd != bwd, lg2, 0.0)
    if b >= 2 * SUBLANES:
        pieces = []
        for b0 in range(0, c, b):
            for rows, is_upper in ((slice(b0, b0 + half), False), (slice(b0 + half, b0 + b), True)):
                src = cum if is_upper != bwd else ncum
                pieces.append(src[rows] - jnp.broadcast_to(src[b0 + off:b0 + off + 1], (half, w)))
        return jnp.concatenate(pieces, axis=0)
    cum3 = cum.reshape(c // SUBLANES, SUBLANES, w)
    sub = lax.broadcasted_iota(jnp.int32, (1, SUBLANES, 1), 1)
    if b == SUBLANES:
        ref3 = jnp.broadcast_to(cum3[:, off:off + 1], cum3.shape)
    else:
        ref3 = jnp.where(sub < b, cum3[:, off:off + 1], cum3[:, b + off:b + off + 1])
    upper = (sub % b) >= half
    sgn = jnp.where(upper != bwd, 1.0, -1.0)
    return ((cum3 - ref3) * sgn).reshape(c, w)


def _head_stack(xb, width):
    n = xb.shape[1] // width
    lane = lax.broadcasted_iota(jnp.int32, (1, xb.shape[1]), 1)
    zero = jnp.zeros_like(xb)
    return jnp.concatenate([jnp.where((lane >= j * width) & (lane < (j + 1) * width), xb, zero) for j in range(n)],
                           axis=0)


def _scan_group(chunks):
    dk = chunks[0][0].shape[1] // N_HEADS
    hpg = HEADS_PER_GROUP
    gw = hpg * dk
    ngroups = N_HEADS // hpg
    vw = hpg * HEAD_W
    work = []
    for q, k, v, lg2, tri_ref, m_ref, st_ref, bwd in chunks:
        hi = lg2.astype(BF16)
        r1 = lg2 - hi.astype(F32)
        mid = r1.astype(BF16)
        lo = (r1 - mid.astype(F32)).astype(BF16)
        cum = jnp.dot(tri_ref[...], jnp.concatenate([hi, mid, lo], axis=0), preferred_element_type=F32)
        work.append(dict(q=q, k=k, v=v, lg2=lg2, cum=cum, ncum=-cum, m_ref=m_ref, st_ref=st_ref, bwd=bwd,
                         q16=q.astype(BF16), k16=k.astype(BF16), att=[None] * ngroups))
    nlev = chunks[0][5].shape[0] - 1
    for lev in range(nlev + 1):
        for w in work:
            if lev == 0:
                qb, kb = w["q16"], w["k16"]
            else:
                e = jnp.exp2(_level_decay(w["cum"], w["ncum"], w["lg2"], lev, w["bwd"])).astype(BF16)
                qb, kb = w["q16"] * e, w["k16"] * e
            for g in range(ngroups):
                sl = slice(g * gw, (g + 1) * gw)
                prod = lax.dot_general(qb[:, sl], _head_stack(kb[:, sl], dk), NT_DIMS,
                                       preferred_element_type=F32)
                term = w["m_ref"][lev] * prod
                w["att"][g] = term if w["att"][g] is None else w["att"][g] + term
    lane = lax.broadcasted_iota(jnp.int32, (1, gw), 1)
    results = []
    for w in work:
        q, k, v, cum, st_ref = w["q"], w["k"], w["v"], w["cum"], w["st_ref"]
        c = q.shape[0]
        last = 0 if w["bwd"] else c - 1
        e_cum = jnp.exp2(cum)
        d_last = e_cum[last:last + 1, :]
        qe = (q * e_cum).astype(BF16)
        kd = (k * jnp.exp2(cum[last:last + 1, :] - cum)).astype(BF16)
        vb = v.astype(BF16)
        outs = []
        for g in range(ngroups):
            sl = slice(g * gw, (g + 1) * gw)
            st = st_ref[g]
            stb = st.astype(BF16)
            o_g = jnp.dot(w["att"][g].astype(BF16), _head_stack(vb[:, g * vw:(g + 1) * vw], HEAD_W),
                          preferred_element_type=F32)
            qe_g = qe[:, sl]
            zero = jnp.zeros_like(qe_g)
            inter = [lax.dot_general(jnp.where((lane >= j * dk) & (lane < (j + 1) * dk), qe_g, zero), stb, NT_DIMS,
                                     preferred_element_type=F32) for j in range(hpg)]
            outs.append(o_g + jnp.concatenate(inter, axis=1))
            vstack = jnp.concatenate([v[:, g * vw + j * HEAD_W:g * vw + (j + 1) * HEAD_W] for j in range(hpg)],
                                     axis=0)
            st_ref[g] = st * d_last[:, sl] + jnp.dot(vstack.T.astype(BF16), _head_stack(kd[:, sl], dk),
                                                      preferred_element_type=F32)
        results.append(jnp.concatenate(outs, axis=1))
    return results


def _bidir_scan_body(load_fwd, load_bwd, g_ref, s0_ref, norm_ref, af_ref, ab_ref, mf_ref, mb_ref,
                     o_ref, s_out_ref, of_scr, ob_scr, stf_scr, stb_scr, seq_len):
    c = SCAN_CHUNK
    j = pl.program_id(1)
    nblk = pl.num_programs(1)
    n = SCAN_ROWS // c
    ngroups, _, gw = stf_scr.shape
    hpg = N_HEADS // ngroups
    dk = gw // hpg

    @pl.when(j == 0)
    def _():
        for g in range(ngroups):
            if s0_ref is None:
                stf_scr[g] = jnp.zeros((HEAD_W, gw), F32)
                stb_scr[g] = jnp.zeros((HEAD_W, gw), F32)
            else:
                stf_scr[g] = jnp.concatenate([s0_ref[0, g * hpg + j].T for j in range(hpg)], axis=1)
                stb_scr[g] = jnp.concatenate([s0_ref[1, g * hpg + j].T for j in range(hpg)], axis=1)

    base_f = j * SCAN_ROWS
    base_b = (nblk - 1 - j) * SCAN_ROWS

    def step(i, carry):
        chunks, stores = [], []
        for u in range(SCAN_GROUP):
            rf = pl.multiple_of((i * SCAN_GROUP + u) * c, c)
            chunks.append(load_fwd(rf) + (af_ref, mf_ref, stf_scr, False))
            stores.append((of_scr, pl.multiple_of(base_f + rf, c)))
        for u in range(SCAN_GROUP):
            rb = pl.multiple_of((n - 1 - i * SCAN_GROUP - u) * c, c)
            chunks.append(load_bwd(rb) + (ab_ref, mb_ref, stb_scr, True))
            stores.append((ob_scr, pl.multiple_of(base_b + rb, c)))
        for (scr, r0), o in zip(stores, _scan_group(chunks)):
            scr[pl.ds(r0, c), :] = o
        return carry

    lax.fori_loop(0, n // SCAN_GROUP, step, 0)

    @pl.when(j == nblk - 1)
    def _():
        def fin(jj, carry):
            r0 = pl.multiple_of(jj * SCAN_ROWS, SCAN_ROWS)
            o = of_scr[pl.ds(r0, SCAN_ROWS), :] + ob_scr[pl.ds(r0, SCAN_ROWS), :]
            parts = []
            for h in range(N_HEADS):
                sl = slice(h * HEAD_W, (h + 1) * HEAD_W)
                parts.append(_rms(o[:, sl], norm_ref[:, sl]))
            o_ref[pl.ds(r0, SCAN_ROWS), :] = jnp.concatenate(parts, axis=1) * _silu(g_ref[pl.ds(r0, SCAN_ROWS), :])
            return carry

        lax.fori_loop(0, seq_len // SCAN_ROWS, fin, 0)
        for h in range(N_HEADS):
            g, hs = h // hpg, slice((h % hpg) * dk, (h % hpg + 1) * dk)
            s_out_ref[0, h] = stf_scr[g][:, hs].T
            s_out_ref[1, h] = stb_scr[g][:, hs].T


def _hgrn_kernel(*refs, layer, has_s0, seq_len):
    qf_ref, ff_ref, if_ref, qb_ref, fb_ref, ib_ref, g_ref, lbraw_ref, norm_ref, af_ref, ab_ref, mf_ref, mb_ref = refs[:13]
    s0_ref = refs[13] if has_s0 else None
    o_ref, s_out_ref, of_scr, ob_scr, stf_scr, stb_scr = refs[13 + has_s0:]
    c = SCAN_CHUNK
    raw = lbraw_ref[...]
    ex = jnp.exp(raw - jnp.max(raw, axis=0, keepdims=True))
    sm = ex / jnp.sum(ex, axis=0, keepdims=True)
    lb = sm[0]
    for j in range(1, layer + 1):
        lb = lb + sm[j]

    def load(q_ref, f_ref, i_ref, lb_row):
        def fn(r0):
            q = _silu(q_ref[pl.ds(r0, c), :]) * (HEAD_W ** -0.5)
            sig = 1.0 / (1.0 + jnp.exp(-f_ref[pl.ds(r0, c), :]))
            f = lb_row + (1.0 - lb_row) * sig
            return q, 1.0 - f, i_ref[pl.ds(r0, c), :], jnp.log2(f)
        return fn

    _bidir_scan_body(load(qf_ref, ff_ref, if_ref, lb[0:1]), load(qb_ref, fb_ref, ib_ref, lb[1:2]), g_ref,
                     s0_ref, norm_ref, af_ref, ab_ref, mf_ref, mb_ref,
                     o_ref, s_out_ref, of_scr, ob_scr, stf_scr, stb_scr, seq_len)


def _gla_kernel(*refs, has_s0, seq_len):
    fwd_refs, bwd_refs = refs[0:4], refs[4:8]
    g_ref, aw_ref, ab_ref, norm_ref, af_ref, abk_ref, mf_ref, mb_ref = refs[8:16]
    s0_ref = refs[16] if has_s0 else None
    o_ref, s_out_ref, of_scr, ob_scr, stf_scr, stb_scr = refs[16 + has_s0:]
    c = SCAN_CHUNK

    def load(d, q_ref, k_ref, v_ref, da_ref):
        def fn(r0):
            q = q_ref[pl.ds(r0, c), :] * (DK_D ** -0.5)
            da = da_ref[pl.ds(r0, c), :]
            da_hi = da.astype(BF16)
            da_mid = (da - da_hi.astype(F32)).astype(BF16)
            xa = jnp.dot(jnp.concatenate([da_hi, da_hi, da_mid], axis=1), aw_ref[d],
                         preferred_element_type=F32) + ab_ref[d]
            la = (jnp.minimum(xa, 0.0) - jnp.log(1.0 + jnp.exp(-jnp.abs(xa)))) * (LOG2E / GLA_TAU)
            return q, k_ref[pl.ds(r0, c), :], v_ref[pl.ds(r0, c), :], la
        return fn

    _bidir_scan_body(load(0, *fwd_refs), load(1, *bwd_refs), g_ref,
                     s0_ref, norm_ref, af_ref, abk_ref, mf_ref, mb_ref,
                     o_ref, s_out_ref, of_scr, ob_scr, stf_scr, stb_scr, seq_len)


def _scan_call(kern, proj, streams, extra, s0, seq_len, dk):
    b = proj.shape[0]
    nblk = seq_len // SCAN_ROWS
    hpg = HEADS_PER_GROUP
    tabs = _scan_tables(SCAN_CHUNK, hpg)
    full = lambda a: pl.BlockSpec(a.shape, lambda i, j, _n=a.ndim: (0,) * _n)
    in_specs = []
    for cb, w, kind in streams:
        if kind == "f":
            in_specs.append(pl.BlockSpec((None, SCAN_ROWS, w), lambda i, j, _c=cb: (i, j, _c)))
        elif kind == "b":
            in_specs.append(pl.BlockSpec((None, SCAN_ROWS, w), lambda i, j, _c=cb: (i, nblk - 1 - j, _c)))
        else:
            in_specs.append(pl.BlockSpec((None, seq_len, w), lambda i, j, _c=cb: (i, 0, _c)))
    args = [proj] * len(streams)
    for a in tuple(extra) + tabs:
        in_specs.append(full(a))
        args.append(a)
    if s0 is not None:
        in_specs.append(pl.BlockSpec((None, 2, N_HEADS, dk, HEAD_W), lambda i, j: (i, 0, 0, 0, 0)))
        args.append(s0)
    return pl.pallas_call(
        kern,
        grid=(b, nblk),
        in_specs=in_specs,
        out_specs=[pl.BlockSpec((None, seq_len, MIX_W), lambda i, j: (i, 0, 0)),
                   pl.BlockSpec((None, None, 2, N_HEADS, dk, HEAD_W), lambda i, j: (i, 0, 0, 0, 0, 0))],
        out_shape=[jax.ShapeDtypeStruct((b, seq_len, MIX_W), F32),
                   jax.ShapeDtypeStruct((b, 1, 2, N_HEADS, dk, HEAD_W), F32)],
        scratch_shapes=[pltpu.VMEM((seq_len, MIX_W), F32), pltpu.VMEM((seq_len, MIX_W), F32),
                        pltpu.VMEM((N_HEADS // hpg, HEAD_W, hpg * dk), F32),
                        pltpu.VMEM((N_HEADS // hpg, HEAD_W, hpg * dk), F32)],
        compiler_params=_params("arbitrary", "arbitrary"),
    )(*args)


def _hgrn_mixer(proj, hgrn_lb, norm, s0, layer):
    seq_len = proj.shape[1]
    kern = functools.partial(_hgrn_kernel, layer=layer, has_s0=s0 is not None, seq_len=seq_len)
    streams = [(0, MIX_W, "f"), (1, MIX_W, "f"), (3, MIX_W, "f"),
               (0, MIX_W, "b"), (2, MIX_W, "b"), (3, MIX_W, "b"), (4, MIX_W, "w")]
    return _scan_call(kern, proj, streams, (hgrn_lb, norm.reshape(1, MIX_W)), s0, seq_len, HEAD_W)


def _gla_mixer(proj, aw, ab, norm, s0):
    seq_len = proj.shape[1]
    kern = functools.partial(_gla_kernel, has_s0=s0 is not None, seq_len=seq_len)
    kw = N_HEADS * DK_D
    streams = [(1536 // kw, kw, "f"), (1792 // kw, kw, "f"), (2048 // MIX_W, MIX_W, "f"), (3072 // LANES, LANES, "f"),
               (1536 // kw, kw, "b"), (1792 // kw, kw, "b"), (2048 // MIX_W, MIX_W, "b"), (3072 // LANES, LANES, "b"),
               (2560 // MIX_W, MIX_W, "w")]
    return _scan_call(kern, proj, streams, (aw, ab, norm.reshape(1, MIX_W)), s0, seq_len, DK_D)


def _dwconv3_rows(x, w_ref, b_ref):
    l = x.shape[0]
    row = lax.broadcasted_iota(jnp.int32, (l, 1), 0)
    xp = _patch_rows(pltpu.roll(x, 1, 0), row != 0, [0])
    xn = _patch_rows(pltpu.roll(x, l - 1, 0), row != l - 1, [l - SUBLANES])
    return xp * w_ref[0:1, :] + x * w_ref[1:2, :] + xn * w_ref[2:3, :] + b_ref[...]


def _hyena_filter_kernel(z_ref, w1_ref, b1_ref, w2_ref, b2_ref, w3_ref, fr_ref, dl_ref, o_ref):
    z = z_ref[...]
    fr = fr_ref[...]
    h = jnp.sin(fr * (jnp.dot(z, w1_ref[...], preferred_element_type=F32, precision=HIGHEST) + b1_ref[...]))
    h = jnp.sin(fr * (jnp.dot(h, w2_ref[...], preferred_element_type=F32, precision=HIGHEST) + b2_ref[...]))
    h_hi = h.astype(BF16)
    h_mid = (h - h_hi.astype(F32)).astype(BF16)
    h = jnp.dot(jnp.concatenate([h_hi, h_hi, h_mid], axis=1), w3_ref[...],
                preferred_element_type=F32)
    win = jnp.exp(-z[:, 0:1] * dl_ref[...])
    o_ref[...] = h * jnp.concatenate([win] * (2 * HY_ORDER), axis=1)


def _hyena_pos_features(l):
    t = jnp.linspace(0.0, 1.0, l, dtype=F32)[:, None]
    w = 2.0 * math.pi * jnp.arange(l, dtype=F32)[:, None] / l
    fb = jnp.linspace(1e-4, HY_BANDS - 1, HY_BANDS, dtype=F32)[None]
    z = jnp.concatenate([t, jnp.cos(fb * w), -jnp.sin(fb * w)], axis=-1)
    return jnp.pad(z, ((0, 0), (0, HY_FF - HY_EMB)))


def _hyena_filters(lens, w1, b1, w2, b2, w3, freq):
    z = jnp.concatenate([_hyena_pos_features(l) for l in lens], axis=0)
    rows = z.shape[0]
    tr = 256
    w1p = jnp.pad(w1, ((0, HY_FF - HY_EMB), (0, 0)))
    max_decay = math.log(HY_TARGET) / HY_FAST
    min_decay = math.log(HY_TARGET) / HY_SLOW
    deltas = jnp.abs(jnp.linspace(min_decay, max_decay, W_B, dtype=F32))[None]
    nout = w3.shape[1]
    w3_hi = w3.astype(BF16)
    w3_mid = (w3 - w3_hi.astype(F32)).astype(BF16)
    w3s = jnp.concatenate([w3_hi, w3_mid, w3_hi], axis=0)
    full = lambda a: pl.BlockSpec(a.shape, lambda i, _n=a.ndim: (0,) * _n)
    ins = (w1p, b1.reshape(1, -1), w2, b2.reshape(1, -1), w3s, freq.reshape(1, -1), deltas)
    return pl.pallas_call(
        _hyena_filter_kernel,
        grid=(rows // tr,),
        in_specs=[pl.BlockSpec((tr, HY_FF), lambda i: (i, 0))] + [full(a) for a in ins],
        out_specs=pl.BlockSpec((tr, nout), lambda i: (i, 0)),
        out_shape=jax.ShapeDtypeStruct((rows, nout), F32),
        compiler_params=_params("arbitrary"),
    )(z, *ins)


def _dft_table_kernel(ca_ref, sa_ref, cb_ref, sb_ref, cos_ref, sinf_ref, sini_ref):
    tk = ca_ref.shape[0]
    ca, sa = ca_ref[...], sa_ref[...]
    row = pl.program_id(0) * tk + lax.broadcasted_iota(jnp.int32, (tk, 1), 0)
    lane = lax.broadcasted_iota(jnp.int32, (1, LANES), 1)
    alt_row = jnp.where(row % 2 == 0, 1.0, -1.0)
    for grp in range(cb_ref.shape[1]):
        cols = slice(grp * LANES, (grp + 1) * LANES)
        cbg, sbg = cb_ref[:, grp:grp + 1], sb_ref[:, grp:grp + 1]
        sin_t = sa * cbg + ca * sbg
        col = grp * LANES + lane
        cos_ref[:, cols] = (ca * cbg - sa * sbg).astype(BF16)
        sinf_ref[:, cols] = jnp.where(row == 0, jnp.where(col % 2 == 0, 1.0, -1.0), sin_t).astype(BF16)
        sini_ref[:, cols] = jnp.where(col == 0, alt_row, sin_t).astype(BF16)


def _dft_tables(l):
    n = 2 * l
    k = jnp.arange(l, dtype=jnp.int32)[:, None]
    t1 = jnp.arange(LANES, dtype=jnp.int32)[None, :]
    t2 = (jnp.arange(l // LANES, dtype=jnp.int32) * LANES)[None, :]
    ang = lambda m: (m % n).astype(F32) * (2.0 * math.pi / n)
    small = (jnp.cos(ang(k * t1)), jnp.sin(ang(k * t1)), jnp.cos(ang(k * t2)), jnp.sin(ang(k * t2)))
    tk = min(l, 256)
    out = jax.ShapeDtypeStruct((l, l), BF16)
    ospec = pl.BlockSpec((tk, l), lambda i: (i, 0))
    return pl.pallas_call(
        _dft_table_kernel,
        grid=(l // tk,),
        in_specs=[pl.BlockSpec((tk, a.shape[1]), lambda i: (i, 0)) for a in small],
        out_specs=[ospec, ospec, ospec],
        out_shape=[out, out, out],
        compiler_params=_params("arbitrary"),
    )(*small)


def _hyena_spectrum_kernel(c_ref, s_ref, f0_ref, f1_ref, kr_ref, kia_ref, krb_ref, *, seq_len):
    kt = pl.program_id(1)
    tk = c_ref.shape[0]
    row = lax.broadcasted_iota(jnp.int32, (seq_len, 1), 0)
    f0 = f0_ref[...]
    f1 = jnp.where(row != 0, f1_ref[...], 0.0)
    fsum = f0 + f1
    kr = jnp.dot(c_ref[...], fsum.astype(BF16), preferred_element_type=F32)
    ki = jnp.dot(s_ref[...], (f1 - f0).astype(BF16), preferred_element_type=F32)
    nyq = jnp.sum(jnp.where(row % 2 == 0, fsum, -fsum), axis=0, keepdims=True)
    krow = kt * tk + lax.broadcasted_iota(jnp.int32, (tk, 1), 0)
    dc = krow == 0
    wk = jnp.where(dc, 1.0, 2.0) * (1.0 / (2 * seq_len))
    kr_ref[...] = kr * wk
    kia_ref[...] = jnp.where(dc, 0.0, ki) * wk
    krb_ref[...] = jnp.where(dc, nyq, kr) * wk


def _hyena_spectrum(filt, row0, seq_len, tables, tk):
    cos_t, sin_f, _ = tables
    rb = row0 // seq_len
    out = jax.ShapeDtypeStruct((HY_ORDER, seq_len, W_B), F32)
    kern = functools.partial(_hyena_spectrum_kernel, seq_len=seq_len)
    ospec = pl.BlockSpec((None, tk, W_B), lambda o, kt: (o, kt, 0))
    return pl.pallas_call(
        kern,
        grid=(HY_ORDER, seq_len // tk),
        in_specs=[pl.BlockSpec((tk, seq_len), lambda o, kt: (kt, 0)),
                  pl.BlockSpec((tk, seq_len), lambda o, kt: (kt, 0)),
                  pl.BlockSpec((seq_len, W_B), lambda o, kt: (rb, 2 * o)),
                  pl.BlockSpec((seq_len, W_B), lambda o, kt: (rb, 2 * o + 1))],
        out_specs=[ospec, ospec, ospec],
        out_shape=[out, out, out],
        compiler_params=_params("arbitrary", "arbitrary"),
    )(cos_t, sin_f, filt, filt)


def _hyena_order_kernel(zin_ref, gate_ref, cwz_ref, cbz_ref, cwg_ref, cbg_ref, d_ref, kr_ref, kia_ref, krb_ref,
                        cf_ref, sf_ref, ci_ref, si_ref, o_ref, z_scr, zb_scr, acc_scr, *, conv_input):
    kt = pl.program_id(1)

    @pl.when(kt == 0)
    def _():
        z = zin_ref[...]
        if conv_input:
            z = _dwconv3_rows(z, cwz_ref, cbz_ref)
        z_scr[...] = z
        zb_scr[...] = z.astype(BF16)
        acc_scr[...] = jnp.zeros_like(acc_scr)

    zb = zb_scr[...]
    p = jnp.dot(cf_ref[...], zb, preferred_element_type=F32)
    q = jnp.dot(sf_ref[...], zb, preferred_element_type=F32)
    kia = kia_ref[...]
    yr = (p * kr_ref[...] + q * kia).astype(BF16)
    yi = (q * krb_ref[...] - p * kia).astype(BF16)
    acc_scr[...] += (jnp.dot(ci_ref[...], yr, preferred_element_type=F32)
                     + jnp.dot(si_ref[...], yi, preferred_element_type=F32))

    @pl.when(kt == pl.num_programs(1) - 1)
    def _():
        gate = _dwconv3_rows(gate_ref[...], cwg_ref, cbg_ref)
        o_ref[...] = gate * (acc_scr[...] + z_scr[...] * d_ref[...])


def _hyena_order(zin, zin_col, proj, order, conv_w, conv_b, hy_d, spectrum, tables, tk):
    b, seq_len = proj.shape[0], proj.shape[1]
    cos_t, sin_f, sin_i = tables
    kr, kia, krb = spectrum
    hy0 = 5
    conv_input = order == 0
    cw = conv_w.reshape(3, 1 + HY_ORDER, W_B).transpose(1, 0, 2)
    cbias = conv_b.reshape(1 + HY_ORDER, 1, W_B)
    kern = functools.partial(_hyena_order_kernel, conv_input=conv_input)
    kspec = pl.BlockSpec((None, tk, W_B), lambda i, kt: (order, kt, 0))
    return pl.pallas_call(
        kern,
        grid=(b, seq_len // tk),
        in_specs=[pl.BlockSpec((None, seq_len, W_B), lambda i, kt: (i, 0, zin_col)),
                  pl.BlockSpec((None, seq_len, W_B), lambda i, kt: (i, 0, hy0 + 1 + order)),
                  pl.BlockSpec((None, 3, W_B), lambda i, kt: (0, 0, 0)),
                  pl.BlockSpec((None, 1, W_B), lambda i, kt: (0, 0, 0)),
                  pl.BlockSpec((None, 3, W_B), lambda i, kt: (1 + order, 0, 0)),
                  pl.BlockSpec((None, 1, W_B), lambda i, kt: (1 + order, 0, 0)),
                  pl.BlockSpec((None, 1, W_B), lambda i, kt: (order, 0, 0)),
                  kspec, kspec, kspec,
                  pl.BlockSpec((tk, seq_len), lambda i, kt: (kt, 0)),
                  pl.BlockSpec((tk, seq_len), lambda i, kt: (kt, 0)),
                  pl.BlockSpec((seq_len, tk), lambda i, kt: (0, kt)),
                  pl.BlockSpec((seq_len, tk), lambda i, kt: (0, kt))],
        out_specs=pl.BlockSpec((None, seq_len, W_B), lambda i, kt: (i, 0, 0)),
        out_shape=jax.ShapeDtypeStruct((b, seq_len, W_B), F32),
        scratch_shapes=[pltpu.VMEM((seq_len, W_B), F32), pltpu.VMEM((seq_len, W_B), BF16),
                        pltpu.VMEM((seq_len, W_B), F32)],
        compiler_params=_params("arbitrary", "arbitrary"),
    )(zin, proj, cw, cbias, cw, cbias, hy_d.reshape(HY_ORDER, 1, W_B), kr, kia, krb, cos_t, sin_f, cos_t, sin_i)


def _hyena_short_kernel(v_ref, x1_ref, x2_ref, cw_ref, cb_ref, d_ref, kr_ref, kia_ref, krb_ref,
                        cf_ref, sf_ref, si_ref, o_ref):
    cf, sf, si = cf_ref[...], sf_ref[...], si_ref[...]
    nb = v_ref.shape[0]
    zs = [_dwconv3_rows(v_ref[i], cw_ref.at[0], cb_ref.at[0]) for i in range(nb)]
    for order, gate_ref in enumerate((x1_ref, x2_ref)):
        zb = [z.astype(BF16) for z in zs]
        ps = [jnp.dot(cf, z, preferred_element_type=F32) for z in zb]
        qs = [jnp.dot(sf, z, preferred_element_type=F32) for z in zb]
        kia = kia_ref[order]
        yr = [(p * kr_ref[order] + q * kia).astype(BF16) for p, q in zip(ps, qs)]
        yi = [(q * krb_ref[order] - p * kia).astype(BF16) for p, q in zip(ps, qs)]
        conv = [jnp.dot(cf, r, preferred_element_type=F32) + jnp.dot(si, m, preferred_element_type=F32)
                for r, m in zip(yr, yi)]
        gates = [_dwconv3_rows(gate_ref[i], cw_ref.at[1 + order], cb_ref.at[1 + order]) for i in range(nb)]
        zs = [g * (c + z * d_ref[order]) for g, c, z in zip(gates, conv, zs)]
    for i in range(nb):
        o_ref[i] = zs[i]


def _hyena_short(proj, conv_w, conv_b, hy_d, spectrum, tables):
    b, seq_len = proj.shape[0], proj.shape[1]
    cos_t, sin_f, sin_i = tables
    nb = 2 if b % 2 == 0 else 1
    hy0 = 5
    cw = conv_w.reshape(3, 1 + HY_ORDER, W_B).transpose(1, 0, 2)
    cbias = conv_b.reshape(1 + HY_ORDER, 1, W_B)
    full = lambda a: pl.BlockSpec(a.shape, lambda i, _n=a.ndim: (0,) * _n)
    col = lambda j: pl.BlockSpec((nb, seq_len, W_B), lambda i: (i, 0, hy0 + j))
    consts = (cw, cbias, hy_d.reshape(HY_ORDER, 1, W_B)) + tuple(spectrum) + (cos_t, sin_f, sin_i)
    return pl.pallas_call(
        _hyena_short_kernel,
        grid=(b // nb,),
        in_specs=[col(0), col(1), col(2)] + [full(a) for a in consts],
        out_specs=pl.BlockSpec((nb, seq_len, W_B), lambda i: (i, 0, 0)),
        out_shape=jax.ShapeDtypeStruct((b, seq_len, W_B), F32),
        compiler_params=_params("arbitrary"),
    )(proj, proj, proj, *consts)


def _hyena_mixer(proj, conv_w, conv_b, hy_d, spectrum, tables, tk):
    if tk == proj.shape[1]:
        return _hyena_short(proj, conv_w, conv_b, hy_d, spectrum, tables)
    z = _hyena_order(proj, 5, proj, 0, conv_w, conv_b, hy_d, spectrum, tables, tk)
    return _hyena_order(z, 0, proj, 1, conv_w, conv_b, hy_d, spectrum, tables, tk)


def _diff_lambda(lp_ref, lam_init):
    lp = lp_ref[...]
    a = jnp.sum(lp[0:1] * lp[1:2], axis=-1, keepdims=True)
    b = jnp.sum(lp[2:3] * lp[3:4], axis=-1, keepdims=True)
    return jnp.exp(a) - jnp.exp(b) + lam_init


def _diff_attend(q, keys_b, vals_b, lam, stack):
    tq = q.shape[0]
    lane = lax.broadcasted_iota(jnp.int32, (1, 2 * DH_C), 1)
    qs = q * (DH_C ** -0.5 * LOG2E)

    def attend(qm):
        s = lax.dot_general(qm.astype(BF16), keys_b, NT_DIMS, preferred_element_type=F32)
        e = jnp.exp2(s - jnp.max(s, axis=-1, keepdims=True))
        den = jnp.sum(e, axis=-1, keepdims=True)
        return jnp.dot(e.astype(BF16), vals_b, preferred_element_type=F32) / den

    q_first, q_second = jnp.where(lane < DH_C, qs, 0.0), jnp.where(lane >= DH_C, qs, 0.0)
    if stack:
        o = attend(jnp.concatenate([q_first, q_second], axis=0))
        return o[:tq] - lam * o[tq:]
    return attend(q_first) - lam * attend(q_second)


def _attn_prompt_kernel(q_ref, k_ref, v_ref, lp_ref, norm_ref, o_ref, kc_ref, vc_ref, *, lam_init):
    hw = 2 * DH_C
    lam = _diff_lambda(lp_ref, lam_init)
    for h in range(N_HEADS):
        sl = slice(h * hw, (h + 1) * hw)
        k = k_ref[:, sl]
        v = v_ref[:, sl]
        o = _diff_attend(q_ref[:, sl], k.astype(BF16), v.astype(BF16), lam, stack=True)
        o_ref[:, sl] = _rms(o, norm_ref[:, sl]) * (1.0 - lam_init)
        kc_ref[h] = k
        vc_ref[h] = v


def _attn_prompt(proj, diff_lambda, diff_norm, lam_init):
    b, seq_len = proj.shape[0], proj.shape[1]
    hw = 2 * DH_C
    w = N_HEADS * hw
    kern = functools.partial(_attn_prompt_kernel, lam_init=lam_init)
    col = lambda j: pl.BlockSpec((None, seq_len, w), lambda i: (i, 0, j))
    cache_spec = pl.BlockSpec((None, None, N_HEADS, seq_len, hw), lambda i: (i, 0, 0, 0, 0))
    cache_shape = jax.ShapeDtypeStruct((b, 1, N_HEADS, seq_len, hw), F32)
    return pl.pallas_call(
        kern,
        grid=(b,),
        in_specs=[col(0), col(1), col(2),
                  pl.BlockSpec((4, DH_C), lambda i: (0, 0)),
                  pl.BlockSpec((1, w), lambda i: (0, 0))],
        out_specs=[pl.BlockSpec((None, seq_len, w), lambda i: (i, 0, 0)), cache_spec, cache_spec],
        out_shape=[jax.ShapeDtypeStruct((b, seq_len, w), F32), cache_shape, cache_shape],
        compiler_params=_params("arbitrary"),
    )(proj, proj, proj, diff_lambda, diff_norm.reshape(1, -1))


def _rope(x, cos, sin_signed):
    lane = lax.broadcasted_iota(jnp.int32, (1, x.shape[-1]), 1)
    first = (lane % 32) < 16
    partner = jnp.where(first, pltpu.roll(x, x.shape[-1] - 16, 1), pltpu.roll(x, 16, 1))
    return x * cos + partner * sin_signed


def _attn_sample_kernel(q_ref, k_ref, v_ref, ck_ref, cv_ref, cosq_ref, sinq_ref, cosk_ref, sink_ref,
                        lp_ref, norm_ref, o_ref, keys_scr, vals_scr, *, lam_init):
    past = ck_ref.shape[1]
    hw = 2 * DH_C

    @pl.when(pl.program_id(1) == 0)
    def _():
        for h in range(N_HEADS):
            sl = slice(h * hw, (h + 1) * hw)
            keys_scr[h, 0:past, :] = ck_ref[h].astype(BF16)
            vals_scr[h, 0:past, :] = cv_ref[h].astype(BF16)
            keys_scr[h, past:, :] = _rope(k_ref[:, sl], cosk_ref[...], sink_ref[...]).astype(BF16)
            vals_scr[h, past:, :] = v_ref[:, sl].astype(BF16)

    lam = _diff_lambda(lp_ref, lam_init)
    for h in range(N_HEADS):
        sl = slice(h * hw, (h + 1) * hw)
        q = _rope(q_ref[:, sl], cosq_ref[...], sinq_ref[...])
        o = _diff_attend(q, keys_scr[h], vals_scr[h], lam, stack=False)
        o_ref[:, sl] = _rms(o, norm_ref[:, sl]) * (1.0 - lam_init)


def _rope_tables(seq_len):
    pos = jnp.arange(seq_len)[:, None]
    half = DH_C // 2
    nfreq = half // 2
    inv = ROPE_BASE ** (-jnp.arange(0, half, 2, dtype=F32) / half)
    lane = jnp.arange(2 * DH_C)[None, :]
    p = jnp.where((lane % DH_C) < half, pos // GRID_W, pos % GRID_W).astype(F32)
    ang = p * inv[lane[0] % nfreq][None, :]
    return jnp.cos(ang), jnp.where((lane % half) < nfreq, -jnp.sin(ang), jnp.sin(ang))


def _attn_sample(proj, ctx_k, ctx_v, diff_lambda, diff_norm, lam_init, tq):
    b, seq_len = proj.shape[0], proj.shape[1]
    past = ctx_k.shape[2]
    hw = 2 * DH_C
    w = N_HEADS * hw
    cos, sin = _rope_tables(seq_len)
    kern = functools.partial(_attn_sample_kernel, lam_init=lam_init)
    ctx_spec = pl.BlockSpec((None, N_HEADS, past, hw), lambda i, j: (i, 0, 0, 0))
    return pl.pallas_call(
        kern,
        grid=(b, seq_len // tq),
        in_specs=[pl.BlockSpec((None, tq, w), lambda i, j: (i, j, 0)),
                  pl.BlockSpec((None, seq_len, w), lambda i, j: (i, 0, 1)),
                  pl.BlockSpec((None, seq_len, w), lambda i, j: (i, 0, 2)),
                  ctx_spec, ctx_spec,
                  pl.BlockSpec((tq, hw), lambda i, j: (j, 0)),
                  pl.BlockSpec((tq, hw), lambda i, j: (j, 0)),
                  pl.BlockSpec((seq_len, hw), lambda i, j: (0, 0)),
                  pl.BlockSpec((seq_len, hw), lambda i, j: (0, 0)),
                  pl.BlockSpec((4, DH_C), lambda i, j: (0, 0)),
                  pl.BlockSpec((1, w), lambda i, j: (0, 0))],
        out_specs=pl.BlockSpec((None, tq, w), lambda i, j: (i, j, 0)),
        out_shape=jax.ShapeDtypeStruct((b, seq_len, w), F32),
        scratch_shapes=[pltpu.VMEM((N_HEADS, past + seq_len, hw), BF16),
                        pltpu.VMEM((N_HEADS, past + seq_len, hw), BF16)],
        compiler_params=_params("arbitrary", "arbitrary"),
    )(proj, proj, proj, ctx_k, ctx_v, cos, sin, cos, sin, diff_lambda, diff_norm.reshape(1, -1))


def kernel(x_prompt, x_sample, state_hgrn, cache_diff_k, cache_diff_v, state_gla, c, c_ctx, ada_w, ada_b, norm_g, ffn_up, ffn_conv_w, ffn_conv_b, ffn_down, w_in_even, w_out_even, hgrn_lb, hgrn_norm, hy_conv_w, hy_conv_b, hy_w1, hy_b1, hy_w2, hy_b2, hy_w3, hy_freq, hy_d, w_in_odd, w_out_odd, diff_lambda, diff_norm, gla_aw, gla_ab, gla_norm):
    bp, lp, d = x_prompt.shape
    bs, ls, _ = x_sample.shape

    cvec_t = jnp.zeros((d, SUBLANES), F32).at[:, 0].set(c_ctx).at[:, 1:1 + bs].set(c.T)
    mod = _ada_mod(cvec_t, 1 + bs, ada_w, ada_b)

    yp = x_prompt.reshape(1, bp * lp, d)
    ys = x_sample
    tm = 512

    filt = _hyena_filters((ls, lp), hy_w1[0], hy_b1[0], hy_w2[0], hy_b2[0], hy_w3[0], hy_freq[0])
    tab_p, tab_s = _dft_tables(lp), _dft_tables(ls)
    spec_s = _hyena_spectrum(filt, 0, ls, tab_s, 512)
    spec_p = _hyena_spectrum(filt, ls, lp, tab_p, lp)

    ffn_up_b, ffn_down_b = ffn_up.astype(BF16), ffn_down.astype(BF16)
    ffn_cb = ffn_conv_b.reshape(DEPTH, 1, -1)
    outs = {}
    for l in range(DEPTH):
        m = mod[l].reshape(SUBLANES, 6, 1, d)
        mp = [m[0:1, j] for j in range(6)]
        ms = [m[1:1 + bs, j] for j in range(6)]
        g = [norm_g[l, j].reshape(1, d) for j in range(4)]
        if l % 2 == 0:
            e = l // 2
            w_in = w_in_even[e].astype(BF16)
            w_out = w_out_even[e].astype(BF16)
            pp = _normmod_matmul(yp, g[0], mp[0], mp[1], w_in, tm).reshape(bp, lp, -1)
            ps = _normmod_matmul(ys, g[0], ms[0], ms[1], w_in, tm)
            oa_p, st_p = _hgrn_mixer(pp, hgrn_lb, hgrn_norm[e], None, l)
            oa_s, _ = _hgrn_mixer(ps, hgrn_lb, hgrn_norm[e], state_hgrn[:, e], l)
            ob_p = _hyena_mixer(pp, hy_conv_w[e], hy_conv_b[e], hy_d[e], spec_p, tab_p, lp)
            ob_s = _hyena_mixer(ps, hy_conv_w[e], hy_conv_b[e], hy_d[e], spec_s, tab_s, 256)
            outs["hgrn"] = st_p
        else:
            o = l // 2
            lam_init = 0.8 - 0.6 * math.exp(-0.3 * l)
            pad_cols = -w_in_odd.shape[-1] % LANES
            w_in = jnp.pad(w_in_odd[o], ((0, 0), (0, pad_cols))).astype(BF16)
            w_out = w_out_odd[o].astype(BF16)
            kw = N_HEADS * DK_D
            aw = jnp.zeros((2, LANES, kw), F32)
            aw = aw.at[0, 0:GLA_RANK].set(gla_aw[o, 0]).at[1, GLA_RANK:2 * GLA_RANK].set(gla_aw[o, 1])
            aw_hi = aw.astype(BF16)
            aw_mid = (aw - aw_hi.astype(F32)).astype(BF16)
            aw = jnp.concatenate([aw_hi, aw_mid, aw_hi], axis=1)
            ab = gla_ab[o].reshape(2, 1, kw)
            s0 = state_gla[:, o]
            pp = _normmod_matmul(yp, g[0], mp[0], mp[1], w_in, tm).reshape(bp, lp, -1)
            ps = _normmod_matmul(ys, g[0], ms[0], ms[1], w_in, tm)
            oa_p, kc, vc = _attn_prompt(pp, diff_lambda[o], diff_norm[o], lam_init)
            oa_s = _attn_sample(ps, cache_diff_k[:, o], cache_diff_v[:, o], diff_lambda[o], diff_norm[o], lam_init, 256)
            ob_p, st_p = _gla_mixer(pp, aw, ab, gla_norm[o], None)
            ob_s, _ = _gla_mixer(ps, aw, ab, gla_norm[o], s0)
            outs["k"], outs["v"], outs["gla"] = kc, vc, st_p
        ffn_args = (ffn_up_b, ffn_conv_w, ffn_cb, ffn_down_b, l)
        yp = _mix_ffn(yp, oa_p.reshape(1, bp * lp, -1), ob_p.reshape(1, bp * lp, -1), w_out, mp[2], g[1],
                      g[2], mp[3], mp[4], *ffn_args, mp[5], g[3], lp, tm, 1408)
        ys = _mix_ffn(ys, oa_s, ob_s, w_out, ms[2], g[1], g[2], ms[3], ms[4], *ffn_args, ms[5], g[3], ls, tm, 1408)

    return (yp.reshape(bp, lp, d), ys, outs["hgrn"], outs["k"], outs["v"], outs["gla"])
```

```python
import functools
import math

import jax
import jax.numpy as jnp
import numpy as np
from jax import lax
from jax.experimental import pallas as pl
from jax.experimental.pallas import tpu as pltpu

F32 = jnp.float32
BF16 = jnp.bfloat16
HIGHEST = lax.Precision.HIGHEST

D_MODEL = 1024
DEPTH = 2
GRID_W = 64
N_HEADS = 4
HEAD_W = 128
MIX_W = N_HEADS * HEAD_W
W_B = 512
HY_ORDER = 2
HY_EMB = 33
HY_BANDS = (HY_EMB - 1) // 2
HY_FF = 64
HY_TARGET = 1e-2
HY_FAST = 0.3
HY_SLOW = 1.5
DH_C = 64
DK_D = 64
GLA_RANK = 16
HEADS_PER_GROUP = 2
GLA_TAU = 16.0
ROPE_BASE = 10000.0
D_FF = 2816
EPS = 1e-6

LANES = 128
SUBLANES = 8
VMEM_LIMIT = 56 * 1024 * 1024
ROW_TILE = 512
FFN_COL_TILE = D_FF // 2
HYENA_FREQ_TILE = 256
SPECTRUM_FREQ_TILE = 512
ATTN_Q_TILE = 256
SCAN_CHUNK = 64
SCAN_ROWS = 256
SCAN_GROUP = 2
NT_DIMS = (((1,), (1,)), ((), ()))
LOG2E = 1.4426950408889634


def _params(*sem):
    return pltpu.CompilerParams(dimension_semantics=sem, vmem_limit_bytes=VMEM_LIMIT)


def _silu(x):
    return x * (1.0 / (1.0 + jnp.exp(-x)))


def _rms(x, g):
    return x * lax.rsqrt(jnp.mean(x * x, axis=-1, keepdims=True) + EPS) * g


def _ada_kernel(c_ref, w_ref, b_ref, o_ref, *, n_rows):
    s = _silu(c_ref[...])
    w = w_ref[...]
    rows = [jnp.sum(w * s[:, r:r + 1], axis=0, keepdims=True) for r in range(n_rows)]
    rows.append(jnp.zeros((SUBLANES - n_rows, w.shape[1]), F32))
    o_ref[...] = jnp.concatenate(rows, axis=0) + b_ref[...]


def _ada_mod(cvec_t, n_rows, ada_w, ada_b):
    n = ada_w.shape[-1]
    tn = 1536
    return pl.pallas_call(
        functools.partial(_ada_kernel, n_rows=n_rows),
        grid=(DEPTH, n // tn),
        in_specs=[pl.BlockSpec((D_MODEL, SUBLANES), lambda l, j: (0, 0)),
                  pl.BlockSpec((None, D_MODEL, tn), lambda l, j: (l, 0, j)),
                  pl.BlockSpec((None, 1, tn), lambda l, j: (l, 0, j))],
        out_specs=pl.BlockSpec((None, SUBLANES, tn), lambda l, j: (l, 0, j)),
        out_shape=jax.ShapeDtypeStruct((DEPTH, SUBLANES, n), F32),
        compiler_params=_params("arbitrary", "arbitrary"),
    )(cvec_t, ada_w, ada_b.reshape(DEPTH, 1, n))


def _normmod_matmul_kernel(x_ref, g_ref, sh_ref, sc_ref, w_ref, o_ref):
    h = _rms(x_ref[...], g_ref[...]) * (1.0 + sc_ref[...]) + sh_ref[...]
    o_ref[...] = jnp.dot(h.astype(BF16), w_ref[...], preferred_element_type=F32)


def _mod_index(n_mod):
    return (lambda b, i: (b, 0, 0)) if n_mod > 1 else (lambda b, i: (0, 0, 0))


def _normmod_matmul(x, g, shift, scale, w, tm):
    b, l, d = x.shape
    n = w.shape[1]
    mod_spec = pl.BlockSpec((None, 1, d), _mod_index(shift.shape[0]))
    return pl.pallas_call(
        _normmod_matmul_kernel,
        grid=(b, l // tm),
        in_specs=[pl.BlockSpec((None, tm, d), lambda b, i: (b, i, 0)),
                  pl.BlockSpec((1, d), lambda b, i: (0, 0)),
                  mod_spec, mod_spec,
                  pl.BlockSpec((d, n), lambda b, i: (0, 0))],
        out_specs=pl.BlockSpec((None, tm, n), lambda b, i: (b, i, 0)),
        out_shape=jax.ShapeDtypeStruct((b, l, n), F32),
        compiler_params=_params("arbitrary", "arbitrary"),
    )(x, g, shift, scale, w)


def _patch_rows(x, keep, starts):
    pieces, r = [], 0
    for s in starts:
        if s > r:
            pieces.append(x[r:s])
        pieces.append(jnp.where(keep[s:s + SUBLANES], x[s:s + SUBLANES], 0.0))
        r = s + SUBLANES
    if r < x.shape[0]:
        pieces.append(x[r:])
    return jnp.concatenate(pieces, axis=0)


def _mix_ffn_kernel(y_ref, yp_ref, yn_ref, a_ref, ap_ref, an_ref, b_ref, bp_ref, bn_ref, wo_ref, gate1_ref, g1_ref,
                    g2_ref, sh_ref, sc_ref, ua_ref, ug_ref, cwa_ref, cwg_ref, cba_ref, cbg_ref, dn_ref, gate2_ref,
                    g3_ref, o_ref, y1_scr, h_scr, acc_scr, *, seq_len):
    i = pl.program_id(1)
    f = pl.program_id(2)
    tm = y_ref.shape[0]
    halo = yp_ref.shape[0]
    half = a_ref.shape[1]

    @pl.when(f == 0)
    def _():
        rows_of = lambda p, m, n: jnp.concatenate([p[...], m[...], n[...]], axis=0)
        a_all = rows_of(ap_ref, a_ref, an_ref).astype(BF16)
        b_all = rows_of(bp_ref, b_ref, bn_ref).astype(BF16)
        m = jnp.dot(a_all, wo_ref[:half, :], preferred_element_type=F32)
        m = m + jnp.dot(b_all, wo_ref[half:, :], preferred_element_type=F32)
        y1 = rows_of(yp_ref, y_ref, yn_ref) + gate1_ref[...] * _rms(m, g1_ref[...])
        y1_scr[...] = y1[halo:halo + tm]
        h_scr[...] = (_rms(y1, g2_ref[...]) * (1.0 + sc_ref[...]) + sh_ref[...]).astype(BF16)
        acc_scr[...] = jnp.zeros_like(acc_scr)

    rows = tm + 2 * halo
    pos = (i * tm + lax.broadcasted_iota(jnp.int32, (tm, 1), 0)) % seq_len
    has_prev = pos != 0
    has_next = pos != seq_len - 1
    period = math.gcd(tm, seq_len)
    first_groups = list(range(0, tm, period))
    last_groups = [s + period - SUBLANES for s in first_groups]

    def conv(u_ref, cw_ref, cb_ref):
        u = jnp.dot(h_scr[...], u_ref[...], preferred_element_type=F32)
        up = _patch_rows(pltpu.roll(u, 1, 0)[halo:halo + tm], has_prev, first_groups)
        un = _patch_rows(pltpu.roll(u, rows - 1, 0)[halo:halo + tm], has_next, last_groups)
        uc = u[halo:halo + tm]
        return up * cw_ref[0:1, :] + uc * cw_ref[1:2, :] + un * cw_ref[2:3, :] + cb_ref[...]

    a = conv(ua_ref, cwa_ref, cba_ref)
    gt = conv(ug_ref, cwg_ref, cbg_ref)
    act = (_silu(gt) * a).astype(BF16)
    acc_scr[...] += jnp.dot(act, dn_ref[...], preferred_element_type=F32)

    @pl.when(f == pl.num_programs(2) - 1)
    def _():
        o_ref[...] = y1_scr[...] + gate2_ref[...] * _rms(acc_scr[...], g3_ref[...])


def _mix_ffn(y, a, bm, w_out, gate1, g1, g2, shift, scale, up, cw, cb, down, layer, gate2, g3, seq_len, tm, tf):
    b, l, d = y.shape
    wa = a.shape[-1]
    nf = D_FF // tf
    halo = SUBLANES
    hb = tm // halo
    last_hb = l // halo - 1
    n_mod = shift.shape[0]
    mod_spec = pl.BlockSpec((None, 1, d), (lambda b, i, f: (b, 0, 0)) if n_mod > 1 else (lambda b, i, f: (0, 0, 0)))
    vec = lambda off: pl.BlockSpec((None, 1, tf), lambda b, i, f: (layer, 0, off + f))
    row_d = pl.BlockSpec((1, d), lambda b, i, f: (0, 0))

    def tiles(w):
        return [pl.BlockSpec((None, tm, w), lambda b, i, f: (b, i, 0)),
                pl.BlockSpec((None, halo, w), lambda b, i, f: (b, jnp.maximum(i * hb - 1, 0), 0)),
                pl.BlockSpec((None, halo, w), lambda b, i, f: (b, jnp.minimum((i + 1) * hb, last_hb), 0))]

    kern = functools.partial(_mix_ffn_kernel, seq_len=seq_len)
    return pl.pallas_call(
        kern,
        grid=(b, l // tm, nf),
        in_specs=tiles(d) + tiles(wa) + tiles(wa) + [
                  pl.BlockSpec((2 * wa, d), lambda b, i, f: (0, 0)),
                  mod_spec, row_d,
                  row_d, mod_spec, mod_spec,
                  pl.BlockSpec((None, d, tf), lambda b, i, f: (layer, 0, f)),
                  pl.BlockSpec((None, d, tf), lambda b, i, f: (layer, 0, nf + f)),
                  pl.BlockSpec((None, 3, tf), lambda b, i, f: (layer, 0, f)),
                  pl.BlockSpec((None, 3, tf), lambda b, i, f: (layer, 0, nf + f)),
                  vec(0), vec(nf),
                  pl.BlockSpec((None, tf, d), lambda b, i, f: (layer, f, 0)),
                  mod_spec, row_d],
        out_specs=pl.BlockSpec((None, tm, d), lambda b, i, f: (b, i, 0)),
        out_shape=jax.ShapeDtypeStruct((b, l, d), F32),
        scratch_shapes=[pltpu.VMEM((tm, d), F32), pltpu.VMEM((tm + 2 * halo, d), BF16), pltpu.VMEM((tm, d), F32)],
        compiler_params=_params("arbitrary", "arbitrary", "arbitrary"),
    )(y, y, y, a, a, a, bm, bm, bm, w_out, gate1, g1, g2, shift, scale, up, up, cw, cw, cb, cb, down, gate2, g3)


def _scan_tables(c, heads_per_group):
    nlev = int(math.log2(c))
    t = np.arange(c)[:, None]
    r = np.arange(c)[None, :]
    masks_f = [np.eye(c, dtype=bool)]
    for lev in range(1, nlev + 1):
        bsz = 2 ** lev
        mid = (t // bsz) * bsz + bsz // 2
        masks_f.append(((t // bsz) == (r // bsz)) & (t >= mid) & (r < mid))
    m_f = np.stack(masks_f).astype(np.float32)
    m_b = np.transpose(m_f, (0, 2, 1))
    tri = lambda a: jnp.asarray(np.concatenate([a, a, a], axis=1).astype(np.float32), dtype=BF16)
    rep = lambda m: jnp.asarray(np.concatenate([m] * heads_per_group, axis=2))
    return tri(r <= t), tri(r >= t), rep(m_f), rep(m_b)


def _level_decay(cum, ncum, lg2, lev, bwd):
    c, w = cum.shape
    b = 1 << lev
    half = b // 2
    off = half - 1 + int(bwd)
    if b == 2:
        odd = lax.broadcasted_iota(jnp.int32, (c, 1), 0) % 2 == 1
        return jnp.where(odd != bwd, lg2, 0.0)
    if b >= 2 * SUBLANES:
        pieces = []
        for b0 in range(0, c, b):
            for rows, is_upper in ((slice(b0, b0 + half), False), (slice(b0 + half, b0 + b), True)):
                src = cum if is_upper != bwd else ncum
                pieces.append(src[rows] - jnp.broadcast_to(src[b0 + off:b0 + off + 1], (half, w)))
        return jnp.concatenate(pieces, axis=0)
    cum3 = cum.reshape(c // SUBLANES, SUBLANES, w)
    sub = lax.broadcasted_iota(jnp.int32, (1, SUBLANES, 1), 1)
    if b == SUBLANES:
        ref3 = jnp.broadcast_to(cum3[:, off:off + 1], cum3.shape)
    else:
        ref3 = jnp.where(sub < b, cum3[:, off:off + 1], cum3[:, b + off:b + off + 1])
    upper = (sub % b) >= half
    sgn = jnp.where(upper != bwd, 1.0, -1.0)
    return ((cum3 - ref3) * sgn).reshape(c, w)


def _head_stack(xb, width):
    n = xb.shape[1] // width
    lane = lax.broadcasted_iota(jnp.int32, (1, xb.shape[1]), 1)
    zero = jnp.zeros_like(xb)
    return jnp.concatenate([jnp.where((lane >= j * width) & (lane < (j + 1) * width), xb, zero) for j in range(n)],
                           axis=0)


def _scan_group(chunks):
    dk = chunks[0][0].shape[1] // N_HEADS
    hpg = HEADS_PER_GROUP
    gw = hpg * dk
    ngroups = N_HEADS // hpg
    vw = hpg * HEAD_W
    work = []
    for q, k, v, lg2, tri_ref, m_ref, st_ref, bwd in chunks:
        hi = lg2.astype(BF16)
        r1 = lg2 - hi.astype(F32)
        mid = r1.astype(BF16)
        lo = (r1 - mid.astype(F32)).astype(BF16)
        cum = jnp.dot(tri_ref[...], jnp.concatenate([hi, mid, lo], axis=0), preferred_element_type=F32)
        work.append(dict(q=q, k=k, v=v, lg2=lg2, cum=cum, ncum=-cum, m_ref=m_ref, st_ref=st_ref, bwd=bwd,
                         q16=q.astype(BF16), k16=k.astype(BF16), att=[None] * ngroups))
    nlev = chunks[0][5].shape[0] - 1
    for lev in range(nlev + 1):
        for w in work:
            if lev == 0:
                qb, kb = w["q16"], w["k16"]
            else:
                e = jnp.exp2(_level_decay(w["cum"], w["ncum"], w["lg2"], lev, w["bwd"])).astype(BF16)
                qb, kb = w["q16"] * e, w["k16"] * e
            for g in range(ngroups):
                sl = slice(g * gw, (g + 1) * gw)
                prod = lax.dot_general(qb[:, sl], _head_stack(kb[:, sl], dk), NT_DIMS,
                                       preferred_element_type=F32)
                term = w["m_ref"][lev] * prod
                w["att"][g] = term if w["att"][g] is None else w["att"][g] + term
    lane = lax.broadcasted_iota(jnp.int32, (1, gw), 1)
    results = []
    for w in work:
        q, k, v, cum, st_ref = w["q"], w["k"], w["v"], w["cum"], w["st_ref"]
        c = q.shape[0]
        last = 0 if w["bwd"] else c - 1
        e_cum = jnp.exp2(cum)
        d_last = e_cum[last:last + 1, :]
        qe = (q * e_cum).astype(BF16)
        kd = (k * jnp.exp2(cum[last:last + 1, :] - cum)).astype(BF16)
        vb = v.astype(BF16)
        outs = []
        for g in range(ngroups):
            sl = slice(g * gw, (g + 1) * gw)
            st = st_ref[g]
            stb = st.astype(BF16)
            o_g = jnp.dot(w["att"][g].astype(BF16), _head_stack(vb[:, g * vw:(g + 1) * vw], HEAD_W),
                          preferred_element_type=F32)
            qe_g = qe[:, sl]
            zero = jnp.zeros_like(qe_g)
            inter = [lax.dot_general(jnp.where((lane >= j * dk) & (lane < (j + 1) * dk), qe_g, zero), stb, NT_DIMS,
                                     preferred_element_type=F32) for j in range(hpg)]
            outs.append(o_g + jnp.concatenate(inter, axis=1))
            vstack = jnp.concatenate([v[:, g * vw + j * HEAD_W:g * vw + (j + 1) * HEAD_W] for j in range(hpg)],
                                     axis=0)
            st_ref[g] = st * d_last[:, sl] + jnp.dot(vstack.T.astype(BF16), _head_stack(kd[:, sl], dk),
                                                      preferred_element_type=F32)
        results.append(jnp.concatenate(outs, axis=1))
    return results


def _bidir_scan_body(load_fwd, load_bwd, g_ref, s0_ref, norm_ref, af_ref, ab_ref, mf_ref, mb_ref,
                     o_ref, s_out_ref, of_scr, ob_scr, stf_scr, stb_scr, seq_len):
    c = SCAN_CHUNK
    j = pl.program_id(1)
    nblk = pl.num_programs(1)
    n = SCAN_ROWS // c
    ngroups, _, gw = stf_scr.shape
    hpg = N_HEADS // ngroups
    dk = gw // hpg

    @pl.when(j == 0)
    def _():
        for g in range(ngroups):
            if s0_ref is None:
                stf_scr[g] = jnp.zeros((HEAD_W, gw), F32)
                stb_scr[g] = jnp.zeros((HEAD_W, gw), F32)
            else:
                stf_scr[g] = jnp.concatenate([s0_ref[0, g * hpg + j].T for j in range(hpg)], axis=1)
                stb_scr[g] = jnp.concatenate([s0_ref[1, g * hpg + j].T for j in range(hpg)], axis=1)

    base_f = j * SCAN_ROWS
    base_b = (nblk - 1 - j) * SCAN_ROWS

    def step(i, carry):
        chunks, stores = [], []
        for u in range(SCAN_GROUP):
            rf = pl.multiple_of((i * SCAN_GROUP + u) * c, c)
            chunks.append(load_fwd(rf) + (af_ref, mf_ref, stf_scr, False))
            stores.append((of_scr, pl.multiple_of(base_f + rf, c)))
        for u in range(SCAN_GROUP):
            rb = pl.multiple_of((n - 1 - i * SCAN_GROUP - u) * c, c)
            chunks.append(load_bwd(rb) + (ab_ref, mb_ref, stb_scr, True))
            stores.append((ob_scr, pl.multiple_of(base_b + rb, c)))
        for (scr, r0), o in zip(stores, _scan_group(chunks)):
            scr[pl.ds(r0, c), :] = o
        return carry

    lax.fori_loop(0, n // SCAN_GROUP, step, 0)

    @pl.when(j == nblk - 1)
    def _():
        def fin(jj, carry):
            r0 = pl.multiple_of(jj * SCAN_ROWS, SCAN_ROWS)
            o = of_scr[pl.ds(r0, SCAN_ROWS), :] + ob_scr[pl.ds(r0, SCAN_ROWS), :]
            parts = []
            for h in range(N_HEADS):
                sl = slice(h * HEAD_W, (h + 1) * HEAD_W)
                parts.append(_rms(o[:, sl], norm_ref[:, sl]))
            o_ref[pl.ds(r0, SCAN_ROWS), :] = jnp.concatenate(parts, axis=1) * _silu(g_ref[pl.ds(r0, SCAN_ROWS), :])
            return carry

        lax.fori_loop(0, seq_len // SCAN_ROWS, fin, 0)
        for h in range(N_HEADS):
            g, hs = h // hpg, slice((h % hpg) * dk, (h % hpg + 1) * dk)
            s_out_ref[0, h] = stf_scr[g][:, hs].T
            s_out_ref[1, h] = stb_scr[g][:, hs].T


def _hgrn_kernel(*refs, layer, has_s0, seq_len):
    qf_ref, ff_ref, if_ref, qb_ref, fb_ref, ib_ref, g_ref, lbraw_ref, norm_ref, af_ref, ab_ref, mf_ref, mb_ref = refs[:13]
    s0_ref = refs[13] if has_s0 else None
    o_ref, s_out_ref, of_scr, ob_scr, stf_scr, stb_scr = refs[13 + has_s0:]
    c = SCAN_CHUNK
    raw = lbraw_ref[...]
    ex = jnp.exp(raw - jnp.max(raw, axis=0, keepdims=True))
    sm = ex / jnp.sum(ex, axis=0, keepdims=True)
    lb = sm[0]
    for j in range(1, layer + 1):
        lb = lb + sm[j]

    def load(q_ref, f_ref, i_ref, lb_row):
        def fn(r0):
            q = _silu(q_ref[pl.ds(r0, c), :]) * (HEAD_W ** -0.5)
            sig = 1.0 / (1.0 + jnp.exp(-f_ref[pl.ds(r0, c), :]))
            f = lb_row + (1.0 - lb_row) * sig
            return q, 1.0 - f, i_ref[pl.ds(r0, c), :], jnp.log2(f)
        return fn

    _bidir_scan_body(load(qf_ref, ff_ref, if_ref, lb[0:1]), load(qb_ref, fb_ref, ib_ref, lb[1:2]), g_ref,
                     s0_ref, norm_ref, af_ref, ab_ref, mf_ref, mb_ref,
                     o_ref, s_out_ref, of_scr, ob_scr, stf_scr, stb_scr, seq_len)


def _gla_kernel(*refs, has_s0, seq_len):
    fwd_refs, bwd_refs = refs[0:4], refs[4:8]
    g_ref, aw_ref, ab_ref, norm_ref, af_ref, abk_ref, mf_ref, mb_ref = refs[8:16]
    s0_ref = refs[16] if has_s0 else None
    o_ref, s_out_ref, of_scr, ob_scr, stf_scr, stb_scr = refs[16 + has_s0:]
    c = SCAN_CHUNK

    def load(d, q_ref, k_ref, v_ref, da_ref):
        def fn(r0):
            q = q_ref[pl.ds(r0, c), :] * (DK_D ** -0.5)
            da = da_ref[pl.ds(r0, c), :]
            da_hi = da.astype(BF16)
            da_mid = (da - da_hi.astype(F32)).astype(BF16)
            xa = jnp.dot(jnp.concatenate([da_hi, da_hi, da_mid], axis=1), aw_ref[d],
                         preferred_element_type=F32) + ab_ref[d]
            la = (jnp.minimum(xa, 0.0) - jnp.log(1.0 + jnp.exp(-jnp.abs(xa)))) * (LOG2E / GLA_TAU)
            return q, k_ref[pl.ds(r0, c), :], v_ref[pl.ds(r0, c), :], la
        return fn

    _bidir_scan_body(load(0, *fwd_refs), load(1, *bwd_refs), g_ref,
                     s0_ref, norm_ref, af_ref, abk_ref, mf_ref, mb_ref,
                     o_ref, s_out_ref, of_scr, ob_scr, stf_scr, stb_scr, seq_len)


def _scan_call(kern, proj, streams, extra, s0, seq_len, dk):
    b = proj.shape[0]
    nblk = seq_len // SCAN_ROWS
    hpg = HEADS_PER_GROUP
    tabs = _scan_tables(SCAN_CHUNK, hpg)
    full = lambda a: pl.BlockSpec(a.shape, lambda i, j, _n=a.ndim: (0,) * _n)
    in_specs = []
    for cb, w, kind in streams:
        if kind == "f":
            in_specs.append(pl.BlockSpec((None, SCAN_ROWS, w), lambda i, j, _c=cb: (i, j, _c)))
        elif kind == "b":
            in_specs.append(pl.BlockSpec((None, SCAN_ROWS, w), lambda i, j, _c=cb: (i, nblk - 1 - j, _c)))
        else:
            in_specs.append(pl.BlockSpec((None, seq_len, w), lambda i, j, _c=cb: (i, 0, _c)))
    args = [proj] * len(streams)
    for a in tuple(extra) + tabs:
        in_specs.append(full(a))
        args.append(a)
    if s0 is not None:
        in_specs.append(pl.BlockSpec((None, 2, N_HEADS, dk, HEAD_W), lambda i, j: (i, 0, 0, 0, 0)))
        args.append(s0)
    return pl.pallas_call(
        kern,
        grid=(b, nblk),
        in_specs=in_specs,
        out_specs=[pl.BlockSpec((None, seq_len, MIX_W), lambda i, j: (i, 0, 0)),
                   pl.BlockSpec((None, None, 2, N_HEADS, dk, HEAD_W), lambda i, j: (i, 0, 0, 0, 0, 0))],
        out_shape=[jax.ShapeDtypeStruct((b, seq_len, MIX_W), F32),
                   jax.ShapeDtypeStruct((b, 1, 2, N_HEADS, dk, HEAD_W), F32)],
        scratch_shapes=[pltpu.VMEM((seq_len, MIX_W), F32), pltpu.VMEM((seq_len, MIX_W), F32),
                        pltpu.VMEM((N_HEADS // hpg, HEAD_W, hpg * dk), F32),
                        pltpu.VMEM((N_HEADS // hpg, HEAD_W, hpg * dk), F32)],
        compiler_params=_params("arbitrary", "arbitrary"),
    )(*args)


def _hgrn_mixer(proj, hgrn_lb, norm, s0, layer):
    seq_len = proj.shape[1]
    kern = functools.partial(_hgrn_kernel, layer=layer, has_s0=s0 is not None, seq_len=seq_len)
    streams = [(0, MIX_W, "f"), (1, MIX_W, "f"), (3, MIX_W, "f"),
               (0, MIX_W, "b"), (2, MIX_W, "b"), (3, MIX_W, "b"), (4, MIX_W, "w")]
    return _scan_call(kern, proj, streams, (hgrn_lb, norm.reshape(1, MIX_W)), s0, seq_len, HEAD_W)


def _gla_mixer(proj, aw, ab, norm, s0):
    seq_len = proj.shape[1]
    kern = functools.partial(_gla_kernel, has_s0=s0 is not None, seq_len=seq_len)
    kw = N_HEADS * DK_D
    streams = [(1536 // kw, kw, "f"), (1792 // kw, kw, "f"), (2048 // MIX_W, MIX_W, "f"), (3072 // LANES, LANES, "f"),
               (1536 // kw, kw, "b"), (1792 // kw, kw, "b"), (2048 // MIX_W, MIX_W, "b"), (3072 // LANES, LANES, "b"),
               (2560 // MIX_W, MIX_W, "w")]
    return _scan_call(kern, proj, streams, (aw, ab, norm.reshape(1, MIX_W)), s0, seq_len, DK_D)


def _dwconv3_rows(x, w_ref, b_ref):
    l = x.shape[0]
    row = lax.broadcasted_iota(jnp.int32, (l, 1), 0)
    xp = _patch_rows(pltpu.roll(x, 1, 0), row != 0, [0])
    xn = _patch_rows(pltpu.roll(x, l - 1, 0), row != l - 1, [l - SUBLANES])
    return xp * w_ref[0:1, :] + x * w_ref[1:2, :] + xn * w_ref[2:3, :] + b_ref[...]


def _hyena_filter_kernel(z_ref, w1_ref, b1_ref, w2_ref, b2_ref, w3_ref, fr_ref, dl_ref, o_ref):
    z = z_ref[...]
    fr = fr_ref[...]
    h = jnp.sin(fr * (jnp.dot(z, w1_ref[...], preferred_element_type=F32, precision=HIGHEST) + b1_ref[...]))
    h = jnp.sin(fr * (jnp.dot(h, w2_ref[...], preferred_element_type=F32, precision=HIGHEST) + b2_ref[...]))
    h_hi = h.astype(BF16)
    h_mid = (h - h_hi.astype(F32)).astype(BF16)
    h = jnp.dot(jnp.concatenate([h_hi, h_hi, h_mid], axis=1), w3_ref[...],
                preferred_element_type=F32)
    win = jnp.exp(-z[:, 0:1] * dl_ref[...])
    o_ref[...] = h * jnp.concatenate([win] * (2 * HY_ORDER), axis=1)


def _hyena_pos_features(l):
    t = jnp.linspace(0.0, 1.0, l, dtype=F32)[:, None]
    w = 2.0 * math.pi * jnp.arange(l, dtype=F32)[:, None] / l
    fb = jnp.linspace(1e-4, HY_BANDS - 1, HY_BANDS, dtype=F32)[None]
    z = jnp.concatenate([t, jnp.cos(fb * w), -jnp.sin(fb * w)], axis=-1)
    return jnp.pad(z, ((0, 0), (0, HY_FF - HY_EMB)))


def _hyena_filters(lens, w1, b1, w2, b2, w3, freq):
    z = jnp.concatenate([_hyena_pos_features(l) for l in lens], axis=0)
    rows = z.shape[0]
    tr = 256
    w1p = jnp.pad(w1, ((0, HY_FF - HY_EMB), (0, 0)))
    max_decay = math.log(HY_TARGET) / HY_FAST
    min_decay = math.log(HY_TARGET) / HY_SLOW
    deltas = jnp.abs(jnp.linspace(min_decay, max_decay, W_B, dtype=F32))[None]
    nout = w3.shape[1]
    w3_hi = w3.astype(BF16)
    w3_mid = (w3 - w3_hi.astype(F32)).astype(BF16)
    w3s = jnp.concatenate([w3_hi, w3_mid, w3_hi], axis=0)
    full = lambda a: pl.BlockSpec(a.shape, lambda i, _n=a.ndim: (0,) * _n)
    ins = (w1p, b1.reshape(1, -1), w2, b2.reshape(1, -1), w3s, freq.reshape(1, -1), deltas)
    return pl.pallas_call(
        _hyena_filter_kernel,
        grid=(rows // tr,),
        in_specs=[pl.BlockSpec((tr, HY_FF), lambda i: (i, 0))] + [full(a) for a in ins],
        out_specs=pl.BlockSpec((tr, nout), lambda i: (i, 0)),
        out_shape=jax.ShapeDtypeStruct((rows, nout), F32),
        compiler_params=_params("arbitrary"),
    )(z, *ins)


def _dft_table_kernel(ca_ref, sa_ref, cb_ref, sb_ref, cos_ref, sinf_ref, sini_ref):
    tk = ca_ref.shape[0]
    ca, sa = ca_ref[...], sa_ref[...]
    row = pl.program_id(0) * tk + lax.broadcasted_iota(jnp.int32, (tk, 1), 0)
    lane = lax.broadcasted_iota(jnp.int32, (1, LANES), 1)
    alt_row = jnp.where(row % 2 == 0, 1.0, -1.0)
    for grp in range(cb_ref.shape[1]):
        cols = slice(grp * LANES, (grp + 1) * LANES)
        cbg, sbg = cb_ref[:, grp:grp + 1], sb_ref[:, grp:grp + 1]
        sin_t = sa * cbg + ca * sbg
        col = grp * LANES + lane
        cos_ref[:, cols] = (ca * cbg - sa * sbg).astype(BF16)
        sinf_ref[:, cols] = jnp.where(row == 0, jnp.where(col % 2 == 0, 1.0, -1.0), sin_t).astype(BF16)
        sini_ref[:, cols] = jnp.where(col == 0, alt_row, sin_t).astype(BF16)


def _dft_tables(l):
    n = 2 * l
    k = jnp.arange(l, dtype=jnp.int32)[:, None]
    t1 = jnp.arange(LANES, dtype=jnp.int32)[None, :]
    t2 = (jnp.arange(l // LANES, dtype=jnp.int32) * LANES)[None, :]
    ang = lambda m: (m % n).astype(F32) * (2.0 * math.pi / n)
    small = (jnp.cos(ang(k * t1)), jnp.sin(ang(k * t1)), jnp.cos(ang(k * t2)), jnp.sin(ang(k * t2)))
    tk = min(l, 256)
    out = jax.ShapeDtypeStruct((l, l), BF16)
    ospec = pl.BlockSpec((tk, l), lambda i: (i, 0))
    return pl.pallas_call(
        _dft_table_kernel,
        grid=(l // tk,),
        in_specs=[pl.BlockSpec((tk, a.shape[1]), lambda i: (i, 0)) for a in small],
        out_specs=[ospec, ospec, ospec],
        out_shape=[out, out, out],
        compiler_params=_params("arbitrary"),
    )(*small)


def _hyena_spectrum_kernel(c_ref, s_ref, f0_ref, f1_ref, kr_ref, kia_ref, krb_ref, fs_scr, fd_scr, nyq_scr, *,
                           seq_len):
    kt = pl.program_id(1)
    tk = c_ref.shape[0]

    @pl.when(kt == 0)
    def _():
        row = lax.broadcasted_iota(jnp.int32, (seq_len, 1), 0)
        f0 = f0_ref[...]
        f1 = jnp.where(row != 0, f1_ref[...], 0.0)
        fsum = f0 + f1
        fs_scr[...] = fsum.astype(BF16)
        fd_scr[...] = (f1 - f0).astype(BF16)
        nyq_scr[...] = jnp.sum(jnp.where(row % 2 == 0, fsum, -fsum), axis=0, keepdims=True)

    kr = jnp.dot(c_ref[...], fs_scr[...], preferred_element_type=F32)
    ki = jnp.dot(s_ref[...], fd_scr[...], preferred_element_type=F32)
    krow = kt * tk + lax.broadcasted_iota(jnp.int32, (tk, 1), 0)
    dc = krow == 0
    wk = jnp.where(dc, 1.0, 2.0) * (1.0 / (2 * seq_len))
    kr_ref[...] = kr * wk
    kia_ref[...] = jnp.where(dc, 0.0, ki) * wk
    krb_ref[...] = jnp.where(dc, nyq_scr[...], kr) * wk


def _hyena_spectrum(filt, row0, seq_len, tables, tk):
    cos_t, sin_f, _ = tables
    rb = row0 // seq_len
    out = jax.ShapeDtypeStruct((HY_ORDER, seq_len, W_B), F32)
    kern = functools.partial(_hyena_spectrum_kernel, seq_len=seq_len)
    ospec = pl.BlockSpec((None, tk, W_B), lambda o, kt: (o, kt, 0))
    return pl.pallas_call(
        kern,
        grid=(HY_ORDER, seq_len // tk),
        in_specs=[pl.BlockSpec((tk, seq_len), lambda o, kt: (kt, 0)),
                  pl.BlockSpec((tk, seq_len), lambda o, kt: (kt, 0)),
                  pl.BlockSpec((seq_len, W_B), lambda o, kt: (rb, 2 * o)),
                  pl.BlockSpec((seq_len, W_B), lambda o, kt: (rb, 2 * o + 1))],
        out_specs=[ospec, ospec, ospec],
        out_shape=[out, out, out],
        scratch_shapes=[pltpu.VMEM((seq_len, W_B), BF16), pltpu.VMEM((seq_len, W_B), BF16),
                        pltpu.VMEM((1, W_B), F32)],
        compiler_params=_params("arbitrary", "arbitrary"),
    )(cos_t, sin_f, filt, filt)


def _hyena_order_kernel(zin_ref, gate_ref, cwz_ref, cbz_ref, cwg_ref, cbg_ref, d_ref, kr_ref, kia_ref, krb_ref,
                        cf_ref, sf_ref, ci_ref, si_ref, o_ref, z_scr, zb_scr, acc_scr, *, conv_input):
    kt = pl.program_id(1)

    @pl.when(kt == 0)
    def _():
        z = zin_ref[...]
        if conv_input:
            z = _dwconv3_rows(z, cwz_ref, cbz_ref)
        z_scr[...] = z
        zb_scr[...] = z.astype(BF16)
        acc_scr[...] = jnp.zeros_like(acc_scr)

    zb = zb_scr[...]
    p = jnp.dot(cf_ref[...], zb, preferred_element_type=F32)
    q = jnp.dot(sf_ref[...], zb, preferred_element_type=F32)
    kia = kia_ref[...]
    yr = (p * kr_ref[...] + q * kia).astype(BF16)
    yi = (q * krb_ref[...] - p * kia).astype(BF16)
    acc_scr[...] += (jnp.dot(ci_ref[...], yr, preferred_element_type=F32)
                     + jnp.dot(si_ref[...], yi, preferred_element_type=F32))

    @pl.when(kt == pl.num_programs(1) - 1)
    def _():
        gate = _dwconv3_rows(gate_ref[...], cwg_ref, cbg_ref)
        o_ref[...] = gate * (acc_scr[...] + z_scr[...] * d_ref[...])


def _hyena_order(zin, zin_col, proj, order, conv_w, conv_b, hy_d, spectrum, tables, tk):
    b, seq_len = proj.shape[0], proj.shape[1]
    cos_t, sin_f, sin_i = tables
    kr, kia, krb = spectrum
    hy0 = 5
    conv_input = order == 0
    cw = conv_w.reshape(3, 1 + HY_ORDER, W_B).transpose(1, 0, 2)
    cbias = conv_b.reshape(1 + HY_ORDER, 1, W_B)
    kern = functools.partial(_hyena_order_kernel, conv_input=conv_input)
    kspec = pl.BlockSpec((None, tk, W_B), lambda i, kt: (order, kt, 0))
    return pl.pallas_call(
        kern,
        grid=(b, seq_len // tk),
        in_specs=[pl.BlockSpec((None, seq_len, W_B), lambda i, kt: (i, 0, zin_col)),
                  pl.BlockSpec((None, seq_len, W_B), lambda i, kt: (i, 0, hy0 + 1 + order)),
                  pl.BlockSpec((None, 3, W_B), lambda i, kt: (0, 0, 0)),
                  pl.BlockSpec((None, 1, W_B), lambda i, kt: (0, 0, 0)),
                  pl.BlockSpec((None, 3, W_B), lambda i, kt: (1 + order, 0, 0)),
                  pl.BlockSpec((None, 1, W_B), lambda i, kt: (1 + order, 0, 0)),
                  pl.BlockSpec((None, 1, W_B), lambda i, kt: (order, 0, 0)),
                  kspec, kspec, kspec,
                  pl.BlockSpec((tk, seq_len), lambda i, kt: (kt, 0)),
                  pl.BlockSpec((tk, seq_len), lambda i, kt: (kt, 0)),
                  pl.BlockSpec((seq_len, tk), lambda i, kt: (0, kt)),
                  pl.BlockSpec((seq_len, tk), lambda i, kt: (0, kt))],
        out_specs=pl.BlockSpec((None, seq_len, W_B), lambda i, kt: (i, 0, 0)),
        out_shape=jax.ShapeDtypeStruct((b, seq_len, W_B), F32),
        scratch_shapes=[pltpu.VMEM((seq_len, W_B), F32), pltpu.VMEM((seq_len, W_B), BF16),
                        pltpu.VMEM((seq_len, W_B), F32)],
        compiler_params=_params("arbitrary", "arbitrary"),
    )(zin, proj, cw, cbias, cw, cbias, hy_d.reshape(HY_ORDER, 1, W_B), kr, kia, krb, cos_t, sin_f, cos_t, sin_i)


def _hyena_short_kernel(v_ref, x1_ref, x2_ref, cw_ref, cb_ref, d_ref, kr_ref, kia_ref, krb_ref,
                        cf_ref, sf_ref, si_ref, o_ref):
    cf, sf, si = cf_ref[...], sf_ref[...], si_ref[...]
    nb = v_ref.shape[0]
    zs = [_dwconv3_rows(v_ref[i], cw_ref.at[0], cb_ref.at[0]) for i in range(nb)]
    for order, gate_ref in enumerate((x1_ref, x2_ref)):
        zb = [z.astype(BF16) for z in zs]
        ps = [jnp.dot(cf, z, preferred_element_type=F32) for z in zb]
        qs = [jnp.dot(sf, z, preferred_element_type=F32) for z in zb]
        kia = kia_ref[order]
        yr = [(p * kr_ref[order] + q * kia).astype(BF16) for p, q in zip(ps, qs)]
        yi = [(q * krb_ref[order] - p * kia).astype(BF16) for p, q in zip(ps, qs)]
        conv = [jnp.dot(cf, r, preferred_element_type=F32) + jnp.dot(si, m, preferred_element_type=F32)
                for r, m in zip(yr, yi)]
        gates = [_dwconv3_rows(gate_ref[i], cw_ref.at[1 + order], cb_ref.at[1 + order]) for i in range(nb)]
        zs = [g * (c + z * d_ref[order]) for g, c, z in zip(gates, conv, zs)]
    for i in range(nb):
        o_ref[i] = zs[i]


def _hyena_short(proj, conv_w, conv_b, hy_d, spectrum, tables):
    b, seq_len = proj.shape[0], proj.shape[1]
    cos_t, sin_f, sin_i = tables
    nb = 2 if b % 2 == 0 else 1
    hy0 = 5
    cw = conv_w.reshape(3, 1 + HY_ORDER, W_B).transpose(1, 0, 2)
    cbias = conv_b.reshape(1 + HY_ORDER, 1, W_B)
    full = lambda a: pl.BlockSpec(a.shape, lambda i, _n=a.ndim: (0,) * _n)
    col = lambda j: pl.BlockSpec((nb, seq_len, W_B), lambda i: (i, 0, hy0 + j))
    consts = (cw, cbias, hy_d.reshape(HY_ORDER, 1, W_B)) + tuple(spectrum) + (cos_t, sin_f, sin_i)
    return pl.pallas_call(
        _hyena_short_kernel,
        grid=(b // nb,),
        in_specs=[col(0), col(1), col(2)] + [full(a) for a in consts],
        out_specs=pl.BlockSpec((nb, seq_len, W_B), lambda i: (i, 0, 0)),
        out_shape=jax.ShapeDtypeStruct((b, seq_len, W_B), F32),
        compiler_params=_params("arbitrary"),
    )(proj, proj, proj, *consts)


def _hyena_mixer(proj, conv_w, conv_b, hy_d, spectrum, tables, tk):
    if tk == proj.shape[1]:
        return _hyena_short(proj, conv_w, conv_b, hy_d, spectrum, tables)
    z = _hyena_order(proj, 5, proj, 0, conv_w, conv_b, hy_d, spectrum, tables, tk)
    return _hyena_order(z, 0, proj, 1, conv_w, conv_b, hy_d, spectrum, tables, tk)


def _diff_lambda(lp_ref, lam_init):
    lp = lp_ref[...]
    a = jnp.sum(lp[0:1] * lp[1:2], axis=-1, keepdims=True)
    b = jnp.sum(lp[2:3] * lp[3:4], axis=-1, keepdims=True)
    return jnp.exp(a) - jnp.exp(b) + lam_init


def _diff_attend(q, keys_b, vals_b, lam, stack):
    tq = q.shape[0]
    lane = lax.broadcasted_iota(jnp.int32, (1, 2 * DH_C), 1)
    qs = q * (DH_C ** -0.5 * LOG2E)

    def attend(qm):
        s = lax.dot_general(qm.astype(BF16), keys_b, NT_DIMS, preferred_element_type=F32)
        e = jnp.exp2(s - jnp.max(s, axis=-1, keepdims=True))
        den = jnp.sum(e, axis=-1, keepdims=True)
        return jnp.dot(e.astype(BF16), vals_b, preferred_element_type=F32) / den

    q_first, q_second = jnp.where(lane < DH_C, qs, 0.0), jnp.where(lane >= DH_C, qs, 0.0)
    if stack:
        o = attend(jnp.concatenate([q_first, q_second], axis=0))
        return o[:tq] - lam * o[tq:]
    return attend(q_first) - lam * attend(q_second)


def _attn_prompt_kernel(q_ref, k_ref, v_ref, lp_ref, norm_ref, o_ref, kc_ref, vc_ref, *, lam_init):
    hw = 2 * DH_C
    lam = _diff_lambda(lp_ref, lam_init)
    for h in range(N_HEADS):
        sl = slice(h * hw, (h + 1) * hw)
        k = k_ref[:, sl]
        v = v_ref[:, sl]
        o = _diff_attend(q_ref[:, sl], k.astype(BF16), v.astype(BF16), lam, stack=True)
        o_ref[:, sl] = _rms(o, norm_ref[:, sl]) * (1.0 - lam_init)
        kc_ref[h] = k
        vc_ref[h] = v


def _attn_prompt(proj, diff_lambda, diff_norm, lam_init):
    b, seq_len = proj.shape[0], proj.shape[1]
    hw = 2 * DH_C
    w = N_HEADS * hw
    kern = functools.partial(_attn_prompt_kernel, lam_init=lam_init)
    col = lambda j: pl.BlockSpec((None, seq_len, w), lambda i: (i, 0, j))
    cache_spec = pl.BlockSpec((None, None, N_HEADS, seq_len, hw), lambda i: (i, 0, 0, 0, 0))
    cache_shape = jax.ShapeDtypeStruct((b, 1, N_HEADS, seq_len, hw), F32)
    return pl.pallas_call(
        kern,
        grid=(b,),
        in_specs=[col(0), col(1), col(2),
                  pl.BlockSpec((4, DH_C), lambda i: (0, 0)),
                  pl.BlockSpec((1, w), lambda i: (0, 0))],
        out_specs=[pl.BlockSpec((None, seq_len, w), lambda i: (i, 0, 0)), cache_spec, cache_spec],
        out_shape=[jax.ShapeDtypeStruct((b, seq_len, w), F32), cache_shape, cache_shape],
        compiler_params=_params("arbitrary"),
    )(proj, proj, proj, diff_lambda, diff_norm.reshape(1, -1))


def _rope(x, cos, sin_signed):
    lane = lax.broadcasted_iota(jnp.int32, (1, x.shape[-1]), 1)
    first = (lane % 32) < 16
    partner = jnp.where(first, pltpu.roll(x, x.shape[-1] - 16, 1), pltpu.roll(x, 16, 1))
    return x * cos + partner * sin_signed


def _attn_sample_kernel(q_ref, k_ref, v_ref, ck_ref, cv_ref, cosq_ref, sinq_ref, cosk_ref, sink_ref,
                        lp_ref, norm_ref, o_ref, keys_scr, vals_scr, *, lam_init):
    past = ck_ref.shape[1]
    hw = 2 * DH_C

    @pl.when(pl.program_id(1) == 0)
    def _():
        for h in range(N_HEADS):
            sl = slice(h * hw, (h + 1) * hw)
            keys_scr[h, 0:past, :] = ck_ref[h].astype(BF16)
            vals_scr[h, 0:past, :] = cv_ref[h].astype(BF16)
            keys_scr[h, past:, :] = _rope(k_ref[:, sl], cosk_ref[...], sink_ref[...]).astype(BF16)
            vals_scr[h, past:, :] = v_ref[:, sl].astype(BF16)

    lam = _diff_lambda(lp_ref, lam_init)
    for h in range(N_HEADS):
        sl = slice(h * hw, (h + 1) * hw)
        q = _rope(q_ref[:, sl], cosq_ref[...], sinq_ref[...])
        o = _diff_attend(q, keys_scr[h], vals_scr[h], lam, stack=False)
        o_ref[:, sl] = _rms(o, norm_ref[:, sl]) * (1.0 - lam_init)


def _rope_tables(seq_len):
    pos = jnp.arange(seq_len)[:, None]
    half = DH_C // 2
    nfreq = half // 2
    inv = ROPE_BASE ** (-jnp.arange(0, half, 2, dtype=F32) / half)
    lane = jnp.arange(2 * DH_C)[None, :]
    p = jnp.where((lane % DH_C) < half, pos // GRID_W, pos % GRID_W).astype(F32)
    ang = p * inv[lane[0] % nfreq][None, :]
    return jnp.cos(ang), jnp.where((lane % half) < nfreq, -jnp.sin(ang), jnp.sin(ang))


def _attn_sample(proj, ctx_k, ctx_v, diff_lambda, diff_norm, lam_init, tq):
    b, seq_len = proj.shape[0], proj.shape[1]
    past = ctx_k.shape[2]
    hw = 2 * DH_C
    w = N_HEADS * hw
    cos, sin = _rope_tables(seq_len)
    kern = functools.partial(_attn_sample_kernel, lam_init=lam_init)
    ctx_spec = pl.BlockSpec((None, N_HEADS, past, hw), lambda i, j: (i, 0, 0, 0))
    return pl.pallas_call(
        kern,
        grid=(b, seq_len // tq),
        in_specs=[pl.BlockSpec((None, tq, w), lambda i, j: (i, j, 0)),
                  pl.BlockSpec((None, seq_len, w), lambda i, j: (i, 0, 1)),
                  pl.BlockSpec((None, seq_len, w), lambda i, j: (i, 0, 2)),
                  ctx_spec, ctx_spec,
                  pl.BlockSpec((tq, hw), lambda i, j: (j, 0)),
                  pl.BlockSpec((tq, hw), lambda i, j: (j, 0)),
                  pl.BlockSpec((seq_len, hw), lambda i, j: (0, 0)),
                  pl.BlockSpec((seq_len, hw), lambda i, j: (0, 0)),
                  pl.BlockSpec((4, DH_C), lambda i, j: (0, 0)),
                  pl.BlockSpec((1, w), lambda i, j: (0, 0))],
        out_specs=pl.BlockSpec((None, tq, w), lambda i, j: (i, j, 0)),
        out_shape=jax.ShapeDtypeStruct((b, seq_len, w), F32),
        scratch_shapes=[pltpu.VMEM((N_HEADS, past + seq_len, hw), BF16),
                        pltpu.VMEM((N_HEADS, past + seq_len, hw), BF16)],
        compiler_params=_params("arbitrary", "arbitrary"),
    )(proj, proj, proj, ctx_k, ctx_v, cos, sin, cos, sin, diff_lambda, diff_norm.reshape(1, -1))


def kernel(x_prompt, x_sample, state_hgrn, cache_diff_k, cache_diff_v, state_gla, c, c_ctx, ada_w, ada_b, norm_g, ffn_up, ffn_conv_w, ffn_conv_b, ffn_down, w_in_even, w_out_even, hgrn_lb, hgrn_norm, hy_conv_w, hy_conv_b, hy_w1, hy_b1, hy_w2, hy_b2, hy_w3, hy_freq, hy_d, w_in_odd, w_out_odd, diff_lambda, diff_norm, gla_aw, gla_ab, gla_norm):
    bp, lp, d = x_prompt.shape
    bs, ls, _ = x_sample.shape

    cvec_t = jnp.zeros((d, SUBLANES), F32).at[:, 0].set(c_ctx).at[:, 1:1 + bs].set(c.T)
    mod = _ada_mod(cvec_t, 1 + bs, ada_w, ada_b)

    yp = x_prompt.reshape(1, bp * lp, d)
    ys = x_sample
    tm = ROW_TILE

    filt = _hyena_filters((ls, lp), hy_w1[0], hy_b1[0], hy_w2[0], hy_b2[0], hy_w3[0], hy_freq[0])
    tab_p, tab_s = _dft_tables(lp), _dft_tables(ls)
    spec_s = _hyena_spectrum(filt, 0, ls, tab_s, SPECTRUM_FREQ_TILE)
    spec_p = _hyena_spectrum(filt, ls, lp, tab_p, lp)

    ffn_up_b, ffn_down_b = ffn_up.astype(BF16), ffn_down.astype(BF16)
    ffn_cb = ffn_conv_b.reshape(DEPTH, 1, -1)
    outs = {}
    for l in range(DEPTH):
        m = mod[l].reshape(SUBLANES, 6, 1, d)
        mp = [m[0:1, j] for j in range(6)]
        ms = [m[1:1 + bs, j] for j in range(6)]
        g = [norm_g[l, j].reshape(1, d) for j in range(4)]
        if l % 2 == 0:
            e = l // 2
            w_in = w_in_even[e].astype(BF16)
            w_out = w_out_even[e].astype(BF16)
            pp = _normmod_matmul(yp, g[0], mp[0], mp[1], w_in, tm).reshape(bp, lp, -1)
            ps = _normmod_matmul(ys, g[0], ms[0], ms[1], w_in, tm)
            oa_p, st_p = _hgrn_mixer(pp, hgrn_lb, hgrn_norm[e], None, l)
            oa_s, _ = _hgrn_mixer(ps, hgrn_lb, hgrn_norm[e], state_hgrn[:, e], l)
            ob_p = _hyena_mixer(pp, hy_conv_w[e], hy_conv_b[e], hy_d[e], spec_p, tab_p, lp)
            ob_s = _hyena_mixer(ps, hy_conv_w[e], hy_conv_b[e], hy_d[e], spec_s, tab_s, HYENA_FREQ_TILE)
            outs["hgrn"] = st_p
        else:
            o = l // 2
            lam_init = 0.8 - 0.6 * math.exp(-0.3 * l)
            pad_cols = -w_in_odd.shape[-1] % LANES
            w_in = jnp.pad(w_in_odd[o], ((0, 0), (0, pad_cols))).astype(BF16)
            w_out = w_out_odd[o].astype(BF16)
            kw = N_HEADS * DK_D
            aw = jnp.zeros((2, LANES, kw), F32)
            aw = aw.at[0, 0:GLA_RANK].set(gla_aw[o, 0]).at[1, GLA_RANK:2 * GLA_RANK].set(gla_aw[o, 1])
            aw_hi = aw.astype(BF16)
            aw_mid = (aw - aw_hi.astype(F32)).astype(BF16)
            aw = jnp.concatenate([aw_hi, aw_mid, aw_hi], axis=1)
            ab = gla_ab[o].reshape(2, 1, kw)
            s0 = state_gla[:, o]
            pp = _normmod_matmul(yp, g[0], mp[0], mp[1], w_in, tm).reshape(bp, lp, -1)
            ps = _normmod_matmul(ys, g[0], ms[0], ms[1], w_in, tm)
            oa_p, kc, vc = _attn_prompt(pp, diff_lambda[o], diff_norm[o], lam_init)
            oa_s = _attn_sample(ps, cache_diff_k[:, o], cache_diff_v[:, o], diff_lambda[o], diff_norm[o], lam_init, ATTN_Q_TILE)
            ob_p, st_p = _gla_mixer(pp, aw, ab, gla_norm[o], None)
            ob_s, _ = _gla_mixer(ps, aw, ab, gla_norm[o], s0)
            outs["k"], outs["v"], outs["gla"] = kc, vc, st_p
        ffn_args = (ffn_up_b, ffn_conv_w, ffn_cb, ffn_down_b, l)
        yp = _mix_ffn(yp, oa_p.reshape(1, bp * lp, -1), ob_p.reshape(1, bp * lp, -1), w_out, mp[2], g[1],
                      g[2], mp[3], mp[4], *ffn_args, mp[5], g[3], lp, tm, FFN_COL_TILE)
        ys = _mix_ffn(ys, oa_s, ob_s, w_out, ms[2], g[1], g[2], ms[3], ms[4], *ffn_args, ms[5], g[3], ls, tm, FFN_COL_TILE)

    return (yp.reshape(bp, lp, d), ys, outs["hgrn"], outs["k"], outs["v"], outs["gla"])
```

```python
import functools
import math

import jax
import jax.numpy as jnp
import numpy as np
from jax import lax
from jax.experimental import pallas as pl
from jax.experimental.pallas import tpu as pltpu

F32 = jnp.float32
BF16 = jnp.bfloat16
HIGHEST = lax.Precision.HIGHEST

D_MODEL = 1024
DEPTH = 2
GRID_W = 64
N_HEADS = 4
HEAD_W = 128
MIX_W = N_HEADS * HEAD_W
W_B = 512
HY_ORDER = 2
HY_EMB = 33
HY_BANDS = (HY_EMB - 1) // 2
HY_FF = 64
HY_TARGET = 1e-2
HY_FAST = 0.3
HY_SLOW = 1.5
DH_C = 64
DK_D = 64
GLA_RANK = 16
HEADS_PER_GROUP = 2
GLA_TAU = 16.0
ROPE_BASE = 10000.0
D_FF = 2816
EPS = 1e-6

LANES = 128
SUBLANES = 8
VMEM_LIMIT = 56 * 1024 * 1024
ROW_TILE = 512
FFN_COL_TILE = D_FF // 2
HYENA_FREQ_TILE = 256
SPECTRUM_FREQ_TILE = 512
ATTN_Q_TILE = 256
SCAN_CHUNK = 64
SCAN_ROWS = 256
SCAN_GROUP = 2
NT_DIMS = (((1,), (1,)), ((), ()))
LOG2E = 1.4426950408889634


def _params(*sem):
    return pltpu.CompilerParams(dimension_semantics=sem, vmem_limit_bytes=VMEM_LIMIT)


def _silu(x):
    return x * (1.0 / (1.0 + jnp.exp(-x)))


def _rms(x, g):
    return x * lax.rsqrt(jnp.mean(x * x, axis=-1, keepdims=True) + EPS) * g


def _ada_kernel(c_ref, w_ref, b_ref, o_ref, *, n_rows):
    s = _silu(c_ref[...])
    w = w_ref[...]
    rows = [jnp.sum(w * s[:, r:r + 1], axis=0, keepdims=True) for r in range(n_rows)]
    rows.append(jnp.zeros((SUBLANES - n_rows, w.shape[1]), F32))
    o_ref[...] = jnp.concatenate(rows, axis=0) + b_ref[...]


def _ada_mod(cvec_t, n_rows, ada_w, ada_b):
    n = ada_w.shape[-1]
    tn = 1536
    return pl.pallas_call(
        functools.partial(_ada_kernel, n_rows=n_rows),
        grid=(DEPTH, n // tn),
        in_specs=[pl.BlockSpec((D_MODEL, SUBLANES), lambda l, j: (0, 0)),
                  pl.BlockSpec((None, D_MODEL, tn), lambda l, j: (l, 0, j)),
                  pl.BlockSpec((None, 1, tn), lambda l, j: (l, 0, j))],
        out_specs=pl.BlockSpec((None, SUBLANES, tn), lambda l, j: (l, 0, j)),
        out_shape=jax.ShapeDtypeStruct((DEPTH, SUBLANES, n), F32),
        compiler_params=_params("arbitrary", "arbitrary"),
    )(cvec_t, ada_w, ada_b.reshape(DEPTH, 1, n))


def _normmod_matmul_kernel(x_ref, g_ref, sh_ref, sc_ref, w_ref, o_ref):
    h = _rms(x_ref[...], g_ref[...]) * (1.0 + sc_ref[...]) + sh_ref[...]
    o_ref[...] = jnp.dot(h.astype(BF16), w_ref[...], preferred_element_type=F32)


def _mod_index(n_mod):
    return (lambda b, i: (b, 0, 0)) if n_mod > 1 else (lambda b, i: (0, 0, 0))


def _normmod_matmul(x, g, shift, scale, w, tm):
    b, l, d = x.shape
    n = w.shape[1]
    mod_spec = pl.BlockSpec((None, 1, d), _mod_index(shift.shape[0]))
    return pl.pallas_call(
        _normmod_matmul_kernel,
        grid=(b, l // tm),
        in_specs=[pl.BlockSpec((None, tm, d), lambda b, i: (b, i, 0)),
                  pl.BlockSpec((1, d), lambda b, i: (0, 0)),
                  mod_spec, mod_spec,
                  pl.BlockSpec((d, n), lambda b, i: (0, 0))],
        out_specs=pl.BlockSpec((None, tm, n), lambda b, i: (b, i, 0)),
        out_shape=jax.ShapeDtypeStruct((b, l, n), F32),
        compiler_params=_params("arbitrary", "arbitrary"),
    )(x, g, shift, scale, w)


def _patch_rows(x, keep, starts):
    pieces, r = [], 0
    for s in starts:
        if s > r:
            pieces.append(x[r:s])
        pieces.append(jnp.where(keep[s:s + SUBLANES], x[s:s + SUBLANES], 0.0))
        r = s + SUBLANES
    if r < x.shape[0]:
        pieces.append(x[r:])
    return jnp.concatenate(pieces, axis=0)


def _mix_ffn_kernel(y_ref, yp_ref, yn_ref, a_ref, ap_ref, an_ref, b_ref, bp_ref, bn_ref, wo_ref, gate1_ref, g1_ref,
                    g2_ref, sh_ref, sc_ref, ua_ref, ug_ref, cwa_ref, cwg_ref, cba_ref, cbg_ref, dn_ref, gate2_ref,
                    g3_ref, o_ref, y1_scr, h_scr, acc_scr, *, seq_len):
    i = pl.program_id(1)
    f = pl.program_id(2)
    tm = y_ref.shape[0]
    halo = yp_ref.shape[0]
    half = a_ref.shape[1]

    @pl.when(f == 0)
    def _():
        rows_of = lambda p, m, n: jnp.concatenate([p[...], m[...], n[...]], axis=0)
        a_all = rows_of(ap_ref, a_ref, an_ref).astype(BF16)
        b_all = rows_of(bp_ref, b_ref, bn_ref).astype(BF16)
        m = jnp.dot(a_all, wo_ref[:half, :], preferred_element_type=F32)
        m = m + jnp.dot(b_all, wo_ref[half:, :], preferred_element_type=F32)
        y1 = rows_of(yp_ref, y_ref, yn_ref) + gate1_ref[...] * _rms(m, g1_ref[...])
        y1_scr[...] = y1[halo:halo + tm]
        h_scr[...] = (_rms(y1, g2_ref[...]) * (1.0 + sc_ref[...]) + sh_ref[...]).astype(BF16)
        acc_scr[...] = jnp.zeros_like(acc_scr)

    rows = tm + 2 * halo
    pos = (i * tm + lax.broadcasted_iota(jnp.int32, (tm, 1), 0)) % seq_len
    has_prev = pos != 0
    has_next = pos != seq_len - 1
    period = math.gcd(tm, seq_len)
    first_groups = list(range(0, tm, period))
    last_groups = [s + period - SUBLANES for s in first_groups]

    def conv(u_ref, cw_ref, cb_ref):
        u = jnp.dot(h_scr[...], u_ref[...], preferred_element_type=F32)
        up = _patch_rows(pltpu.roll(u, 1, 0)[halo:halo + tm], has_prev, first_groups)
        un = _patch_rows(pltpu.roll(u, rows - 1, 0)[halo:halo + tm], has_next, last_groups)
        uc = u[halo:halo + tm]
        return up * cw_ref[0:1, :] + uc * cw_ref[1:2, :] + un * cw_ref[2:3, :] + cb_ref[...]

    a = conv(ua_ref, cwa_ref, cba_ref)
    gt = conv(ug_ref, cwg_ref, cbg_ref)
    act = (_silu(gt) * a).astype(BF16)
    acc_scr[...] += jnp.dot(act, dn_ref[...], preferred_element_type=F32)

    @pl.when(f == pl.num_programs(2) - 1)
    def _():
        o_ref[...] = y1_scr[...] + gate2_ref[...] * _rms(acc_scr[...], g3_ref[...])


def _mix_ffn(y, a, bm, w_out, gate1, g1, g2, shift, scale, up, cw, cb, down, layer, gate2, g3, seq_len, tm, tf):
    b, l, d = y.shape
    wa = a.shape[-1]
    nf = D_FF // tf
    halo = SUBLANES
    hb = tm // halo
    last_hb = l // halo - 1
    n_mod = shift.shape[0]
    mod_spec = pl.BlockSpec((None, 1, d), (lambda b, i, f: (b, 0, 0)) if n_mod > 1 else (lambda b, i, f: (0, 0, 0)))
    vec = lambda off: pl.BlockSpec((None, 1, tf), lambda b, i, f: (layer, 0, off + f))
    row_d = pl.BlockSpec((1, d), lambda b, i, f: (0, 0))

    def tiles(w):
        return [pl.BlockSpec((None, tm, w), lambda b, i, f: (b, i, 0)),
                pl.BlockSpec((None, halo, w), lambda b, i, f: (b, jnp.maximum(i * hb - 1, 0), 0)),
                pl.BlockSpec((None, halo, w), lambda b, i, f: (b, jnp.minimum((i + 1) * hb, last_hb), 0))]

    kern = functools.partial(_mix_ffn_kernel, seq_len=seq_len)
    return pl.pallas_call(
        kern,
        grid=(b, l // tm, nf),
        in_specs=tiles(d) + tiles(wa) + tiles(wa) + [
                  pl.BlockSpec((2 * wa, d), lambda b, i, f: (0, 0)),
                  mod_spec, row_d,
                  row_d, mod_spec, mod_spec,
                  pl.BlockSpec((None, d, tf), lambda b, i, f: (layer, 0, f)),
                  pl.BlockSpec((None, d, tf), lambda b, i, f: (layer, 0, nf + f)),
                  pl.BlockSpec((None, 3, tf), lambda b, i, f: (layer, 0, f)),
                  pl.BlockSpec((None, 3, tf), lambda b, i, f: (layer, 0, nf + f)),
                  vec(0), vec(nf),
                  pl.BlockSpec((None, tf, d), lambda b, i, f: (layer, f, 0)),
                  mod_spec, row_d],
        out_specs=pl.BlockSpec((None, tm, d), lambda b, i, f: (b, i, 0)),
        out_shape=jax.ShapeDtypeStruct((b, l, d), F32),
        scratch_shapes=[pltpu.VMEM((tm, d), F32), pltpu.VMEM((tm + 2 * halo, d), BF16), pltpu.VMEM((tm, d), F32)],
        compiler_params=_params("arbitrary", "arbitrary", "arbitrary"),
    )(y, y, y, a, a, a, bm, bm, bm, w_out, gate1, g1, g2, shift, scale, up, up, cw, cw, cb, cb, down, gate2, g3)


def _scan_tables(c, heads_per_group):
    nlev = int(math.log2(c))
    t = np.arange(c)[:, None]
    r = np.arange(c)[None, :]
    masks_f = [np.eye(c, dtype=bool)]
    for lev in range(1, nlev + 1):
        bsz = 2 ** lev
        mid = (t // bsz) * bsz + bsz // 2
        masks_f.append(((t // bsz) == (r // bsz)) & (t >= mid) & (r < mid))
    m_f = np.stack(masks_f).astype(np.float32)
    m_b = np.transpose(m_f, (0, 2, 1))
    tri = lambda a: jnp.asarray(np.concatenate([a, a, a], axis=1).astype(np.float32), dtype=BF16)
    rep = lambda m: jnp.asarray(np.concatenate([m] * heads_per_group, axis=2))
    return tri(r <= t), tri(r >= t), rep(m_f), rep(m_b)


def _level_decay(cum, ncum, lg2, lev, bwd):
    c, w = cum.shape
    b = 1 << lev
    half = b // 2
    off = half - 1 + int(bwd)
    if b == 2:
        odd = lax.broadcasted_iota(jnp.int32, (c, 1), 0) % 2 == 1
        return jnp.where(odd != bwd, lg2, 0.0)
    if b >= 2 * SUBLANES:
        pieces = []
        for b0 in range(0, c, b):
            for rows, is_upper in ((slice(b0, b0 + half), False), (slice(b0 + half, b0 + b), True)):
                src = cum if is_upper != bwd else ncum
                pieces.append(src[rows] - jnp.broadcast_to(src[b0 + off:b0 + off + 1], (half, w)))
        return jnp.concatenate(pieces, axis=0)
    cum3 = cum.reshape(c // SUBLANES, SUBLANES, w)
    sub = lax.broadcasted_iota(jnp.int32, (1, SUBLANES, 1), 1)
    if b == SUBLANES:
        ref3 = jnp.broadcast_to(cum3[:, off:off + 1], cum3.shape)
    else:
        ref3 = jnp.where(sub < b, cum3[:, off:off + 1], cum3[:, b + off:b + off + 1])
    upper = (sub % b) >= half
    sgn = jnp.where(upper != bwd, 1.0, -1.0)
    return ((cum3 - ref3) * sgn).reshape(c, w)


def _head_stack(xb, width):
    n = xb.shape[1] // width
    lane = lax.broadcasted_iota(jnp.int32, (1, xb.shape[1]), 1)
    zero = jnp.zeros_like(xb)
    return jnp.concatenate([jnp.where((lane >= j * width) & (lane < (j + 1) * width), xb, zero) for j in range(n)],
                           axis=0)


def _scan_group(chunks):
    dk = chunks[0][0].shape[1] // N_HEADS
    hpg = HEADS_PER_GROUP
    gw = hpg * dk
    ngroups = N_HEADS // hpg
    vw = hpg * HEAD_W
    work = []
    for q, k, v, lg2, tri_ref, m_ref, st_ref, bwd in chunks:
        hi = lg2.astype(BF16)
        r1 = lg2 - hi.astype(F32)
        mid = r1.astype(BF16)
        lo = (r1 - mid.astype(F32)).astype(BF16)
        cum = jnp.dot(tri_ref[...], jnp.concatenate([hi, mid, lo], axis=0), preferred_element_type=F32)
        work.append(dict(q=q, k=k, v=v, lg2=lg2, cum=cum, ncum=-cum, m_ref=m_ref, st_ref=st_ref, bwd=bwd,
                         q16=q.astype(BF16), k16=k.astype(BF16), att=[None] * ngroups))
    nlev = chunks[0][5].shape[0] - 1
    for lev in range(1, nlev + 1):
        for w in work:
            e = jnp.exp2(_level_decay(w["cum"], w["ncum"], w["lg2"], lev, w["bwd"])).astype(BF16)
            qb, kb = w["q16"] * e, w["k16"] * e
            for g in range(ngroups):
                sl = slice(g * gw, (g + 1) * gw)
                prod = lax.dot_general(qb[:, sl], _head_stack(kb[:, sl], dk), NT_DIMS,
                                       preferred_element_type=F32)
                term = w["m_ref"][lev] * prod
                w["att"][g] = term if w["att"][g] is None else w["att"][g] + term
    lane = lax.broadcasted_iota(jnp.int32, (1, gw), 1)
    results = []
    for w in work:
        q, k, v, cum, st_ref = w["q"], w["k"], w["v"], w["cum"], w["st_ref"]
        c = q.shape[0]
        last = 0 if w["bwd"] else c - 1
        e_cum = jnp.exp2(cum)
        d_last = e_cum[last:last + 1, :]
        qe = (q * e_cum).astype(BF16)
        kd = (k * jnp.exp2(cum[last:last + 1, :] - cum)).astype(BF16)
        vb = v.astype(BF16)
        qk = q * k
        diag = []
        for h in range(N_HEADS):
            slab = qk[:, (h * dk // LANES) * LANES:(h * dk // LANES + max(dk // LANES, 1)) * LANES]
            if dk < LANES:
                slab_lane = lax.broadcasted_iota(jnp.int32, (1, LANES), 1)
                off = (h * dk) % LANES
                slab = jnp.where((slab_lane >= off) & (slab_lane < off + dk), slab, 0.0)
            diag.append(jnp.sum(slab, axis=1, keepdims=True) * v[:, h * HEAD_W:(h + 1) * HEAD_W])
        outs = []
        for g in range(ngroups):
            sl = slice(g * gw, (g + 1) * gw)
            st = st_ref[g]
            stb = st.astype(BF16)
            o_g = jnp.dot(w["att"][g].astype(BF16), _head_stack(vb[:, g * vw:(g + 1) * vw], HEAD_W),
                          preferred_element_type=F32)
            qe_g = qe[:, sl]
            zero = jnp.zeros_like(qe_g)
            inter = [lax.dot_general(jnp.where((lane >= j * dk) & (lane < (j + 1) * dk), qe_g, zero), stb, NT_DIMS,
                                     preferred_element_type=F32) for j in range(hpg)]
            outs.append(o_g + jnp.concatenate(inter, axis=1) + jnp.concatenate(diag[g * hpg:(g + 1) * hpg], axis=1))
            vstack = jnp.concatenate([v[:, g * vw + j * HEAD_W:g * vw + (j + 1) * HEAD_W] for j in range(hpg)],
                                     axis=0)
            st_ref[g] = st * d_last[:, sl] + jnp.dot(vstack.T.astype(BF16), _head_stack(kd[:, sl], dk),
                                                      preferred_element_type=F32)
        results.append(jnp.concatenate(outs, axis=1))
    return results


def _bidir_scan_body(load_fwd, load_bwd, g_ref, s0_ref, norm_ref, af_ref, ab_ref, mf_ref, mb_ref,
                     o_ref, s_out_ref, of_scr, ob_scr, stf_scr, stb_scr, seq_len):
    c = SCAN_CHUNK
    j = pl.program_id(1)
    nblk = pl.num_programs(1)
    n = SCAN_ROWS // c
    ngroups, _, gw = stf_scr.shape
    hpg = N_HEADS // ngroups
    dk = gw // hpg

    @pl.when(j == 0)
    def _():
        for g in range(ngroups):
            if s0_ref is None:
                stf_scr[g] = jnp.zeros((HEAD_W, gw), F32)
                stb_scr[g] = jnp.zeros((HEAD_W, gw), F32)
            else:
                stf_scr[g] = jnp.concatenate([s0_ref[0, g * hpg + j].T for j in range(hpg)], axis=1)
                stb_scr[g] = jnp.concatenate([s0_ref[1, g * hpg + j].T for j in range(hpg)], axis=1)

    base_f = j * SCAN_ROWS
    base_b = (nblk - 1 - j) * SCAN_ROWS

    def step(i, carry):
        chunks, stores = [], []
        for u in range(SCAN_GROUP):
            rf = pl.multiple_of((i * SCAN_GROUP + u) * c, c)
            chunks.append(load_fwd(rf) + (af_ref, mf_ref, stf_scr, False))
            stores.append((of_scr, pl.multiple_of(base_f + rf, c)))
        for u in range(SCAN_GROUP):
            rb = pl.multiple_of((n - 1 - i * SCAN_GROUP - u) * c, c)
            chunks.append(load_bwd(rb) + (ab_ref, mb_ref, stb_scr, True))
            stores.append((ob_scr, pl.multiple_of(base_b + rb, c)))
        for (scr, r0), o in zip(stores, _scan_group(chunks)):
            scr[pl.ds(r0, c), :] = o
        return carry

    lax.fori_loop(0, n // SCAN_GROUP, step, 0)

    @pl.when(j == nblk - 1)
    def _():
        def fin(jj, carry):
            r0 = pl.multiple_of(jj * SCAN_ROWS, SCAN_ROWS)
            o = of_scr[pl.ds(r0, SCAN_ROWS), :] + ob_scr[pl.ds(r0, SCAN_ROWS), :]
            parts = []
            for h in range(N_HEADS):
                sl = slice(h * HEAD_W, (h + 1) * HEAD_W)
                parts.append(_rms(o[:, sl], norm_ref[:, sl]))
            o_ref[pl.ds(r0, SCAN_ROWS), :] = jnp.concatenate(parts, axis=1) * _silu(g_ref[pl.ds(r0, SCAN_ROWS), :])
            return carry

        lax.fori_loop(0, seq_len // SCAN_ROWS, fin, 0)
        for h in range(N_HEADS):
            g, hs = h // hpg, slice((h % hpg) * dk, (h % hpg + 1) * dk)
            s_out_ref[0, h] = stf_scr[g][:, hs].T
            s_out_ref[1, h] = stb_scr[g][:, hs].T


def _hgrn_kernel(*refs, layer, has_s0, seq_len):
    qf_ref, ff_ref, if_ref, qb_ref, fb_ref, ib_ref, g_ref, lbraw_ref, norm_ref, af_ref, ab_ref, mf_ref, mb_ref = refs[:13]
    s0_ref = refs[13] if has_s0 else None
    o_ref, s_out_ref, of_scr, ob_scr, stf_scr, stb_scr = refs[13 + has_s0:]
    c = SCAN_CHUNK
    raw = lbraw_ref[...]
    ex = jnp.exp(raw - jnp.max(raw, axis=0, keepdims=True))
    sm = ex / jnp.sum(ex, axis=0, keepdims=True)
    lb = sm[0]
    for j in range(1, layer + 1):
        lb = lb + sm[j]

    def load(q_ref, f_ref, i_ref, lb_row):
        def fn(r0):
            q = _silu(q_ref[pl.ds(r0, c), :]) * (HEAD_W ** -0.5)
            sig = 1.0 / (1.0 + jnp.exp(-f_ref[pl.ds(r0, c), :]))
            f = lb_row + (1.0 - lb_row) * sig
            return q, 1.0 - f, i_ref[pl.ds(r0, c), :], jnp.log2(f)
        return fn

    _bidir_scan_body(load(qf_ref, ff_ref, if_ref, lb[0:1]), load(qb_ref, fb_ref, ib_ref, lb[1:2]), g_ref,
                     s0_ref, norm_ref, af_ref, ab_ref, mf_ref, mb_ref,
                     o_ref, s_out_ref, of_scr, ob_scr, stf_scr, stb_scr, seq_len)


def _gla_kernel(*refs, has_s0, seq_len):
    fwd_refs, bwd_refs = refs[0:4], refs[4:8]
    g_ref, aw_ref, ab_ref, norm_ref, af_ref, abk_ref, mf_ref, mb_ref = refs[8:16]
    s0_ref = refs[16] if has_s0 else None
    o_ref, s_out_ref, of_scr, ob_scr, stf_scr, stb_scr = refs[16 + has_s0:]
    c = SCAN_CHUNK

    def load(d, q_ref, k_ref, v_ref, da_ref):
        def fn(r0):
            q = q_ref[pl.ds(r0, c), :] * (DK_D ** -0.5)
            da = da_ref[pl.ds(r0, c), :]
            da_hi = da.astype(BF16)
            da_mid = (da - da_hi.astype(F32)).astype(BF16)
            xa = jnp.dot(jnp.concatenate([da_hi, da_hi, da_mid], axis=1), aw_ref[d],
                         preferred_element_type=F32) + ab_ref[d]
            la = (jnp.minimum(xa, 0.0) - jnp.log(1.0 + jnp.exp(-jnp.abs(xa)))) * (LOG2E / GLA_TAU)
            return q, k_ref[pl.ds(r0, c), :], v_ref[pl.ds(r0, c), :], la
        return fn

    _bidir_scan_body(load(0, *fwd_refs), load(1, *bwd_refs), g_ref,
                     s0_ref, norm_ref, af_ref, abk_ref, mf_ref, mb_ref,
                     o_ref, s_out_ref, of_scr, ob_scr, stf_scr, stb_scr, seq_len)


def _scan_call(kern, proj, streams, extra, s0, seq_len, dk):
    b = proj.shape[0]
    nblk = seq_len // SCAN_ROWS
    hpg = HEADS_PER_GROUP
    tabs = _scan_tables(SCAN_CHUNK, hpg)
    full = lambda a: pl.BlockSpec(a.shape, lambda i, j, _n=a.ndim: (0,) * _n)
    in_specs = []
    for cb, w, kind in streams:
        if kind == "f":
            in_specs.append(pl.BlockSpec((None, SCAN_ROWS, w), lambda i, j, _c=cb: (i, j, _c)))
        elif kind == "b":
            in_specs.append(pl.BlockSpec((None, SCAN_ROWS, w), lambda i, j, _c=cb: (i, nblk - 1 - j, _c)))
        else:
            in_specs.append(pl.BlockSpec((None, seq_len, w), lambda i, j, _c=cb: (i, 0, _c)))
    args = [proj] * len(streams)
    for a in tuple(extra) + tabs:
        in_specs.append(full(a))
        args.append(a)
    if s0 is not None:
        in_specs.append(pl.BlockSpec((None, 2, N_HEADS, dk, HEAD_W), lambda i, j: (i, 0, 0, 0, 0)))
        args.append(s0)
    return pl.pallas_call(
        kern,
        grid=(b, nblk),
        in_specs=in_specs,
        out_specs=[pl.BlockSpec((None, seq_len, MIX_W), lambda i, j: (i, 0, 0)),
                   pl.BlockSpec((None, None, 2, N_HEADS, dk, HEAD_W), lambda i, j: (i, 0, 0, 0, 0, 0))],
        out_shape=[jax.ShapeDtypeStruct((b, seq_len, MIX_W), F32),
                   jax.ShapeDtypeStruct((b, 1, 2, N_HEADS, dk, HEAD_W), F32)],
        scratch_shapes=[pltpu.VMEM((seq_len, MIX_W), F32), pltpu.VMEM((seq_len, MIX_W), F32),
                        pltpu.VMEM((N_HEADS // hpg, HEAD_W, hpg * dk), F32),
                        pltpu.VMEM((N_HEADS // hpg, HEAD_W, hpg * dk), F32)],
        compiler_params=_params("arbitrary", "arbitrary"),
    )(*args)


def _hgrn_mixer(proj, hgrn_lb, norm, s0, layer):
    seq_len = proj.shape[1]
    kern = functools.partial(_hgrn_kernel, layer=layer, has_s0=s0 is not None, seq_len=seq_len)
    streams = [(0, MIX_W, "f"), (1, MIX_W, "f"), (3, MIX_W, "f"),
               (0, MIX_W, "b"), (2, MIX_W, "b"), (3, MIX_W, "b"), (4, MIX_W, "w")]
    return _scan_call(kern, proj, streams, (hgrn_lb, norm.reshape(1, MIX_W)), s0, seq_len, HEAD_W)


def _gla_mixer(proj, aw, ab, norm, s0):
    seq_len = proj.shape[1]
    kern = functools.partial(_gla_kernel, has_s0=s0 is not None, seq_len=seq_len)
    kw = N_HEADS * DK_D
    streams = [(1536 // kw, kw, "f"), (1792 // kw, kw, "f"), (2048 // MIX_W, MIX_W, "f"), (3072 // LANES, LANES, "f"),
               (1536 // kw, kw, "b"), (1792 // kw, kw, "b"), (2048 // MIX_W, MIX_W, "b"), (3072 // LANES, LANES, "b"),
               (2560 // MIX_W, MIX_W, "w")]
    return _scan_call(kern, proj, streams, (aw, ab, norm.reshape(1, MIX_W)), s0, seq_len, DK_D)


def _dwconv3_rows(x, w_ref, b_ref):
    l = x.shape[0]
    row = lax.broadcasted_iota(jnp.int32, (l, 1), 0)
    xp = _patch_rows(pltpu.roll(x, 1, 0), row != 0, [0])
    xn = _patch_rows(pltpu.roll(x, l - 1, 0), row != l - 1, [l - SUBLANES])
    return xp * w_ref[0:1, :] + x * w_ref[1:2, :] + xn * w_ref[2:3, :] + b_ref[...]


def _hyena_filter_kernel(z_ref, w1_ref, b1_ref, w2_ref, b2_ref, w3_ref, fr_ref, dl_ref, o_ref):
    z = z_ref[...]
    fr = fr_ref[...]
    h = jnp.sin(fr * (jnp.dot(z, w1_ref[...], preferred_element_type=F32, precision=HIGHEST) + b1_ref[...]))
    h = jnp.sin(fr * (jnp.dot(h, w2_ref[...], preferred_element_type=F32, precision=HIGHEST) + b2_ref[...]))
    h_hi = h.astype(BF16)
    h_mid = (h - h_hi.astype(F32)).astype(BF16)
    h = jnp.dot(jnp.concatenate([h_hi, h_hi, h_mid], axis=1), w3_ref[...],
                preferred_element_type=F32)
    win = jnp.exp(-z[:, 0:1] * dl_ref[...])
    o_ref[...] = h * jnp.concatenate([win] * (2 * HY_ORDER), axis=1)


def _hyena_pos_features(l):
    t = jnp.linspace(0.0, 1.0, l, dtype=F32)[:, None]
    w = 2.0 * math.pi * jnp.arange(l, dtype=F32)[:, None] / l
    fb = jnp.linspace(1e-4, HY_BANDS - 1, HY_BANDS, dtype=F32)[None]
    z = jnp.concatenate([t, jnp.cos(fb * w), -jnp.sin(fb * w)], axis=-1)
    return jnp.pad(z, ((0, 0), (0, HY_FF - HY_EMB)))


def _hyena_filters(lens, w1, b1, w2, b2, w3, freq):
    z = jnp.concatenate([_hyena_pos_features(l) for l in lens], axis=0)
    rows = z.shape[0]
    tr = 256
    w1p = jnp.pad(w1, ((0, HY_FF - HY_EMB), (0, 0)))
    max_decay = math.log(HY_TARGET) / HY_FAST
    min_decay = math.log(HY_TARGET) / HY_SLOW
    deltas = jnp.abs(jnp.linspace(min_decay, max_decay, W_B, dtype=F32))[None]
    nout = w3.shape[1]
    w3_hi = w3.astype(BF16)
    w3_mid = (w3 - w3_hi.astype(F32)).astype(BF16)
    w3s = jnp.concatenate([w3_hi, w3_mid, w3_hi], axis=0)
    full = lambda a: pl.BlockSpec(a.shape, lambda i, _n=a.ndim: (0,) * _n)
    ins = (w1p, b1.reshape(1, -1), w2, b2.reshape(1, -1), w3s, freq.reshape(1, -1), deltas)
    return pl.pallas_call(
        _hyena_filter_kernel,
        grid=(rows // tr,),
        in_specs=[pl.BlockSpec((tr, HY_FF), lambda i: (i, 0))] + [full(a) for a in ins],
        out_specs=pl.BlockSpec((tr, nout), lambda i: (i, 0)),
        out_shape=jax.ShapeDtypeStruct((rows, nout), F32),
        compiler_params=_params("arbitrary"),
    )(z, *ins)


def _dft_table_kernel(ca_ref, sa_ref, cb_ref, sb_ref, cos_ref, sinf_ref, sini_ref):
    tk = ca_ref.shape[0]
    ca, sa = ca_ref[...], sa_ref[...]
    row = pl.program_id(0) * tk + lax.broadcasted_iota(jnp.int32, (tk, 1), 0)
    lane = lax.broadcasted_iota(jnp.int32, (1, LANES), 1)
    alt_row = jnp.where(row % 2 == 0, 1.0, -1.0)
    for grp in range(cb_ref.shape[1]):
        cols = slice(grp * LANES, (grp + 1) * LANES)
        cbg, sbg = cb_ref[:, grp:grp + 1], sb_ref[:, grp:grp + 1]
        sin_t = sa * cbg + ca * sbg
        col = grp * LANES + lane
        cos_ref[:, cols] = (ca * cbg - sa * sbg).astype(BF16)
        sinf_ref[:, cols] = jnp.where(row == 0, jnp.where(col % 2 == 0, 1.0, -1.0), sin_t).astype(BF16)
        sini_ref[:, cols] = jnp.where(col == 0, alt_row, sin_t).astype(BF16)


def _dft_tables(l):
    n = 2 * l
    k = jnp.arange(l, dtype=jnp.int32)[:, None]
    t1 = jnp.arange(LANES, dtype=jnp.int32)[None, :]
    t2 = (jnp.arange(l // LANES, dtype=jnp.int32) * LANES)[None, :]
    ang = lambda m: (m % n).astype(F32) * (2.0 * math.pi / n)
    small = (jnp.cos(ang(k * t1)), jnp.sin(ang(k * t1)), jnp.cos(ang(k * t2)), jnp.sin(ang(k * t2)))
    tk = min(l, 256)
    out = jax.ShapeDtypeStruct((l, l), BF16)
    ospec = pl.BlockSpec((tk, l), lambda i: (i, 0))
    return pl.pallas_call(
        _dft_table_kernel,
        grid=(l // tk,),
        in_specs=[pl.BlockSpec((tk, a.shape[1]), lambda i: (i, 0)) for a in small],
        out_specs=[ospec, ospec, ospec],
        out_shape=[out, out, out],
        compiler_params=_params("arbitrary"),
    )(*small)


def _hyena_spectrum_kernel(c_ref, s_ref, f0_ref, f1_ref, kr_ref, kia_ref, krb_ref, fs_scr, fd_scr, nyq_scr, *,
                           seq_len):
    kt = pl.program_id(1)
    tk = c_ref.shape[0]

    @pl.when(kt == 0)
    def _():
        row = lax.broadcasted_iota(jnp.int32, (seq_len, 1), 0)
        f0 = f0_ref[...]
        f1 = jnp.where(row != 0, f1_ref[...], 0.0)
        fsum = f0 + f1
        fs_scr[...] = fsum.astype(BF16)
        fd_scr[...] = (f1 - f0).astype(BF16)
        nyq_scr[...] = jnp.sum(jnp.where(row % 2 == 0, fsum, -fsum), axis=0, keepdims=True)

    kr = jnp.dot(c_ref[...], fs_scr[...], preferred_element_type=F32)
    ki = jnp.dot(s_ref[...], fd_scr[...], preferred_element_type=F32)
    krow = kt * tk + lax.broadcasted_iota(jnp.int32, (tk, 1), 0)
    dc = krow == 0
    wk = jnp.where(dc, 1.0, 2.0) * (1.0 / (2 * seq_len))
    kr_ref[...] = kr * wk
    kia_ref[...] = jnp.where(dc, 0.0, ki) * wk
    krb_ref[...] = jnp.where(dc, nyq_scr[...], kr) * wk


def _hyena_spectrum(filt, row0, seq_len, tables, tk):
    cos_t, sin_f, _ = tables
    rb = row0 // seq_len
    out = jax.ShapeDtypeStruct((HY_ORDER, seq_len, W_B), F32)
    kern = functools.partial(_hyena_spectrum_kernel, seq_len=seq_len)
    ospec = pl.BlockSpec((None, tk, W_B), lambda o, kt: (o, kt, 0))
    return pl.pallas_call(
        kern,
        grid=(HY_ORDER, seq_len // tk),
        in_specs=[pl.BlockSpec((tk, seq_len), lambda o, kt: (kt, 0)),
                  pl.BlockSpec((tk, seq_len), lambda o, kt: (kt, 0)),
                  pl.BlockSpec((seq_len, W_B), lambda o, kt: (rb, 2 * o)),
                  pl.BlockSpec((seq_len, W_B), lambda o, kt: (rb, 2 * o + 1))],
        out_specs=[ospec, ospec, ospec],
        out_shape=[out, out, out],
        scratch_shapes=[pltpu.VMEM((seq_len, W_B), BF16), pltpu.VMEM((seq_len, W_B), BF16),
                        pltpu.VMEM((1, W_B), F32)],
        compiler_params=_params("arbitrary", "arbitrary"),
    )(cos_t, sin_f, filt, filt)


def _hyena_order_kernel(zin_ref, gate_ref, cwz_ref, cbz_ref, cwg_ref, cbg_ref, d_ref, kr_ref, kia_ref, krb_ref,
                        cf_ref, sf_ref, ci_ref, si_ref, o_ref, z_scr, zb_scr, acc_scr, *, conv_input):
    kt = pl.program_id(1)

    @pl.when(kt == 0)
    def _():
        z = zin_ref[...]
        if conv_input:
            z = _dwconv3_rows(z, cwz_ref, cbz_ref)
        z_scr[...] = z
        zb_scr[...] = z.astype(BF16)
        acc_scr[...] = jnp.zeros_like(acc_scr)

    zb = zb_scr[...]
    p = jnp.dot(cf_ref[...], zb, preferred_element_type=F32)
    q = jnp.dot(sf_ref[...], zb, preferred_element_type=F32)
    kia = kia_ref[...]
    yr = (p * kr_ref[...] + q * kia).astype(BF16)
    yi = (q * krb_ref[...] - p * kia).astype(BF16)
    acc_scr[...] += (jnp.dot(ci_ref[...], yr, preferred_element_type=F32)
                     + jnp.dot(si_ref[...], yi, preferred_element_type=F32))

    @pl.when(kt == pl.num_programs(1) - 1)
    def _():
        gate = _dwconv3_rows(gate_ref[...], cwg_ref, cbg_ref)
        o_ref[...] = gate * (acc_scr[...] + z_scr[...] * d_ref[...])


def _hyena_order(zin, zin_col, proj, order, conv_w, conv_b, hy_d, spectrum, tables, tk):
    b, seq_len = proj.shape[0], proj.shape[1]
    cos_t, sin_f, sin_i = tables
    kr, kia, krb = spectrum
    hy0 = 5
    conv_input = order == 0
    cw = conv_w.reshape(3, 1 + HY_ORDER, W_B).transpose(1, 0, 2)
    cbias = conv_b.reshape(1 + HY_ORDER, 1, W_B)
    kern = functools.partial(_hyena_order_kernel, conv_input=conv_input)
    kspec = pl.BlockSpec((None, tk, W_B), lambda i, kt: (order, kt, 0))
    return pl.pallas_call(
        kern,
        grid=(b, seq_len // tk),
        in_specs=[pl.BlockSpec((None, seq_len, W_B), lambda i, kt: (i, 0, zin_col)),
                  pl.BlockSpec((None, seq_len, W_B), lambda i, kt: (i, 0, hy0 + 1 + order)),
                  pl.BlockSpec((None, 3, W_B), lambda i, kt: (0, 0, 0)),
                  pl.BlockSpec((None, 1, W_B), lambda i, kt: (0, 0, 0)),
                  pl.BlockSpec((None, 3, W_B), lambda i, kt: (1 + order, 0, 0)),
                  pl.BlockSpec((None, 1, W_B), lambda i, kt: (1 + order, 0, 0)),
                  pl.BlockSpec((None, 1, W_B), lambda i, kt: (order, 0, 0)),
                  kspec, kspec, kspec,
                  pl.BlockSpec((tk, seq_len), lambda i, kt: (kt, 0)),
                  pl.BlockSpec((tk, seq_len), lambda i, kt: (kt, 0)),
                  pl.BlockSpec((seq_len, tk), lambda i, kt: (0, kt)),
                  pl.BlockSpec((seq_len, tk), lambda i, kt: (0, kt))],
        out_specs=pl.BlockSpec((None, seq_len, W_B), lambda i, kt: (i, 0, 0)),
        out_shape=jax.ShapeDtypeStruct((b, seq_len, W_B), F32),
        scratch_shapes=[pltpu.VMEM((seq_len, W_B), F32), pltpu.VMEM((seq_len, W_B), BF16),
                        pltpu.VMEM((seq_len, W_B), F32)],
        compiler_params=_params("arbitrary", "arbitrary"),
    )(zin, proj, cw, cbias, cw, cbias, hy_d.reshape(HY_ORDER, 1, W_B), kr, kia, krb, cos_t, sin_f, cos_t, sin_i)


def _hyena_short_kernel(v_ref, x1_ref, x2_ref, cw_ref, cb_ref, d_ref, kr_ref, kia_ref, krb_ref,
                        cf_ref, sf_ref, si_ref, o_ref):
    cf, sf, si = cf_ref[...], sf_ref[...], si_ref[...]
    nb = v_ref.shape[0]
    zs = [_dwconv3_rows(v_ref[i], cw_ref.at[0], cb_ref.at[0]) for i in range(nb)]
    for order, gate_ref in enumerate((x1_ref, x2_ref)):
        zb = [z.astype(BF16) for z in zs]
        ps = [jnp.dot(cf, z, preferred_element_type=F32) for z in zb]
        qs = [jnp.dot(sf, z, preferred_element_type=F32) for z in zb]
        kia = kia_ref[order]
        yr = [(p * kr_ref[order] + q * kia).astype(BF16) for p, q in zip(ps, qs)]
        yi = [(q * krb_ref[order] - p * kia).astype(BF16) for p, q in zip(ps, qs)]
        conv = [jnp.dot(cf, r, preferred_element_type=F32) + jnp.dot(si, m, preferred_element_type=F32)
                for r, m in zip(yr, yi)]
        gates = [_dwconv3_rows(gate_ref[i], cw_ref.at[1 + order], cb_ref.at[1 + order]) for i in range(nb)]
        zs = [g * (c + z * d_ref[order]) for g, c, z in zip(gates, conv, zs)]
    for i in range(nb):
        o_ref[i] = zs[i]


def _hyena_short(proj, conv_w, conv_b, hy_d, spectrum, tables):
    b, seq_len = proj.shape[0], proj.shape[1]
    cos_t, sin_f, sin_i = tables
    nb = 2 if b % 2 == 0 else 1
    hy0 = 5
    cw = conv_w.reshape(3, 1 + HY_ORDER, W_B).transpose(1, 0, 2)
    cbias = conv_b.reshape(1 + HY_ORDER, 1, W_B)
    full = lambda a: pl.BlockSpec(a.shape, lambda i, _n=a.ndim: (0,) * _n)
    col = lambda j: pl.BlockSpec((nb, seq_len, W_B), lambda i: (i, 0, hy0 + j))
    consts = (cw, cbias, hy_d.reshape(HY_ORDER, 1, W_B)) + tuple(spectrum) + (cos_t, sin_f, sin_i)
    return pl.pallas_call(
        _hyena_short_kernel,
        grid=(b // nb,),
        in_specs=[col(0), col(1), col(2)] + [full(a) for a in consts],
        out_specs=pl.BlockSpec((nb, seq_len, W_B), lambda i: (i, 0, 0)),
        out_shape=jax.ShapeDtypeStruct((b, seq_len, W_B), F32),
        compiler_params=_params("arbitrary"),
    )(proj, proj, proj, *consts)


def _hyena_mixer(proj, conv_w, conv_b, hy_d, spectrum, tables, tk):
    if tk == proj.shape[1]:
        return _hyena_short(proj, conv_w, conv_b, hy_d, spectrum, tables)
    z = _hyena_order(proj, 5, proj, 0, conv_w, conv_b, hy_d, spectrum, tables, tk)
    return _hyena_order(z, 0, proj, 1, conv_w, conv_b, hy_d, spectrum, tables, tk)


def _diff_lambda(lp_ref, lam_init):
    lp = lp_ref[...]
    a = jnp.sum(lp[0:1] * lp[1:2], axis=-1, keepdims=True)
    b = jnp.sum(lp[2:3] * lp[3:4], axis=-1, keepdims=True)
    return jnp.exp(a) - jnp.exp(b) + lam_init


def _diff_attend(q, keys_b, vals_b, lam, stack):
    tq = q.shape[0]
    lane = lax.broadcasted_iota(jnp.int32, (1, 2 * DH_C), 1)
    qs = q * (DH_C ** -0.5 * LOG2E)

    def attend(qm):
        s = lax.dot_general(qm.astype(BF16), keys_b, NT_DIMS, preferred_element_type=F32)
        e = jnp.exp2(s - jnp.max(s, axis=-1, keepdims=True))
        den = jnp.sum(e, axis=-1, keepdims=True)
        return jnp.dot(e.astype(BF16), vals_b, preferred_element_type=F32) / den

    q_first, q_second = jnp.where(lane < DH_C, qs, 0.0), jnp.where(lane >= DH_C, qs, 0.0)
    if stack:
        o = attend(jnp.concatenate([q_first, q_second], axis=0))
        return o[:tq] - lam * o[tq:]
    return attend(q_first) - lam * attend(q_second)


def _attn_prompt_kernel(q_ref, k_ref, v_ref, lp_ref, norm_ref, o_ref, kc_ref, vc_ref, *, lam_init):
    hw = 2 * DH_C
    lam = _diff_lambda(lp_ref, lam_init)
    for h in range(N_HEADS):
        sl = slice(h * hw, (h + 1) * hw)
        k = k_ref[:, sl]
        v = v_ref[:, sl]
        o = _diff_attend(q_ref[:, sl], k.astype(BF16), v.astype(BF16), lam, stack=True)
        o_ref[:, sl] = _rms(o, norm_ref[:, sl]) * (1.0 - lam_init)
        kc_ref[h] = k
        vc_ref[h] = v


def _attn_prompt(proj, diff_lambda, diff_norm, lam_init):
    b, seq_len = proj.shape[0], proj.shape[1]
    hw = 2 * DH_C
    w = N_HEADS * hw
    kern = functools.partial(_attn_prompt_kernel, lam_init=lam_init)
    col = lambda j: pl.BlockSpec((None, seq_len, w), lambda i: (i, 0, j))
    cache_spec = pl.BlockSpec((None, None, N_HEADS, seq_len, hw), lambda i: (i, 0, 0, 0, 0))
    cache_shape = jax.ShapeDtypeStruct((b, 1, N_HEADS, seq_len, hw), F32)
    return pl.pallas_call(
        kern,
        grid=(b,),
        in_specs=[col(0), col(1), col(2),
                  pl.BlockSpec((4, DH_C), lambda i: (0, 0)),
                  pl.BlockSpec((1, w), lambda i: (0, 0))],
        out_specs=[pl.BlockSpec((None, seq_len, w), lambda i: (i, 0, 0)), cache_spec, cache_spec],
        out_shape=[jax.ShapeDtypeStruct((b, seq_len, w), F32), cache_shape, cache_shape],
        compiler_params=_params("arbitrary"),
    )(proj, proj, proj, diff_lambda, diff_norm.reshape(1, -1))


def _rope(x, cos, sin_signed):
    lane = lax.broadcasted_iota(jnp.int32, (1, x.shape[-1]), 1)
    first = (lane % 32) < 16
    partner = jnp.where(first, pltpu.roll(x, x.shape[-1] - 16, 1), pltpu.roll(x, 16, 1))
    return x * cos + partner * sin_signed


def _attn_sample_kernel(q_ref, k_ref, v_ref, ck_ref, cv_ref, cosq_ref, sinq_ref, cosk_ref, sink_ref,
                        lp_ref, norm_ref, o_ref, keys_scr, vals_scr, *, lam_init):
    past = ck_ref.shape[1]
    hw = 2 * DH_C

    @pl.when(pl.program_id(1) == 0)
    def _():
        for h in range(N_HEADS):
            sl = slice(h * hw, (h + 1) * hw)
            keys_scr[h, 0:past, :] = ck_ref[h].astype(BF16)
            vals_scr[h, 0:past, :] = cv_ref[h].astype(BF16)
            keys_scr[h, past:, :] = _rope(k_ref[:, sl], cosk_ref[...], sink_ref[...]).astype(BF16)
            vals_scr[h, past:, :] = v_ref[:, sl].astype(BF16)

    lam = _diff_lambda(lp_ref, lam_init)
    for h in range(N_HEADS):
        sl = slice(h * hw, (h + 1) * hw)
        q = _rope(q_ref[:, sl], cosq_ref[...], sinq_ref[...])
        o = _diff_attend(q, keys_scr[h], vals_scr[h], lam, stack=False)
        o_ref[:, sl] = _rms(o, norm_ref[:, sl]) * (1.0 - lam_init)


def _rope_tables(seq_len):
    pos = jnp.arange(seq_len)[:, None]
    half = DH_C // 2
    nfreq = half // 2
    inv = ROPE_BASE ** (-jnp.arange(0, half, 2, dtype=F32) / half)
    lane = jnp.arange(2 * DH_C)[None, :]
    p = jnp.where((lane % DH_C) < half, pos // GRID_W, pos % GRID_W).astype(F32)
    ang = p * inv[lane[0] % nfreq][None, :]
    return jnp.cos(ang), jnp.where((lane % half) < nfreq, -jnp.sin(ang), jnp.sin(ang))


def _attn_sample(proj, ctx_k, ctx_v, diff_lambda, diff_norm, lam_init, tq):
    b, seq_len = proj.shape[0], proj.shape[1]
    past = ctx_k.shape[2]
    hw = 2 * DH_C
    w = N_HEADS * hw
    cos, sin = _rope_tables(seq_len)
    kern = functools.partial(_attn_sample_kernel, lam_init=lam_init)
    ctx_spec = pl.BlockSpec((None, N_HEADS, past, hw), lambda i, j: (i, 0, 0, 0))
    return pl.pallas_call(
        kern,
        grid=(b, seq_len // tq),
        in_specs=[pl.BlockSpec((None, tq, w), lambda i, j: (i, j, 0)),
                  pl.BlockSpec((None, seq_len, w), lambda i, j: (i, 0, 1)),
                  pl.BlockSpec((None, seq_len, w), lambda i, j: (i, 0, 2)),
                  ctx_spec, ctx_spec,
                  pl.BlockSpec((tq, hw), lambda i, j: (j, 0)),
                  pl.BlockSpec((tq, hw), lambda i, j: (j, 0)),
                  pl.BlockSpec((seq_len, hw), lambda i, j: (0, 0)),
                  pl.BlockSpec((seq_len, hw), lambda i, j: (0, 0)),
                  pl.BlockSpec((4, DH_C), lambda i, j: (0, 0)),
                  pl.BlockSpec((1, w), lambda i, j: (0, 0))],
        out_specs=pl.BlockSpec((None, tq, w), lambda i, j: (i, j, 0)),
        out_shape=jax.ShapeDtypeStruct((b, seq_len, w), F32),
        scratch_shapes=[pltpu.VMEM((N_HEADS, past + seq_len, hw), BF16),
                        pltpu.VMEM((N_HEADS, past + seq_len, hw), BF16)],
        compiler_params=_params("arbitrary", "arbitrary"),
    )(proj, proj, proj, ctx_k, ctx_v, cos, sin, cos, sin, diff_lambda, diff_norm.reshape(1, -1))


def kernel(x_prompt, x_sample, state_hgrn, cache_diff_k, cache_diff_v, state_gla, c, c_ctx, ada_w, ada_b, norm_g, ffn_up, ffn_conv_w, ffn_conv_b, ffn_down, w_in_even, w_out_even, hgrn_lb, hgrn_norm, hy_conv_w, hy_conv_b, hy_w1, hy_b1, hy_w2, hy_b2, hy_w3, hy_freq, hy_d, w_in_odd, w_out_odd, diff_lambda, diff_norm, gla_aw, gla_ab, gla_norm):
    bp, lp, d = x_prompt.shape
    bs, ls, _ = x_sample.shape

    cvec_t = jnp.zeros((d, SUBLANES), F32).at[:, 0].set(c_ctx).at[:, 1:1 + bs].set(c.T)
    mod = _ada_mod(cvec_t, 1 + bs, ada_w, ada_b)

    yp = x_prompt.reshape(1, bp * lp, d)
    ys = x_sample
    tm = ROW_TILE

    filt = _hyena_filters((ls, lp), hy_w1[0], hy_b1[0], hy_w2[0], hy_b2[0], hy_w3[0], hy_freq[0])
    tab_p, tab_s = _dft_tables(lp), _dft_tables(ls)
    spec_s = _hyena_spectrum(filt, 0, ls, tab_s, SPECTRUM_FREQ_TILE)
    spec_p = _hyena_spectrum(filt, ls, lp, tab_p, lp)

    ffn_up_b, ffn_down_b = ffn_up.astype(BF16), ffn_down.astype(BF16)
    ffn_cb = ffn_conv_b.reshape(DEPTH, 1, -1)
    outs = {}
    for l in range(DEPTH):
        m = mod[l].reshape(SUBLANES, 6, 1, d)
        mp = [m[0:1, j] for j in range(6)]
        ms = [m[1:1 + bs, j] for j in range(6)]
        g = [norm_g[l, j].reshape(1, d) for j in range(4)]
        if l % 2 == 0:
            e = l // 2
            w_in = w_in_even[e].astype(BF16)
            w_out = w_out_even[e].astype(BF16)
            pp = _normmod_matmul(yp, g[0], mp[0], mp[1], w_in, tm).reshape(bp, lp, -1)
            ps = _normmod_matmul(ys, g[0], ms[0], ms[1], w_in, tm)
            oa_p, st_p = _hgrn_mixer(pp, hgrn_lb, hgrn_norm[e], None, l)
            oa_s, _ = _hgrn_mixer(ps, hgrn_lb, hgrn_norm[e], state_hgrn[:, e], l)
            ob_p = _hyena_mixer(pp, hy_conv_w[e], hy_conv_b[e], hy_d[e], spec_p, tab_p, lp)
            ob_s = _hyena_mixer(ps, hy_conv_w[e], hy_conv_b[e], hy_d[e], spec_s, tab_s, HYENA_FREQ_TILE)
            outs["hgrn"] = st_p
        else:
            o = l // 2
            lam_init = 0.8 - 0.6 * math.exp(-0.3 * l)
            pad_cols = -w_in_odd.shape[-1] % LANES
            w_in = jnp.pad(w_in_odd[o], ((0, 0), (0, pad_cols))).astype(BF16)
            w_out = w_out_odd[o].astype(BF16)
            kw = N_HEADS * DK_D
            aw = jnp.zeros((2, LANES, kw), F32)
            aw = aw.at[0, 0:GLA_RANK].set(gla_aw[o, 0]).at[1, GLA_RANK:2 * GLA_RANK].set(gla_aw[o, 1])
            aw_hi = aw.astype(BF16)
            aw_mid = (aw - aw_hi.astype(F32)).astype(BF16)
            aw = jnp.concatenate([aw_hi, aw_mid, aw_hi], axis=1)
            ab = gla_ab[o].reshape(2, 1, kw)
            s0 = state_gla[:, o]
            pp = _normmod_matmul(yp, g[0], mp[0], mp[1], w_in, tm).reshape(bp, lp, -1)
            ps = _normmod_matmul(ys, g[0], ms[0], ms[1], w_in, tm)
            oa_p, kc, vc = _attn_prompt(pp, diff_lambda[o], diff_norm[o], lam_init)
            oa_s = _attn_sample(ps, cache_diff_k[:, o], cache_diff_v[:, o], diff_lambda[o], diff_norm[o], lam_init, ATTN_Q_TILE)
            ob_p, st_p = _gla_mixer(pp, aw, ab, gla_norm[o], None)
            ob_s, _ = _gla_mixer(ps, aw, ab, gla_norm[o], s0)
            outs["k"], outs["v"], outs["gla"] = kc, vc, st_p
        ffn_args = (ffn_up_b, ffn_conv_w, ffn_cb, ffn_down_b, l)
        yp = _mix_ffn(yp, oa_p.reshape(1, bp * lp, -1), ob_p.reshape(1, bp * lp, -1), w_out, mp[2], g[1],
                      g[2], mp[3], mp[4], *ffn_args, mp[5], g[3], lp, tm, FFN_COL_TILE)
        ys = _mix_ffn(ys, oa_s, ob_s, w_out, ms[2], g[1], g[2], ms[3], ms[4], *ffn_args, ms[5], g[3], ls, tm, FFN_COL_TILE)

    return (yp.reshape(bp, lp, d), ys, outs["hgrn"], outs["k"], outs["v"], outs["gla"])
```

```python
import functools
import math

import jax
import jax.numpy as jnp
import numpy as np
from jax import lax
from jax.experimental import pallas as pl
from jax.experimental.pallas import tpu as pltpu

F32 = jnp.float32
BF16 = jnp.bfloat16
HIGHEST = lax.Precision.HIGHEST

D_MODEL = 1024
DEPTH = 2
GRID_W = 64
N_HEADS = 4
HEAD_W = 128
MIX_W = N_HEADS * HEAD_W
W_B = 512
HY_ORDER = 2
HY_EMB = 33
HY_BANDS = (HY_EMB - 1) // 2
HY_FF = 64
HY_TARGET = 1e-2
HY_FAST = 0.3
HY_SLOW = 1.5
DH_C = 64
DK_D = 64
GLA_RANK = 16
HEADS_PER_GROUP = 2
GLA_TAU = 16.0
ROPE_BASE = 10000.0
D_FF = 2816
EPS = 1e-6

LANES = 128
SUBLANES = 8
VMEM_LIMIT = 56 * 1024 * 1024
ROW_TILE = 512
FFN_COL_TILE = D_FF // 2
HYENA_FREQ_TILE = 256
SPECTRUM_FREQ_TILE = 512
ATTN_Q_TILE = 512
SCAN_CHUNK = 64
SCAN_ROWS = 256
SCAN_GROUP = 2
NT_DIMS = (((1,), (1,)), ((), ()))
LOG2E = 1.4426950408889634


def _params(*sem):
    return pltpu.CompilerParams(dimension_semantics=sem, vmem_limit_bytes=VMEM_LIMIT)


def _silu(x):
    return x * (1.0 / (1.0 + jnp.exp(-x)))


def _rms(x, g):
    return x * lax.rsqrt(jnp.mean(x * x, axis=-1, keepdims=True) + EPS) * g


def _ada_kernel(c_ref, w_ref, b_ref, o_ref, *, n_rows):
    s = _silu(c_ref[...])
    w = w_ref[...]
    rows = [jnp.sum(w * s[:, r:r + 1], axis=0, keepdims=True) for r in range(n_rows)]
    rows.append(jnp.zeros((SUBLANES - n_rows, w.shape[1]), F32))
    o_ref[...] = jnp.concatenate(rows, axis=0) + b_ref[...]


def _ada_mod(cvec_t, n_rows, ada_w, ada_b):
    n = ada_w.shape[-1]
    tn = 1536
    return pl.pallas_call(
        functools.partial(_ada_kernel, n_rows=n_rows),
        grid=(DEPTH, n // tn),
        in_specs=[pl.BlockSpec((D_MODEL, SUBLANES), lambda l, j: (0, 0)),
                  pl.BlockSpec((None, D_MODEL, tn), lambda l, j: (l, 0, j)),
                  pl.BlockSpec((None, 1, tn), lambda l, j: (l, 0, j))],
        out_specs=pl.BlockSpec((None, SUBLANES, tn), lambda l, j: (l, 0, j)),
        out_shape=jax.ShapeDtypeStruct((DEPTH, SUBLANES, n), F32),
        compiler_params=_params("arbitrary", "arbitrary"),
    )(cvec_t, ada_w, ada_b.reshape(DEPTH, 1, n))


def _normmod_matmul_kernel(x_ref, g_ref, sh_ref, sc_ref, w_ref, o_ref):
    h = _rms(x_ref[...], g_ref[...]) * (1.0 + sc_ref[...]) + sh_ref[...]
    o_ref[...] = jnp.dot(h.astype(BF16), w_ref[...], preferred_element_type=F32)


def _mod_index(n_mod):
    return (lambda b, i: (b, 0, 0)) if n_mod > 1 else (lambda b, i: (0, 0, 0))


def _normmod_matmul(x, g, shift, scale, w, tm):
    b, l, d = x.shape
    n = w.shape[1]
    mod_spec = pl.BlockSpec((None, 1, d), _mod_index(shift.shape[0]))
    return pl.pallas_call(
        _normmod_matmul_kernel,
        grid=(b, l // tm),
        in_specs=[pl.BlockSpec((None, tm, d), lambda b, i: (b, i, 0)),
                  pl.BlockSpec((1, d), lambda b, i: (0, 0)),
                  mod_spec, mod_spec,
                  pl.BlockSpec((d, n), lambda b, i: (0, 0))],
        out_specs=pl.BlockSpec((None, tm, n), lambda b, i: (b, i, 0)),
        out_shape=jax.ShapeDtypeStruct((b, l, n), F32),
        compiler_params=_params("arbitrary", "arbitrary"),
    )(x, g, shift, scale, w)


def _patch_rows(x, keep, starts):
    pieces, r = [], 0
    for s in starts:
        if s > r:
            pieces.append(x[r:s])
        pieces.append(jnp.where(keep[s:s + SUBLANES], x[s:s + SUBLANES], 0.0))
        r = s + SUBLANES
    if r < x.shape[0]:
        pieces.append(x[r:])
    return jnp.concatenate(pieces, axis=0)


def _mix_ffn_kernel(y_ref, yp_ref, yn_ref, a_ref, ap_ref, an_ref, b_ref, bp_ref, bn_ref, wo_ref, gate1_ref, g1_ref,
                    g2_ref, sh_ref, sc_ref, ua_ref, ug_ref, cwa_ref, cwg_ref, cba_ref, cbg_ref, dn_ref, gate2_ref,
                    g3_ref, o_ref, y1_scr, h_scr, acc_scr, *, seq_len):
    i = pl.program_id(1)
    f = pl.program_id(2)
    tm = y_ref.shape[0]
    halo = yp_ref.shape[0]
    half = a_ref.shape[1]

    @pl.when(f == 0)
    def _():
        rows_of = lambda p, m, n: jnp.concatenate([p[...], m[...], n[...]], axis=0)
        a_all = rows_of(ap_ref, a_ref, an_ref).astype(BF16)
        b_all = rows_of(bp_ref, b_ref, bn_ref).astype(BF16)
        m = jnp.dot(a_all, wo_ref[:half, :], preferred_element_type=F32)
        m = m + jnp.dot(b_all, wo_ref[half:, :], preferred_element_type=F32)
        y1 = rows_of(yp_ref, y_ref, yn_ref) + gate1_ref[...] * _rms(m, g1_ref[...])
        y1_scr[...] = y1[halo:halo + tm]
        h_scr[...] = (_rms(y1, g2_ref[...]) * (1.0 + sc_ref[...]) + sh_ref[...]).astype(BF16)
        acc_scr[...] = jnp.zeros_like(acc_scr)

    rows = tm + 2 * halo
    pos = (i * tm + lax.broadcasted_iota(jnp.int32, (tm, 1), 0)) % seq_len
    has_prev = pos != 0
    has_next = pos != seq_len - 1
    period = math.gcd(tm, seq_len)
    first_groups = list(range(0, tm, period))
    last_groups = [s + period - SUBLANES for s in first_groups]

    def conv(u_ref, cw_ref, cb_ref):
        u = jnp.dot(h_scr[...], u_ref[...], preferred_element_type=F32)
        up = _patch_rows(pltpu.roll(u, 1, 0)[halo:halo + tm], has_prev, first_groups)
        un = _patch_rows(pltpu.roll(u, rows - 1, 0)[halo:halo + tm], has_next, last_groups)
        uc = u[halo:halo + tm]
        return up * cw_ref[0:1, :] + uc * cw_ref[1:2, :] + un * cw_ref[2:3, :] + cb_ref[...]

    a = conv(ua_ref, cwa_ref, cba_ref)
    gt = conv(ug_ref, cwg_ref, cbg_ref)
    act = (_silu(gt) * a).astype(BF16)
    acc_scr[...] += jnp.dot(act, dn_ref[...], preferred_element_type=F32)

    @pl.when(f == pl.num_programs(2) - 1)
    def _():
        o_ref[...] = y1_scr[...] + gate2_ref[...] * _rms(acc_scr[...], g3_ref[...])


def _mix_ffn(y, a, bm, w_out, gate1, g1, g2, shift, scale, up, cw, cb, down, layer, gate2, g3, seq_len, tm, tf):
    b, l, d = y.shape
    wa = a.shape[-1]
    nf = D_FF // tf
    halo = SUBLANES
    hb = tm // halo
    last_hb = l // halo - 1
    n_mod = shift.shape[0]
    mod_spec = pl.BlockSpec((None, 1, d), (lambda b, i, f: (b, 0, 0)) if n_mod > 1 else (lambda b, i, f: (0, 0, 0)))
    vec = lambda off: pl.BlockSpec((None, 1, tf), lambda b, i, f: (layer, 0, off + f))
    row_d = pl.BlockSpec((1, d), lambda b, i, f: (0, 0))

    def tiles(w):
        return [pl.BlockSpec((None, tm, w), lambda b, i, f: (b, i, 0)),
                pl.BlockSpec((None, halo, w), lambda b, i, f: (b, jnp.maximum(i * hb - 1, 0), 0)),
                pl.BlockSpec((None, halo, w), lambda b, i, f: (b, jnp.minimum((i + 1) * hb, last_hb), 0))]

    kern = functools.partial(_mix_ffn_kernel, seq_len=seq_len)
    return pl.pallas_call(
        kern,
        grid=(b, l // tm, nf),
        in_specs=tiles(d) + tiles(wa) + tiles(wa) + [
                  pl.BlockSpec((2 * wa, d), lambda b, i, f: (0, 0)),
                  mod_spec, row_d,
                  row_d, mod_spec, mod_spec,
                  pl.BlockSpec((None, d, tf), lambda b, i, f: (layer, 0, f)),
                  pl.BlockSpec((None, d, tf), lambda b, i, f: (layer, 0, nf + f)),
                  pl.BlockSpec((None, 3, tf), lambda b, i, f: (layer, 0, f)),
                  pl.BlockSpec((None, 3, tf), lambda b, i, f: (layer, 0, nf + f)),
                  vec(0), vec(nf),
                  pl.BlockSpec((None, tf, d), lambda b, i, f: (layer, f, 0)),
                  mod_spec, row_d],
        out_specs=pl.BlockSpec((None, tm, d), lambda b, i, f: (b, i, 0)),
        out_shape=jax.ShapeDtypeStruct((b, l, d), F32),
        scratch_shapes=[pltpu.VMEM((tm, d), F32), pltpu.VMEM((tm + 2 * halo, d), BF16), pltpu.VMEM((tm, d), F32)],
        compiler_params=_params("arbitrary", "arbitrary", "arbitrary"),
    )(y, y, y, a, a, a, bm, bm, bm, w_out, gate1, g1, g2, shift, scale, up, up, cw, cw, cb, cb, down, gate2, g3)


def _scan_tables(c, heads_per_group):
    nlev = int(math.log2(c))
    t = np.arange(c)[:, None]
    r = np.arange(c)[None, :]
    masks_f = [np.eye(c, dtype=bool)]
    for lev in range(1, nlev + 1):
        bsz = 2 ** lev
        mid = (t // bsz) * bsz + bsz // 2
        masks_f.append(((t // bsz) == (r // bsz)) & (t >= mid) & (r < mid))
    m_f = np.stack(masks_f).astype(np.float32)
    m_b = np.transpose(m_f, (0, 2, 1))
    tri = lambda a: jnp.asarray(np.concatenate([a, a, a], axis=1).astype(np.float32), dtype=BF16)
    rep = lambda m: jnp.asarray(np.concatenate([m] * heads_per_group, axis=2))
    return tri(r <= t), tri(r >= t), rep(m_f), rep(m_b)


def _level_decay(cum, ncum, lg2, lev, bwd):
    c, w = cum.shape
    b = 1 << lev
    half = b // 2
    off = half - 1 + int(bwd)
    if b == 2:
        odd = lax.broadcasted_iota(jnp.int32, (c, 1), 0) % 2 == 1
        return jnp.where(odd != bwd, lg2, 0.0)
    if b >= 2 * SUBLANES:
        pieces = []
        for b0 in range(0, c, b):
            for rows, is_upper in ((slice(b0, b0 + half), False), (slice(b0 + half, b0 + b), True)):
                src = cum if is_upper != bwd else ncum
                pieces.append(src[rows] - jnp.broadcast_to(src[b0 + off:b0 + off + 1], (half, w)))
        return jnp.concatenate(pieces, axis=0)
    cum3 = cum.reshape(c // SUBLANES, SUBLANES, w)
    sub = lax.broadcasted_iota(jnp.int32, (1, SUBLANES, 1), 1)
    if b == SUBLANES:
        ref3 = jnp.broadcast_to(cum3[:, off:off + 1], cum3.shape)
    else:
        ref3 = jnp.where(sub < b, cum3[:, off:off + 1], cum3[:, b + off:b + off + 1])
    upper = (sub % b) >= half
    sgn = jnp.where(upper != bwd, 1.0, -1.0)
    return ((cum3 - ref3) * sgn).reshape(c, w)


def _head_stack(xb, width):
    n = xb.shape[1] // width
    lane = lax.broadcasted_iota(jnp.int32, (1, xb.shape[1]), 1)
    zero = jnp.zeros_like(xb)
    return jnp.concatenate([jnp.where((lane >= j * width) & (lane < (j + 1) * width), xb, zero) for j in range(n)],
                           axis=0)


def _scan_group(chunks):
    dk = chunks[0][0].shape[1] // N_HEADS
    hpg = HEADS_PER_GROUP
    gw = hpg * dk
    ngroups = N_HEADS // hpg
    vw = hpg * HEAD_W
    work = []
    for q, k, v, lg2, tri_ref, m_ref, st_ref, bwd in chunks:
        hi = lg2.astype(BF16)
        r1 = lg2 - hi.astype(F32)
        mid = r1.astype(BF16)
        lo = (r1 - mid.astype(F32)).astype(BF16)
        cum = jnp.dot(tri_ref[...], jnp.concatenate([hi, mid, lo], axis=0), preferred_element_type=F32)
        work.append(dict(q=q, k=k, v=v, lg2=lg2, cum=cum, ncum=-cum, m_ref=m_ref, st_ref=st_ref, bwd=bwd,
                         q16=q.astype(BF16), k16=k.astype(BF16), att=[None] * ngroups))
    nlev = chunks[0][5].shape[0] - 1
    for lev in range(1, nlev + 1):
        for w in work:
            e = jnp.exp2(_level_decay(w["cum"], w["ncum"], w["lg2"], lev, w["bwd"])).astype(BF16)
            qb, kb = w["q16"] * e, w["k16"] * e
            for g in range(ngroups):
                sl = slice(g * gw, (g + 1) * gw)
                prod = lax.dot_general(qb[:, sl], _head_stack(kb[:, sl], dk), NT_DIMS,
                                       preferred_element_type=F32)
                term = w["m_ref"][lev] * prod
                w["att"][g] = term if w["att"][g] is None else w["att"][g] + term
    lane = lax.broadcasted_iota(jnp.int32, (1, gw), 1)
    results = []
    for w in work:
        q, k, v, cum, st_ref = w["q"], w["k"], w["v"], w["cum"], w["st_ref"]
        c = q.shape[0]
        last = 0 if w["bwd"] else c - 1
        e_cum = jnp.exp2(cum)
        d_last = e_cum[last:last + 1, :]
        qe = (q * e_cum).astype(BF16)
        kd = (k * jnp.exp2(cum[last:last + 1, :] - cum)).astype(BF16)
        vb = v.astype(BF16)
        qk = q * k
        diag = []
        for h in range(N_HEADS):
            slab = qk[:, (h * dk // LANES) * LANES:(h * dk // LANES + max(dk // LANES, 1)) * LANES]
            if dk < LANES:
                slab_lane = lax.broadcasted_iota(jnp.int32, (1, LANES), 1)
                off = (h * dk) % LANES
                slab = jnp.where((slab_lane >= off) & (slab_lane < off + dk), slab, 0.0)
            diag.append(jnp.sum(slab, axis=1, keepdims=True) * v[:, h * HEAD_W:(h + 1) * HEAD_W])
        outs = []
        for g in range(ngroups):
            sl = slice(g * gw, (g + 1) * gw)
            st = st_ref[g]
            stb = st.astype(BF16)
            o_g = jnp.dot(w["att"][g].astype(BF16), _head_stack(vb[:, g * vw:(g + 1) * vw], HEAD_W),
                          preferred_element_type=F32)
            qe_g = qe[:, sl]
            zero = jnp.zeros_like(qe_g)
            inter = [lax.dot_general(jnp.where((lane >= j * dk) & (lane < (j + 1) * dk), qe_g, zero), stb, NT_DIMS,
                                     preferred_element_type=F32) for j in range(hpg)]
            outs.append(o_g + jnp.concatenate(inter, axis=1) + jnp.concatenate(diag[g * hpg:(g + 1) * hpg], axis=1))
            vstack = jnp.concatenate([v[:, g * vw + j * HEAD_W:g * vw + (j + 1) * HEAD_W] for j in range(hpg)],
                                     axis=0)
            st_ref[g] = st * d_last[:, sl] + jnp.dot(vstack.T.astype(BF16), _head_stack(kd[:, sl], dk),
                                                      preferred_element_type=F32)
        results.append(jnp.concatenate(outs, axis=1))
    return results


def _bidir_scan_body(load_fwd, load_bwd, g_ref, s0_ref, norm_ref, af_ref, ab_ref, mf_ref, mb_ref,
                     o_ref, s_out_ref, of_scr, ob_scr, stf_scr, stb_scr, seq_len):
    c = SCAN_CHUNK
    j = pl.program_id(1)
    nblk = pl.num_programs(1)
    n = SCAN_ROWS // c
    ngroups, _, gw = stf_scr.shape
    hpg = N_HEADS // ngroups
    dk = gw // hpg

    @pl.when(j == 0)
    def _():
        for g in range(ngroups):
            if s0_ref is None:
                stf_scr[g] = jnp.zeros((HEAD_W, gw), F32)
                stb_scr[g] = jnp.zeros((HEAD_W, gw), F32)
            else:
                stf_scr[g] = jnp.concatenate([s0_ref[0, g * hpg + j].T for j in range(hpg)], axis=1)
                stb_scr[g] = jnp.concatenate([s0_ref[1, g * hpg + j].T for j in range(hpg)], axis=1)

    base_f = j * SCAN_ROWS
    base_b = (nblk - 1 - j) * SCAN_ROWS

    def step(i, carry):
        chunks, stores = [], []
        for u in range(SCAN_GROUP):
            rf = pl.multiple_of((i * SCAN_GROUP + u) * c, c)
            chunks.append(load_fwd(rf) + (af_ref, mf_ref, stf_scr, False))
            stores.append((of_scr, pl.multiple_of(base_f + rf, c)))
        for u in range(SCAN_GROUP):
            rb = pl.multiple_of((n - 1 - i * SCAN_GROUP - u) * c, c)
            chunks.append(load_bwd(rb) + (ab_ref, mb_ref, stb_scr, True))
            stores.append((ob_scr, pl.multiple_of(base_b + rb, c)))
        for (scr, r0), o in zip(stores, _scan_group(chunks)):
            scr[pl.ds(r0, c), :] = o
        return carry

    lax.fori_loop(0, n // SCAN_GROUP, step, 0)

    @pl.when(j == nblk - 1)
    def _():
        def fin(jj, carry):
            r0 = pl.multiple_of(jj * SCAN_ROWS, SCAN_ROWS)
            o = of_scr[pl.ds(r0, SCAN_ROWS), :] + ob_scr[pl.ds(r0, SCAN_ROWS), :]
            parts = []
            for h in range(N_HEADS):
                sl = slice(h * HEAD_W, (h + 1) * HEAD_W)
                parts.append(_rms(o[:, sl], norm_ref[:, sl]))
            o_ref[pl.ds(r0, SCAN_ROWS), :] = jnp.concatenate(parts, axis=1) * _silu(g_ref[pl.ds(r0, SCAN_ROWS), :])
            return carry

        lax.fori_loop(0, seq_len // SCAN_ROWS, fin, 0)
        for h in range(N_HEADS):
            g, hs = h // hpg, slice((h % hpg) * dk, (h % hpg + 1) * dk)
            s_out_ref[0, h] = stf_scr[g][:, hs].T
            s_out_ref[1, h] = stb_scr[g][:, hs].T


def _hgrn_kernel(*refs, layer, has_s0, seq_len):
    qf_ref, ff_ref, if_ref, qb_ref, fb_ref, ib_ref, g_ref, lbraw_ref, norm_ref, af_ref, ab_ref, mf_ref, mb_ref = refs[:13]
    s0_ref = refs[13] if has_s0 else None
    o_ref, s_out_ref, of_scr, ob_scr, stf_scr, stb_scr = refs[13 + has_s0:]
    c = SCAN_CHUNK
    raw = lbraw_ref[...]
    ex = jnp.exp(raw - jnp.max(raw, axis=0, keepdims=True))
    sm = ex / jnp.sum(ex, axis=0, keepdims=True)
    lb = sm[0]
    for j in range(1, layer + 1):
        lb = lb + sm[j]

    def load(q_ref, f_ref, i_ref, lb_row):
        def fn(r0):
            q = _silu(q_ref[pl.ds(r0, c), :]) * (HEAD_W ** -0.5)
            sig = 1.0 / (1.0 + jnp.exp(-f_ref[pl.ds(r0, c), :]))
            f = lb_row + (1.0 - lb_row) * sig
            return q, 1.0 - f, i_ref[pl.ds(r0, c), :], jnp.log2(f)
        return fn

    _bidir_scan_body(load(qf_ref, ff_ref, if_ref, lb[0:1]), load(qb_ref, fb_ref, ib_ref, lb[1:2]), g_ref,
                     s0_ref, norm_ref, af_ref, ab_ref, mf_ref, mb_ref,
                     o_ref, s_out_ref, of_scr, ob_scr, stf_scr, stb_scr, seq_len)


def _gla_kernel(*refs, has_s0, seq_len):
    fwd_refs, bwd_refs = refs[0:4], refs[4:8]
    g_ref, aw_ref, ab_ref, norm_ref, af_ref, abk_ref, mf_ref, mb_ref = refs[8:16]
    s0_ref = refs[16] if has_s0 else None
    o_ref, s_out_ref, of_scr, ob_scr, stf_scr, stb_scr = refs[16 + has_s0:]
    c = SCAN_CHUNK

    def load(d, q_ref, k_ref, v_ref, da_ref):
        def fn(r0):
            q = q_ref[pl.ds(r0, c), :] * (DK_D ** -0.5)
            da = da_ref[pl.ds(r0, c), :]
            da_hi = da.astype(BF16)
            da_mid = (da - da_hi.astype(F32)).astype(BF16)
            xa = jnp.dot(jnp.concatenate([da_hi, da_hi, da_mid], axis=1), aw_ref[d],
                         preferred_element_type=F32) + ab_ref[d]
            la = (jnp.minimum(xa, 0.0) - jnp.log(1.0 + jnp.exp(-jnp.abs(xa)))) * (LOG2E / GLA_TAU)
            return q, k_ref[pl.ds(r0, c), :], v_ref[pl.ds(r0, c), :], la
        return fn

    _bidir_scan_body(load(0, *fwd_refs), load(1, *bwd_refs), g_ref,
                     s0_ref, norm_ref, af_ref, abk_ref, mf_ref, mb_ref,
                     o_ref, s_out_ref, of_scr, ob_scr, stf_scr, stb_scr, seq_len)


def _scan_call(kern, proj, streams, extra, s0, seq_len, dk):
    b = proj.shape[0]
    nblk = seq_len // SCAN_ROWS
    hpg = HEADS_PER_GROUP
    tabs = _scan_tables(SCAN_CHUNK, hpg)
    full = lambda a: pl.BlockSpec(a.shape, lambda i, j, _n=a.ndim: (0,) * _n)
    in_specs = []
    for cb, w, kind in streams:
        if kind == "f":
            in_specs.append(pl.BlockSpec((None, SCAN_ROWS, w), lambda i, j, _c=cb: (i, j, _c)))
        elif kind == "b":
            in_specs.append(pl.BlockSpec((None, SCAN_ROWS, w), lambda i, j, _c=cb: (i, nblk - 1 - j, _c)))
        else:
            in_specs.append(pl.BlockSpec((None, seq_len, w), lambda i, j, _c=cb: (i, 0, _c)))
    args = [proj] * len(streams)
    for a in tuple(extra) + tabs:
        in_specs.append(full(a))
        args.append(a)
    if s0 is not None:
        in_specs.append(pl.BlockSpec((None, 2, N_HEADS, dk, HEAD_W), lambda i, j: (i, 0, 0, 0, 0)))
        args.append(s0)
    return pl.pallas_call(
        kern,
        grid=(b, nblk),
        in_specs=in_specs,
        out_specs=[pl.BlockSpec((None, seq_len, MIX_W), lambda i, j: (i, 0, 0)),
                   pl.BlockSpec((None, None, 2, N_HEADS, dk, HEAD_W), lambda i, j: (i, 0, 0, 0, 0, 0))],
        out_shape=[jax.ShapeDtypeStruct((b, seq_len, MIX_W), F32),
                   jax.ShapeDtypeStruct((b, 1, 2, N_HEADS, dk, HEAD_W), F32)],
        scratch_shapes=[pltpu.VMEM((seq_len, MIX_W), F32), pltpu.VMEM((seq_len, MIX_W), F32),
                        pltpu.VMEM((N_HEADS // hpg, HEAD_W, hpg * dk), F32),
                        pltpu.VMEM((N_HEADS // hpg, HEAD_W, hpg * dk), F32)],
        compiler_params=_params("arbitrary", "arbitrary"),
    )(*args)


def _hgrn_mixer(proj, hgrn_lb, norm, s0, layer):
    seq_len = proj.shape[1]
    kern = functools.partial(_hgrn_kernel, layer=layer, has_s0=s0 is not None, seq_len=seq_len)
    streams = [(0, MIX_W, "f"), (1, MIX_W, "f"), (3, MIX_W, "f"),
               (0, MIX_W, "b"), (2, MIX_W, "b"), (3, MIX_W, "b"), (4, MIX_W, "w")]
    return _scan_call(kern, proj, streams, (hgrn_lb, norm.reshape(1, MIX_W)), s0, seq_len, HEAD_W)


def _gla_mixer(proj, aw, ab, norm, s0):
    seq_len = proj.shape[1]
    kern = functools.partial(_gla_kernel, has_s0=s0 is not None, seq_len=seq_len)
    kw = N_HEADS * DK_D
    streams = [(1536 // kw, kw, "f"), (1792 // kw, kw, "f"), (2048 // MIX_W, MIX_W, "f"), (3072 // LANES, LANES, "f"),
               (1536 // kw, kw, "b"), (1792 // kw, kw, "b"), (2048 // MIX_W, MIX_W, "b"), (3072 // LANES, LANES, "b"),
               (2560 // MIX_W, MIX_W, "w")]
    return _scan_call(kern, proj, streams, (aw, ab, norm.reshape(1, MIX_W)), s0, seq_len, DK_D)


def _dwconv3_rows(x, w_ref, b_ref):
    l = x.shape[0]
    row = lax.broadcasted_iota(jnp.int32, (l, 1), 0)
    xp = _patch_rows(pltpu.roll(x, 1, 0), row != 0, [0])
    xn = _patch_rows(pltpu.roll(x, l - 1, 0), row != l - 1, [l - SUBLANES])
    return xp * w_ref[0:1, :] + x * w_ref[1:2, :] + xn * w_ref[2:3, :] + b_ref[...]


def _hyena_filter_kernel(z_ref, w1_ref, b1_ref, w2_ref, b2_ref, w3_ref, fr_ref, dl_ref, o_ref):
    z = z_ref[...]
    fr = fr_ref[...]
    h = jnp.sin(fr * (jnp.dot(z, w1_ref[...], preferred_element_type=F32, precision=HIGHEST) + b1_ref[...]))
    h = jnp.sin(fr * (jnp.dot(h, w2_ref[...], preferred_element_type=F32, precision=HIGHEST) + b2_ref[...]))
    h_hi = h.astype(BF16)
    h_mid = (h - h_hi.astype(F32)).astype(BF16)
    h = jnp.dot(jnp.concatenate([h_hi, h_hi, h_mid], axis=1), w3_ref[...],
                preferred_element_type=F32)
    win = jnp.exp(-z[:, 0:1] * dl_ref[...])
    o_ref[...] = h * jnp.concatenate([win] * (2 * HY_ORDER), axis=1)


def _hyena_pos_features(l):
    t = jnp.linspace(0.0, 1.0, l, dtype=F32)[:, None]
    w = 2.0 * math.pi * jnp.arange(l, dtype=F32)[:, None] / l
    fb = jnp.linspace(1e-4, HY_BANDS - 1, HY_BANDS, dtype=F32)[None]
    z = jnp.concatenate([t, jnp.cos(fb * w), -jnp.sin(fb * w)], axis=-1)
    return jnp.pad(z, ((0, 0), (0, HY_FF - HY_EMB)))


def _hyena_filters(lens, w1, b1, w2, b2, w3, freq):
    z = jnp.concatenate([_hyena_pos_features(l) for l in lens], axis=0)
    rows = z.shape[0]
    tr = 256
    w1p = jnp.pad(w1, ((0, HY_FF - HY_EMB), (0, 0)))
    max_decay = math.log(HY_TARGET) / HY_FAST
    min_decay = math.log(HY_TARGET) / HY_SLOW
    deltas = jnp.abs(jnp.linspace(min_decay, max_decay, W_B, dtype=F32))[None]
    nout = w3.shape[1]
    w3_hi = w3.astype(BF16)
    w3_mid = (w3 - w3_hi.astype(F32)).astype(BF16)
    w3s = jnp.concatenate([w3_hi, w3_mid, w3_hi], axis=0)
    full = lambda a: pl.BlockSpec(a.shape, lambda i, _n=a.ndim: (0,) * _n)
    ins = (w1p, b1.reshape(1, -1), w2, b2.reshape(1, -1), w3s, freq.reshape(1, -1), deltas)
    return pl.pallas_call(
        _hyena_filter_kernel,
        grid=(rows // tr,),
        in_specs=[pl.BlockSpec((tr, HY_FF), lambda i: (i, 0))] + [full(a) for a in ins],
        out_specs=pl.BlockSpec((tr, nout), lambda i: (i, 0)),
        out_shape=jax.ShapeDtypeStruct((rows, nout), F32),
        compiler_params=_params("arbitrary"),
    )(z, *ins)


def _dft_table_kernel(ca_ref, sa_ref, cb_ref, sb_ref, cos_ref, sinf_ref, sini_ref):
    tk = ca_ref.shape[0]
    ca, sa = ca_ref[...], sa_ref[...]
    row = pl.program_id(0) * tk + lax.broadcasted_iota(jnp.int32, (tk, 1), 0)
    lane = lax.broadcasted_iota(jnp.int32, (1, LANES), 1)
    alt_row = jnp.where(row % 2 == 0, 1.0, -1.0)
    for grp in range(cb_ref.shape[1]):
        cols = slice(grp * LANES, (grp + 1) * LANES)
        cbg, sbg = cb_ref[:, grp:grp + 1], sb_ref[:, grp:grp + 1]
        sin_t = sa * cbg + ca * sbg
        col = grp * LANES + lane
        cos_ref[:, cols] = (ca * cbg - sa * sbg).astype(BF16)
        sinf_ref[:, cols] = jnp.where(row == 0, jnp.where(col % 2 == 0, 1.0, -1.0), sin_t).astype(BF16)
        sini_ref[:, cols] = jnp.where(col == 0, alt_row, sin_t).astype(BF16)


def _dft_tables(l):
    n = 2 * l
    k = jnp.arange(l, dtype=jnp.int32)[:, None]
    t1 = jnp.arange(LANES, dtype=jnp.int32)[None, :]
    t2 = (jnp.arange(l // LANES, dtype=jnp.int32) * LANES)[None, :]
    ang = lambda m: (m % n).astype(F32) * (2.0 * math.pi / n)
    small = (jnp.cos(ang(k * t1)), jnp.sin(ang(k * t1)), jnp.cos(ang(k * t2)), jnp.sin(ang(k * t2)))
    tk = min(l, 256)
    out = jax.ShapeDtypeStruct((l, l), BF16)
    ospec = pl.BlockSpec((tk, l), lambda i: (i, 0))
    return pl.pallas_call(
        _dft_table_kernel,
        grid=(l // tk,),
        in_specs=[pl.BlockSpec((tk, a.shape[1]), lambda i: (i, 0)) for a in small],
        out_specs=[ospec, ospec, ospec],
        out_shape=[out, out, out],
        compiler_params=_params("arbitrary"),
    )(*small)


def _hyena_spectrum_kernel(c_ref, s_ref, f0_ref, f1_ref, kr_ref, kia_ref, krb_ref, fs_scr, fd_scr, nyq_scr, *,
                           seq_len):
    kt = pl.program_id(1)
    tk = c_ref.shape[0]

    @pl.when(kt == 0)
    def _():
        row = lax.broadcasted_iota(jnp.int32, (seq_len, 1), 0)
        f0 = f0_ref[...]
        f1 = jnp.where(row != 0, f1_ref[...], 0.0)
        fsum = f0 + f1
        fs_scr[...] = fsum.astype(BF16)
        fd_scr[...] = (f1 - f0).astype(BF16)
        nyq_scr[...] = jnp.sum(jnp.where(row % 2 == 0, fsum, -fsum), axis=0, keepdims=True)

    kr = jnp.dot(c_ref[...], fs_scr[...], preferred_element_type=F32)
    ki = jnp.dot(s_ref[...], fd_scr[...], preferred_element_type=F32)
    krow = kt * tk + lax.broadcasted_iota(jnp.int32, (tk, 1), 0)
    dc = krow == 0
    wk = jnp.where(dc, 1.0, 2.0) * (1.0 / (2 * seq_len))
    kr_ref[...] = kr * wk
    kia_ref[...] = jnp.where(dc, 0.0, ki) * wk
    krb_ref[...] = jnp.where(dc, nyq_scr[...], kr) * wk


def _hyena_spectrum(filt, row0, seq_len, tables, tk):
    cos_t, sin_f, _ = tables
    rb = row0 // seq_len
    out = jax.ShapeDtypeStruct((HY_ORDER, seq_len, W_B), F32)
    kern = functools.partial(_hyena_spectrum_kernel, seq_len=seq_len)
    ospec = pl.BlockSpec((None, tk, W_B), lambda o, kt: (o, kt, 0))
    return pl.pallas_call(
        kern,
        grid=(HY_ORDER, seq_len // tk),
        in_specs=[pl.BlockSpec((tk, seq_len), lambda o, kt: (kt, 0)),
                  pl.BlockSpec((tk, seq_len), lambda o, kt: (kt, 0)),
                  pl.BlockSpec((seq_len, W_B), lambda o, kt: (rb, 2 * o)),
                  pl.BlockSpec((seq_len, W_B), lambda o, kt: (rb, 2 * o + 1))],
        out_specs=[ospec, ospec, ospec],
        out_shape=[out, out, out],
        scratch_shapes=[pltpu.VMEM((seq_len, W_B), BF16), pltpu.VMEM((seq_len, W_B), BF16),
                        pltpu.VMEM((1, W_B), F32)],
        compiler_params=_params("arbitrary", "arbitrary"),
    )(cos_t, sin_f, filt, filt)


def _hyena_order_kernel(zin_ref, gate_ref, cwz_ref, cbz_ref, cwg_ref, cbg_ref, d_ref, kr_ref, kia_ref, krb_ref,
                        cf_ref, sf_ref, ci_ref, si_ref, o_ref, z_scr, zb_scr, acc_scr, *, conv_input):
    kt = pl.program_id(1)

    @pl.when(kt == 0)
    def _():
        z = zin_ref[...]
        if conv_input:
            z = _dwconv3_rows(z, cwz_ref, cbz_ref)
        z_scr[...] = z
        zb_scr[...] = z.astype(BF16)
        acc_scr[...] = jnp.zeros_like(acc_scr)

    zb = zb_scr[...]
    p = jnp.dot(cf_ref[...], zb, preferred_element_type=F32)
    q = jnp.dot(sf_ref[...], zb, preferred_element_type=F32)
    kia = kia_ref[...]
    yr = (p * kr_ref[...] + q * kia).astype(BF16)
    yi = (q * krb_ref[...] - p * kia).astype(BF16)
    acc_scr[...] += (jnp.dot(ci_ref[...], yr, preferred_element_type=F32)
                     + jnp.dot(si_ref[...], yi, preferred_element_type=F32))

    @pl.when(kt == pl.num_programs(1) - 1)
    def _():
        gate = _dwconv3_rows(gate_ref[...], cwg_ref, cbg_ref)
        o_ref[...] = gate * (acc_scr[...] + z_scr[...] * d_ref[...])


def _hyena_order(zin, zin_col, proj, order, conv_w, conv_b, hy_d, spectrum, tables, tk):
    b, seq_len = proj.shape[0], proj.shape[1]
    cos_t, sin_f, sin_i = tables
    kr, kia, krb = spectrum
    hy0 = 5
    conv_input = order == 0
    cw = conv_w.reshape(3, 1 + HY_ORDER, W_B).transpose(1, 0, 2)
    cbias = conv_b.reshape(1 + HY_ORDER, 1, W_B)
    kern = functools.partial(_hyena_order_kernel, conv_input=conv_input)
    kspec = pl.BlockSpec((None, tk, W_B), lambda i, kt: (order, kt, 0))
    return pl.pallas_call(
        kern,
        grid=(b, seq_len // tk),
        in_specs=[pl.BlockSpec((None, seq_len, W_B), lambda i, kt: (i, 0, zin_col)),
                  pl.BlockSpec((None, seq_len, W_B), lambda i, kt: (i, 0, hy0 + 1 + order)),
                  pl.BlockSpec((None, 3, W_B), lambda i, kt: (0, 0, 0)),
                  pl.BlockSpec((None, 1, W_B), lambda i, kt: (0, 0, 0)),
                  pl.BlockSpec((None, 3, W_B), lambda i, kt: (1 + order, 0, 0)),
                  pl.BlockSpec((None, 1, W_B), lambda i, kt: (1 + order, 0, 0)),
                  pl.BlockSpec((None, 1, W_B), lambda i, kt: (order, 0, 0)),
                  kspec, kspec, kspec,
                  pl.BlockSpec((tk, seq_len), lambda i, kt: (kt, 0)),
                  pl.BlockSpec((tk, seq_len), lambda i, kt: (kt, 0)),
                  pl.BlockSpec((seq_len, tk), lambda i, kt: (0, kt)),
                  pl.BlockSpec((seq_len, tk), lambda i, kt: (0, kt))],
        out_specs=pl.BlockSpec((None, seq_len, W_B), lambda i, kt: (i, 0, 0)),
        out_shape=jax.ShapeDtypeStruct((b, seq_len, W_B), F32),
        scratch_shapes=[pltpu.VMEM((seq_len, W_B), F32), pltpu.VMEM((seq_len, W_B), BF16),
                        pltpu.VMEM((seq_len, W_B), F32)],
        compiler_params=_params("arbitrary", "arbitrary"),
    )(zin, proj, cw, cbias, cw, cbias, hy_d.reshape(HY_ORDER, 1, W_B), kr, kia, krb, cos_t, sin_f, cos_t, sin_i)


def _hyena_short_kernel(v_ref, x1_ref, x2_ref, cw_ref, cb_ref, d_ref, kr_ref, kia_ref, krb_ref,
                        cf_ref, sf_ref, si_ref, o_ref):
    cf, sf, si = cf_ref[...], sf_ref[...], si_ref[...]
    nb = v_ref.shape[0]
    zs = [_dwconv3_rows(v_ref[i], cw_ref.at[0], cb_ref.at[0]) for i in range(nb)]
    for order, gate_ref in enumerate((x1_ref, x2_ref)):
        zb = [z.astype(BF16) for z in zs]
        ps = [jnp.dot(cf, z, preferred_element_type=F32) for z in zb]
        qs = [jnp.dot(sf, z, preferred_element_type=F32) for z in zb]
        kia = kia_ref[order]
        yr = [(p * kr_ref[order] + q * kia).astype(BF16) for p, q in zip(ps, qs)]
        yi = [(q * krb_ref[order] - p * kia).astype(BF16) for p, q in zip(ps, qs)]
        conv = [jnp.dot(cf, r, preferred_element_type=F32) + jnp.dot(si, m, preferred_element_type=F32)
                for r, m in zip(yr, yi)]
        gates = [_dwconv3_rows(gate_ref[i], cw_ref.at[1 + order], cb_ref.at[1 + order]) for i in range(nb)]
        zs = [g * (c + z * d_ref[order]) for g, c, z in zip(gates, conv, zs)]
    for i in range(nb):
        o_ref[i] = zs[i]


def _hyena_short(proj, conv_w, conv_b, hy_d, spectrum, tables):
    b, seq_len = proj.shape[0], proj.shape[1]
    cos_t, sin_f, sin_i = tables
    nb = 2 if b % 2 == 0 else 1
    hy0 = 5
    cw = conv_w.reshape(3, 1 + HY_ORDER, W_B).transpose(1, 0, 2)
    cbias = conv_b.reshape(1 + HY_ORDER, 1, W_B)
    full = lambda a: pl.BlockSpec(a.shape, lambda i, _n=a.ndim: (0,) * _n)
    col = lambda j: pl.BlockSpec((nb, seq_len, W_B), lambda i: (i, 0, hy0 + j))
    consts = (cw, cbias, hy_d.reshape(HY_ORDER, 1, W_B)) + tuple(spectrum) + (cos_t, sin_f, sin_i)
    return pl.pallas_call(
        _hyena_short_kernel,
        grid=(b // nb,),
        in_specs=[col(0), col(1), col(2)] + [full(a) for a in consts],
        out_specs=pl.BlockSpec((nb, seq_len, W_B), lambda i: (i, 0, 0)),
        out_shape=jax.ShapeDtypeStruct((b, seq_len, W_B), F32),
        compiler_params=_params("arbitrary"),
    )(proj, proj, proj, *consts)


def _hyena_mixer(proj, conv_w, conv_b, hy_d, spectrum, tables, tk):
    if tk == proj.shape[1]:
        return _hyena_short(proj, conv_w, conv_b, hy_d, spectrum, tables)
    z = _hyena_order(proj, 5, proj, 0, conv_w, conv_b, hy_d, spectrum, tables, tk)
    return _hyena_order(z, 0, proj, 1, conv_w, conv_b, hy_d, spectrum, tables, tk)


def _diff_lambda(lp_ref, lam_init):
    lp = lp_ref[...]
    a = jnp.sum(lp[0:1] * lp[1:2], axis=-1, keepdims=True)
    b = jnp.sum(lp[2:3] * lp[3:4], axis=-1, keepdims=True)
    return jnp.exp(a) - jnp.exp(b) + lam_init


def _diff_attend(q, keys_b, vals_b, lam, stack):
    tq = q.shape[0]
    lane = lax.broadcasted_iota(jnp.int32, (1, 2 * DH_C), 1)
    qs = q * (DH_C ** -0.5 * LOG2E)

    def attend(qm):
        s = lax.dot_general(qm.astype(BF16), keys_b, NT_DIMS, preferred_element_type=F32)
        e = jnp.exp2(s - jnp.max(s, axis=-1, keepdims=True))
        den = jnp.sum(e, axis=-1, keepdims=True)
        return jnp.dot(e.astype(BF16), vals_b, preferred_element_type=F32) / den

    q_first, q_second = jnp.where(lane < DH_C, qs, 0.0), jnp.where(lane >= DH_C, qs, 0.0)
    if stack:
        o = attend(jnp.concatenate([q_first, q_second], axis=0))
        return o[:tq] - lam * o[tq:]
    return attend(q_first) - lam * attend(q_second)


def _attn_prompt_kernel(q_ref, k_ref, v_ref, lp_ref, norm_ref, o_ref, kc_ref, vc_ref, *, lam_init):
    hw = 2 * DH_C
    lam = _diff_lambda(lp_ref, lam_init)
    for h in range(N_HEADS):
        sl = slice(h * hw, (h + 1) * hw)
        k = k_ref[:, sl]
        v = v_ref[:, sl]
        o = _diff_attend(q_ref[:, sl], k.astype(BF16), v.astype(BF16), lam, stack=True)
        o_ref[:, sl] = _rms(o, norm_ref[:, sl]) * (1.0 - lam_init)
        kc_ref[h] = k
        vc_ref[h] = v


def _attn_prompt(proj, diff_lambda, diff_norm, lam_init):
    b, seq_len = proj.shape[0], proj.shape[1]
    hw = 2 * DH_C
    w = N_HEADS * hw
    kern = functools.partial(_attn_prompt_kernel, lam_init=lam_init)
    col = lambda j: pl.BlockSpec((None, seq_len, w), lambda i: (i, 0, j))
    cache_spec = pl.BlockSpec((None, None, N_HEADS, seq_len, hw), lambda i: (i, 0, 0, 0, 0))
    cache_shape = jax.ShapeDtypeStruct((b, 1, N_HEADS, seq_len, hw), F32)
    return pl.pallas_call(
        kern,
        grid=(b,),
        in_specs=[col(0), col(1), col(2),
                  pl.BlockSpec((4, DH_C), lambda i: (0, 0)),
                  pl.BlockSpec((1, w), lambda i: (0, 0))],
        out_specs=[pl.BlockSpec((None, seq_len, w), lambda i: (i, 0, 0)), cache_spec, cache_spec],
        out_shape=[jax.ShapeDtypeStruct((b, seq_len, w), F32), cache_shape, cache_shape],
        compiler_params=_params("arbitrary"),
    )(proj, proj, proj, diff_lambda, diff_norm.reshape(1, -1))


def _rope(x, cos, sin_signed):
    lane = lax.broadcasted_iota(jnp.int32, (1, x.shape[-1]), 1)
    first = (lane % 32) < 16
    partner = jnp.where(first, pltpu.roll(x, x.shape[-1] - 16, 1), pltpu.roll(x, 16, 1))
    return x * cos + partner * sin_signed


def _attn_sample_kernel(q_ref, k_ref, v_ref, ck_ref, cv_ref, cosq_ref, sinq_ref, cosk_ref, sink_ref,
                        lp_ref, norm_ref, o_ref, keys_scr, vals_scr, *, lam_init):
    past = ck_ref.shape[1]
    hw = 2 * DH_C

    @pl.when(pl.program_id(1) == 0)
    def _():
        for h in range(N_HEADS):
            sl = slice(h * hw, (h + 1) * hw)
            keys_scr[h, 0:past, :] = ck_ref[h].astype(BF16)
            vals_scr[h, 0:past, :] = cv_ref[h].astype(BF16)
            keys_scr[h, past:, :] = _rope(k_ref[:, sl], cosk_ref[...], sink_ref[...]).astype(BF16)
            vals_scr[h, past:, :] = v_ref[:, sl].astype(BF16)

    lam = _diff_lambda(lp_ref, lam_init)
    for h in range(N_HEADS):
        sl = slice(h * hw, (h + 1) * hw)
        q = _rope(q_ref[:, sl], cosq_ref[...], sinq_ref[...])
        o = _diff_attend(q, keys_scr[h], vals_scr[h], lam, stack=False)
        o_ref[:, sl] = _rms(o, norm_ref[:, sl]) * (1.0 - lam_init)


def _rope_tables(seq_len):
    pos = jnp.arange(seq_len)[:, None]
    half = DH_C // 2
    nfreq = half // 2
    inv = ROPE_BASE ** (-jnp.arange(0, half, 2, dtype=F32) / half)
    lane = jnp.arange(2 * DH_C)[None, :]
    p = jnp.where((lane % DH_C) < half, pos // GRID_W, pos % GRID_W).astype(F32)
    ang = p * inv[lane[0] % nfreq][None, :]
    return jnp.cos(ang), jnp.where((lane % half) < nfreq, -jnp.sin(ang), jnp.sin(ang))


def _attn_sample(proj, ctx_k, ctx_v, diff_lambda, diff_norm, lam_init, tq):
    b, seq_len = proj.shape[0], proj.shape[1]
    past = ctx_k.shape[2]
    hw = 2 * DH_C
    w = N_HEADS * hw
    cos, sin = _rope_tables(seq_len)
    kern = functools.partial(_attn_sample_kernel, lam_init=lam_init)
    ctx_spec = pl.BlockSpec((None, N_HEADS, past, hw), lambda i, j: (i, 0, 0, 0))
    return pl.pallas_call(
        kern,
        grid=(b, seq_len // tq),
        in_specs=[pl.BlockSpec((None, tq, w), lambda i, j: (i, j, 0)),
                  pl.BlockSpec((None, seq_len, w), lambda i, j: (i, 0, 1)),
                  pl.BlockSpec((None, seq_len, w), lambda i, j: (i, 0, 2)),
                  ctx_spec, ctx_spec,
                  pl.BlockSpec((tq, hw), lambda i, j: (j, 0)),
                  pl.BlockSpec((tq, hw), lambda i, j: (j, 0)),
                  pl.BlockSpec((seq_len, hw), lambda i, j: (0, 0)),
                  pl.BlockSpec((seq_len, hw), lambda i, j: (0, 0)),
                  pl.BlockSpec((4, DH_C), lambda i, j: (0, 0)),
                  pl.BlockSpec((1, w), lambda i, j: (0, 0))],
        out_specs=pl.BlockSpec((None, tq, w), lambda i, j: (i, j, 0)),
        out_shape=jax.ShapeDtypeStruct((b, seq_len, w), F32),
        scratch_shapes=[pltpu.VMEM((N_HEADS, past + seq_len, hw), BF16),
                        pltpu.VMEM((N_HEADS, past + seq_len, hw), BF16)],
        compiler_params=_params("arbitrary", "arbitrary"),
    )(proj, proj, proj, ctx_k, ctx_v, cos, sin, cos, sin, diff_lambda, diff_norm.reshape(1, -1))


def kernel(x_prompt, x_sample, state_hgrn, cache_diff_k, cache_diff_v, state_gla, c, c_ctx, ada_w, ada_b, norm_g, ffn_up, ffn_conv_w, ffn_conv_b, ffn_down, w_in_even, w_out_even, hgrn_lb, hgrn_norm, hy_conv_w, hy_conv_b, hy_w1, hy_b1, hy_w2, hy_b2, hy_w3, hy_freq, hy_d, w_in_odd, w_out_odd, diff_lambda, diff_norm, gla_aw, gla_ab, gla_norm):
    bp, lp, d = x_prompt.shape
    bs, ls, _ = x_sample.shape

    cvec_t = jnp.zeros((d, SUBLANES), F32).at[:, 0].set(c_ctx).at[:, 1:1 + bs].set(c.T)
    mod = _ada_mod(cvec_t, 1 + bs, ada_w, ada_b)

    yp = x_prompt.reshape(1, bp * lp, d)
    ys = x_sample
    tm = ROW_TILE

    filt = _hyena_filters((ls, lp), hy_w1[0], hy_b1[0], hy_w2[0], hy_b2[0], hy_w3[0], hy_freq[0])
    tab_p, tab_s = _dft_tables(lp), _dft_tables(ls)
    spec_s = _hyena_spectrum(filt, 0, ls, tab_s, SPECTRUM_FREQ_TILE)
    spec_p = _hyena_spectrum(filt, ls, lp, tab_p, lp)

    ffn_up_b, ffn_down_b = ffn_up.astype(BF16), ffn_down.astype(BF16)
    ffn_cb = ffn_conv_b.reshape(DEPTH, 1, -1)
    outs = {}
    for l in range(DEPTH):
        m = mod[l].reshape(SUBLANES, 6, 1, d)
        mp = [m[0:1, j] for j in range(6)]
        ms = [m[1:1 + bs, j] for j in range(6)]
        g = [norm_g[l, j].reshape(1, d) for j in range(4)]
        if l % 2 == 0:
            e = l // 2
            w_in = w_in_even[e].astype(BF16)
            w_out = w_out_even[e].astype(BF16)
            pp = _normmod_matmul(yp, g[0], mp[0], mp[1], w_in, tm).reshape(bp, lp, -1)
            ps = _normmod_matmul(ys, g[0], ms[0], ms[1], w_in, tm)
            oa_p, st_p = _hgrn_mixer(pp, hgrn_lb, hgrn_norm[e], None, l)
            oa_s, _ = _hgrn_mixer(ps, hgrn_lb, hgrn_norm[e], state_hgrn[:, e], l)
            ob_p = _hyena_mixer(pp, hy_conv_w[e], hy_conv_b[e], hy_d[e], spec_p, tab_p, lp)
            ob_s = _hyena_mixer(ps, hy_conv_w[e], hy_conv_b[e], hy_d[e], spec_s, tab_s, HYENA_FREQ_TILE)
            outs["hgrn"] = st_p
        else:
            o = l // 2
            lam_init = 0.8 - 0.6 * math.exp(-0.3 * l)
            pad_cols = -w_in_odd.shape[-1] % LANES
            w_in = jnp.pad(w_in_odd[o], ((0, 0), (0, pad_cols))).astype(BF16)
            w_out = w_out_odd[o].astype(BF16)
            kw = N_HEADS * DK_D
            aw = jnp.zeros((2, LANES, kw), F32)
            aw = aw.at[0, 0:GLA_RANK].set(gla_aw[o, 0]).at[1, GLA_RANK:2 * GLA_RANK].set(gla_aw[o, 1])
            aw_hi = aw.astype(BF16)
            aw_mid = (aw - aw_hi.astype(F32)).astype(BF16)
            aw = jnp.concatenate([aw_hi, aw_mid, aw_hi], axis=1)
            ab = gla_ab[o].reshape(2, 1, kw)
            s0 = state_gla[:, o]
            pp = _normmod_matmul(yp, g[0], mp[0], mp[1], w_in, tm).reshape(bp, lp, -1)
            ps = _normmod_matmul(ys, g[0], ms[0], ms[1], w_in, tm)
            oa_p, kc, vc = _attn_prompt(pp, diff_lambda[o], diff_norm[o], lam_init)
            oa_s = _attn_sample(ps, cache_diff_k[:, o], cache_diff_v[:, o], diff_lambda[o], diff_norm[o], lam_init, ATTN_Q_TILE)
            ob_p, st_p = _gla_mixer(pp, aw, ab, gla_norm[o], None)
            ob_s, _ = _gla_mixer(ps, aw, ab, gla_norm[o], s0)
            outs["k"], outs["v"], outs["gla"] = kc, vc, st_p
        ffn_args = (ffn_up_b, ffn_conv_w, ffn_cb, ffn_down_b, l)
        yp = _mix_ffn(yp, oa_p.reshape(1, bp * lp, -1), ob_p.reshape(1, bp * lp, -1), w_out, mp[2], g[1],
                      g[2], mp[3], mp[4], *ffn_args, mp[5], g[3], lp, tm, FFN_COL_TILE)
        ys = _mix_ffn(ys, oa_s, ob_s, w_out, ms[2], g[1], g[2], ms[3], ms[4], *ffn_args, ms[5], g[3], ls, tm, FFN_COL_TILE)

    return (yp.reshape(bp, lp, d), ys, outs["hgrn"], outs["k"], outs["v"], outs["gla"])
```
